```python
import math
import jax, jax.numpy as jnp
from jax import lax
import numpy as np

D_MODEL = 1024
BATCH = 8
SEQ = 2048
DEPTH = 4
DEC_BATCH = 128
DEC_SEQ = 4
PAST_LEN = 8192
PAGE_SIZE = 128

N_A_LAYERS = DEPTH // 2
N_B_LAYERS = DEPTH - N_A_LAYERS
MLSTM_HEADS = 8
MLSTM_DV = D_MODEL // MLSTM_HEADS
MLSTM_DQK = MLSTM_DV // 2
MLSTM_CHUNK = 64
MLSTM_IN = 2 * MLSTM_HEADS * MLSTM_DQK + 2 * MLSTM_HEADS * MLSTM_DV + 2 * MLSTM_HEADS
ATTN_HEAD_DIM = 64
ATTN_Q_HEADS = D_MODEL // ATTN_HEAD_DIM
ATTN_KV_HEADS = 4
ATTN_GROUP = ATTN_Q_HEADS // ATTN_KV_HEADS
WINDOW = 128
N_BUCKETS = 32
MAX_EXACT = N_BUCKETS // 2
MAX_DISTANCE = 128
D_FF = 4 * D_MODEL
EPS = 1e-6

kernel_name = "yoco_mlstm_swa_sink_hybrid_step"

F32 = jnp.float32


def rmsnorm(x, g):
    xf = x.astype(F32)
    y = xf * lax.rsqrt(jnp.mean(xf * xf, axis=-1, keepdims=True) + EPS)
    return (y * g.astype(F32)).astype(x.dtype)


def sq_relu_mlp(x, w_up, w_down):
    h = jax.nn.relu(x @ w_up)
    return (h * h) @ w_down


def mlstm_chunkwise(q, k, v, log_i, log_f, C0, n0, m0, chunk):
    B, S, H, _ = q.shape
    n_chunks = S // chunk

    def to_chunks(a):
        return a.reshape((B, n_chunks, chunk) + a.shape[2:]).swapaxes(0, 1)

    xs = tuple(to_chunks(a) for a in (q, k, v, log_i, log_f))
    causal = jnp.tril(jnp.ones((chunk, chunk), dtype=bool))

    def step(carry, inp):
        C, n, m = carry
        qc, kc, vc, lic, lfc = inp
        b = jnp.cumsum(lfc, axis=1).transpose(0, 2, 1)
        li = lic.transpose(0, 2, 1)
        dmat = b[..., :, None] - b[..., None, :] + li[..., None, :]
        dmat = jnp.where(causal, dmat, -jnp.inf)
        inter = b + m[..., None]
        m_t = jnp.maximum(inter, jnp.max(dmat, axis=-1))
        w_intra = jnp.exp(dmat - m_t[..., None])
        w_inter = jnp.exp(inter - m_t)
        a = jnp.einsum('blhd,bshd->bhls', qc, kc) * w_intra
        num = (jnp.einsum('bhls,bshe->blhe', a, vc)
               + jnp.einsum('blhd,bhde->blhe', qc, C) * w_inter.transpose(0, 2, 1)[..., None])
        den = jnp.sum(a, axis=-1) + w_inter * jnp.einsum('blhd,bhd->bhl', qc, n)
        scale = jnp.maximum(jnp.abs(den), jnp.exp(-m_t)).transpose(0, 2, 1)[..., None]
        h = num / scale
        b_last = b[..., -1]
        g = b_last[..., None] - b + li
        m_new = jnp.maximum(b_last + m, jnp.max(g, axis=-1))
        decay = jnp.exp(b_last + m - m_new)
        wk = jnp.exp(g - m_new[..., None])
        C_new = decay[..., None, None] * C + jnp.einsum('bhs,bshd,bshe->bhde', wk, kc, vc)
        n_new = decay[..., None] * n + jnp.einsum('bhs,bshd->bhd', wk, kc)
        return (C_new, n_new, m_new), h

    (C, n, m), hs = lax.scan(step, (C0, n0, m0), xs)
    h = hs.swapaxes(0, 1).reshape(B, S, H, v.shape[-1])
    return h, C, n, m


def mlstm_mixer(xn, w_in, b_i, b_f, head_gain, w_out, C0, n0, m0, chunk):
    B, S, _ = xn.shape
    H, DK, DV = MLSTM_HEADS, MLSTM_DQK, MLSTM_DV
    proj = (xn @ w_in).astype(F32)
    cuts = [H * DK, 2 * H * DK, 2 * H * DK + H * DV, 2 * H * DK + 2 * H * DV, 2 * H * DK + 2 * H * DV + H]
    q, k, v, o, ig, fg = jnp.split(proj, cuts, axis=-1)
    q = q.reshape(B, S, H, DK)
    k = k.reshape(B, S, H, DK) * (DK ** -0.5)
    v = v.reshape(B, S, H, DV)
    log_i = ig + b_i.astype(F32)
    log_f = jax.nn.log_sigmoid(fg + b_f.astype(F32))
    h, C, n, m = mlstm_chunkwise(q, k, v, log_i, log_f,
                                 C0.astype(F32), n0.astype(F32), m0.astype(F32), chunk)
    h = h * lax.rsqrt(jnp.mean(h * h, axis=-1, keepdims=True) + EPS) * head_gain.reshape(H, DV).astype(F32)
    out = (jax.nn.sigmoid(o) * h.reshape(B, S, H * DV)).astype(xn.dtype) @ w_out
    return out, C, n, m


def rel_bucket(dist):
    n = jnp.maximum(dist, 0)
    large = MAX_EXACT + (jnp.log(jnp.maximum(n, 1).astype(F32) / MAX_EXACT)
                         / math.log(MAX_DISTANCE / MAX_EXACT) * (N_BUCKETS - MAX_EXACT)).astype(jnp.int32)
    large = jnp.minimum(large, N_BUCKETS - 1)
    return jnp.where(n < MAX_EXACT, n, large)


def rel_bias_from_dist(dist, rel_bias):
    bias = rel_bias.astype(F32)[rel_bucket(dist)]
    Qn, Sn = dist.shape
    return bias.transpose(2, 0, 1).reshape(ATTN_KV_HEADS, ATTN_GROUP, Qn, Sn)


def window_attention(qg, kk, vv, bias, valid, sinks):
    s = jnp.einsum('...qkgd,...skd->...kgqs', qg.astype(F32), kk.astype(F32)) * (ATTN_HEAD_DIM ** -0.5) + bias
    s = jnp.where(valid, s, -jnp.inf)
    sink = sinks.astype(F32).reshape(ATTN_KV_HEADS, ATTN_GROUP, 1)
    m = jnp.maximum(jnp.max(s, axis=-1), sink)
    e = jnp.exp(s - m[..., None])
    p = e / (jnp.sum(e, axis=-1) + jnp.exp(sink - m))[..., None]
    return jnp.einsum('...kgqs,...skd->...qkgd', p, vv.astype(F32))


def trunk(x, C_in, n_in, m_in, k_past, v_past, mlstm_chunk,
          norm_mix, norm_ffn, w_mlstm_in, b_igate, b_fgate, mlstm_norm, w_mlstm_out,
          kv_norm, w_kv, w_q, attn_sinks, w_attn_out, rel_bias, w_up, w_down, final_norm):
    B, S, _ = x.shape
    Cs, ns, ms = [], [], []
    for l in range(DEPTH):
        if l < N_A_LAYERS:
            mix, C, n, m = mlstm_mixer(rmsnorm(x, norm_mix[l]), w_mlstm_in[l], b_igate[l], b_fgate[l],
                                       mlstm_norm[l], w_mlstm_out[l], C_in[l], n_in[l], m_in[l], mlstm_chunk)
            Cs.append(C.astype(C_in.dtype)); ns.append(n.astype(n_in.dtype)); ms.append(m.astype(m_in.dtype))
            x = x + mix.astype(x.dtype)
        else:
            if l == N_A_LAYERS:
                kv = rmsnorm(x, kv_norm) @ w_kv
                k_new, v_new = jnp.split(kv, 2, axis=-1)
                k_new = k_new.reshape(B, S, ATTN_KV_HEADS, ATTN_HEAD_DIM)
                v_new = v_new.reshape(B, S, ATTN_KV_HEADS, ATTN_HEAD_DIM)
                if k_past is None:
                    nb = S // WINDOW
                    kb = k_new.reshape(B, nb, WINDOW, ATTN_KV_HEADS, ATTN_HEAD_DIM)
                    vb = v_new.reshape(B, nb, WINDOW, ATTN_KV_HEADS, ATTN_HEAD_DIM)
                    kk = jnp.concatenate([jnp.concatenate([jnp.zeros_like(kb[:, :1]), kb[:, :-1]], 1), kb], 2)
                    vv = jnp.concatenate([jnp.concatenate([jnp.zeros_like(vb[:, :1]), vb[:, :-1]], 1), vb], 2)
                    dist = jnp.arange(WINDOW)[:, None] + WINDOW - jnp.arange(2 * WINDOW)[None, :]
                    keypos = (jnp.arange(nb)[:, None] - 1) * WINDOW + jnp.arange(2 * WINDOW)[None, :]
                    valid = ((dist >= 0) & (dist < WINDOW))[None] & (keypos >= 0)[:, None, :]
                    valid = valid[:, None, None]
                    win_k, win_v = k_new[:, -WINDOW:], v_new[:, -WINDOW:]
                else:
                    kk = jnp.concatenate([k_past.astype(k_new.dtype), k_new], axis=1)
                    vv = jnp.concatenate([v_past.astype(v_new.dtype), v_new], axis=1)
                    dist = jnp.arange(S)[:, None] + WINDOW - jnp.arange(WINDOW + S)[None, :]
                    valid = (dist >= 0) & (dist < WINDOW)
                    win_k, win_v = kk[:, -WINDOW:], vv[:, -WINDOW:]
                bias = rel_bias_from_dist(dist, rel_bias)
            j = l - N_A_LAYERS
            q = rmsnorm(x, norm_mix[l]) @ w_q[j]
            if k_past is None:
                qg = q.reshape(B, S // WINDOW, WINDOW, ATTN_KV_HEADS, ATTN_GROUP, ATTN_HEAD_DIM)
            else:
                qg = q.reshape(B, S, ATTN_KV_HEADS, ATTN_GROUP, ATTN_HEAD_DIM)
            o = window_attention(qg, kk, vv, bias, valid, attn_sinks[j]).reshape(B, S, ATTN_Q_HEADS * ATTN_HEAD_DIM)
            x = x + o.astype(x.dtype) @ w_attn_out[j]
        x = x + sq_relu_mlp(rmsnorm(x, norm_ffn[l]), w_up[l], w_down[l])
    return rmsnorm(x, final_norm), jnp.stack(Cs), jnp.stack(ns), jnp.stack(ms), win_k, win_v


def setup_inputs(seed: int = 0) -> dict:
    key = jax.random.key(seed)
    ks = jax.random.split(key, 24)
    nrm = jax.random.normal
    H, DK, DV = MLSTM_HEADS, MLSTM_DQK, MLSTM_DV
    HQD = ATTN_Q_HEADS * ATTN_HEAD_DIM
    return {
        "x_prompt": nrm(ks[0], (BATCH, SEQ, D_MODEL), F32),
        "x_sample": nrm(ks[1], (DEC_BATCH, DEC_SEQ, D_MODEL), F32),
        "state_C": 0.3 * nrm(ks[2], (N_A_LAYERS, DEC_BATCH, H, DK, DV), F32),
        "state_n": 0.3 * nrm(ks[3], (N_A_LAYERS, DEC_BATCH, H, DK), F32),
        "state_m": 0.5 * nrm(ks[4], (N_A_LAYERS, DEC_BATCH, H), F32),
        "cache_k": nrm(ks[5], (DEC_BATCH, WINDOW, ATTN_KV_HEADS, ATTN_HEAD_DIM), F32),
        "cache_v": nrm(ks[6], (DEC_BATCH, WINDOW, ATTN_KV_HEADS, ATTN_HEAD_DIM), F32),
        "norm_mix": 1.0 + 0.02 * nrm(ks[7], (DEPTH, D_MODEL), F32),
        "norm_ffn": 1.0 + 0.02 * nrm(ks[8], (DEPTH, D_MODEL), F32),
        "w_mlstm_in": nrm(ks[9], (N_A_LAYERS, D_MODEL, MLSTM_IN), F32) * D_MODEL ** -0.5,
        "b_igate": 0.1 * nrm(ks[10], (N_A_LAYERS, H), F32),
        "b_fgate": jnp.linspace(3.0, 6.0, H, dtype=F32)[None, :] + 0.1 * nrm(ks[11], (N_A_LAYERS, H), F32),
        "mlstm_norm": 1.0 + 0.02 * nrm(ks[12], (N_A_LAYERS, H * DV), F32),
        "w_mlstm_out": nrm(ks[13], (N_A_LAYERS, H * DV, D_MODEL), F32) * (H * DV) ** -0.5,
        "kv_norm": 1.0 + 0.02 * nrm(ks[14], (D_MODEL,), F32),
        "w_kv": nrm(ks[15], (D_MODEL, 2 * ATTN_KV_HEADS * ATTN_HEAD_DIM), F32) * D_MODEL ** -0.5,
        "w_q": nrm(ks[16], (N_B_LAYERS, D_MODEL, HQD), F32) * D_MODEL ** -0.5,
        "attn_sinks": 0.5 * nrm(ks[17], (N_B_LAYERS, ATTN_Q_HEADS), F32),
        "w_attn_out": nrm(ks[18], (N_B_LAYERS, HQD, D_MODEL), F32) * HQD ** -0.5,
        "rel_bias": 0.5 * nrm(ks[19], (N_BUCKETS, ATTN_Q_HEADS), F32),
        "w_up": nrm(ks[20], (DEPTH, D_MODEL, D_FF), F32) * D_MODEL ** -0.5,
        "w_down": nrm(ks[21], (DEPTH, D_FF, D_MODEL), F32) * D_FF ** -0.5,
        "final_norm": 1.0 + 0.02 * nrm(ks[22], (D_MODEL,), F32),
    }


def reference(x_prompt, x_sample, state_C, state_n, state_m, cache_k, cache_v,
              norm_mix, norm_ffn, w_mlstm_in, b_igate, b_fgate, mlstm_norm, w_mlstm_out,
              kv_norm, w_kv, w_q, attn_sinks, w_attn_out, rel_bias, w_up, w_down, final_norm):
    weights = (norm_mix, norm_ffn, w_mlstm_in, b_igate, b_fgate, mlstm_norm, w_mlstm_out,
               kv_norm, w_kv, w_q, attn_sinks, w_attn_out, rel_bias, w_up, w_down, final_norm)
    Bp = x_prompt.shape[0]
    C0 = jnp.zeros((N_A_LAYERS, Bp, MLSTM_HEADS, MLSTM_DQK, MLSTM_DV), state_C.dtype)
    n0 = jnp.zeros((N_A_LAYERS, Bp, MLSTM_HEADS, MLSTM_DQK), state_n.dtype)
    m0 = jnp.zeros((N_A_LAYERS, Bp, MLSTM_HEADS), state_m.dtype)
    y_prompt, C_p, n_p, m_p, k_p, v_p = trunk(x_prompt, C0, n0, m0, None, None, MLSTM_CHUNK, *weights)
    y_sample, C_s, n_s, m_s, k_s, v_s = trunk(x_sample, state_C, state_n, state_m, cache_k, cache_v,
                                              x_sample.shape[1], *weights)
    return (y_prompt, y_sample, C_p, n_p, m_p, k_p, v_p, C_s, n_s, m_s, k_s, v_s)
```

```python
import functools
import math

import jax
import jax.numpy as jnp
from jax import lax
from jax.experimental import pallas as pl
from jax.experimental.pallas import tpu as pltpu

F32 = jnp.float32
BF16 = jnp.bfloat16
EPS = 1e-6
NEG_INF = float("-inf")

N_HEADS = 8
DQK = 64
DV = 128
Q_HEADS = 16
KV_HEADS = 4
GROUP = Q_HEADS // KV_HEADS
HEAD_DIM = 64
WINDOW = 128
N_BUCKETS = 32
MAX_EXACT = N_BUCKETS // 2
MAX_DISTANCE = 128

LANES = 128
VMEM_LIMIT = 48 * 1024 * 1024

PROMPT_CHUNK = 64
PROMPT_BLOCK = 256
SAMPLE_BATCH_BLOCK = 8
ATTN_SAMPLE_BLOCK = 4


def _cparams(*sem):
    return pltpu.CompilerParams(dimension_semantics=sem, vmem_limit_bytes=VMEM_LIMIT)


def _rms(x, g):
    return x * lax.rsqrt(jnp.mean(x * x, axis=-1, keepdims=True) + EPS) * g


def _split3(x):
    hi = x.astype(BF16)
    r = x - hi.astype(F32)
    mid = r.astype(BF16)
    lo = (r - mid.astype(F32)).astype(BF16)
    return hi, mid, lo


def _dot01(x, onehot):
    hi, mid, lo = _split3(x)
    d = lambda a: jnp.dot(a, onehot, preferred_element_type=F32)
    return d(hi) + d(mid) + d(lo)


def _dot01_left(onehot, x):
    hi, mid, lo = _split3(x)
    d = lambda a: jnp.dot(onehot, a, preferred_element_type=F32)
    return d(hi) + d(mid) + d(lo)


def _log_sigmoid(x):
    return jnp.minimum(x, 0.0) - jnp.log1p(jnp.exp(-jnp.abs(x)))


def _sigmoid(x):
    return 1.0 / (1.0 + jnp.exp(-x))


def _norm_mm_kernel(x_ref, g_ref, w_ref, *out_refs, splits):
    y = _rms(x_ref[...], g_ref[...]).astype(BF16)
    r = jnp.dot(y, w_ref[...], preferred_element_type=F32)
    off = 0
    for o_ref, n in zip(out_refs, splits):
        o_ref[...] = r[:, off:off + n].astype(o_ref.dtype)
        off += n


def norm_matmul(x, g, w, splits, dtypes, tm):
    m, d = x.shape
    n = w.shape[1]
    assert sum(splits) == n and m % tm == 0
    return pl.pallas_call(
        functools.partial(_norm_mm_kernel, splits=splits),
        grid=(m // tm,),
        in_specs=[pl.BlockSpec((tm, d), lambda i: (i, 0)),
                  pl.BlockSpec((1, d), lambda i: (0, 0)),
                  pl.BlockSpec((d, n), lambda i: (0, 0))],
        out_specs=[pl.BlockSpec((tm, s), lambda i: (i, 0)) for s in splits],
        out_shape=[jax.ShapeDtypeStruct((m, s), dt) for s, dt in zip(splits, dtypes)],
        compiler_params=_cparams("parallel"),
        name="norm_matmul",
    )(x, g.reshape(1, d), w)


def _mm_res_kernel(a_ref, w_ref, x_ref, o_ref):
    o_ref[...] = x_ref[...] + jnp.dot(a_ref[...].astype(BF16), w_ref[...],
                                      preferred_element_type=F32)


def matmul_residual(a, w, x, tm):
    m, k = a.shape
    n = w.shape[1]
    return pl.pallas_call(
        _mm_res_kernel,
        grid=(m // tm,),
        in_specs=[pl.BlockSpec((tm, k), lambda i: (i, 0)),
                  pl.BlockSpec((k, n), lambda i: (0, 0)),
                  pl.BlockSpec((tm, n), lambda i: (i, 0))],
        out_specs=pl.BlockSpec((tm, n), lambda i: (i, 0)),
        out_shape=jax.ShapeDtypeStruct((m, n), F32),
        compiler_params=_cparams("parallel"),
        name="matmul_residual",
    )(a, w, x)


def _mlp_kernel(x_ref, g_ref, wu_ref, wd_ref, gf_ref, o_ref, xn_ref, acc_ref, *, final_norm):
    j = pl.program_id(1)

    @pl.when(j == 0)
    def _():
        x = x_ref[...]
        xn_ref[...] = _rms(x, g_ref[...]).astype(BF16)
        acc_ref[...] = x

    h = jnp.dot(xn_ref[...], wu_ref[...], preferred_element_type=F32)
    h = jnp.maximum(h, 0.0)
    acc_ref[...] += jnp.dot((h * h).astype(BF16), wd_ref[...], preferred_element_type=F32)

    @pl.when(j == pl.num_programs(1) - 1)
    def _():
        y = acc_ref[...]
        if final_norm:
            y = _rms(y, gf_ref[...])
        o_ref[...] = y


def mlp(x, g, w_up, w_down, g_final, final_norm, tm, tf):
    m, d = x.shape
    ff = w_up.shape[1]
    return pl.pallas_call(
        functools.partial(_mlp_kernel, final_norm=final_norm),
        grid=(m // tm, ff // tf),
        in_specs=[pl.BlockSpec((tm, d), lambda i, j: (i, 0)),
                  pl.BlockSpec((1, d), lambda i, j: (0, 0)),
                  pl.BlockSpec((d, tf), lambda i, j: (0, j)),
                  pl.BlockSpec((tf, d), lambda i, j: (j, 0)),
                  pl.BlockSpec((1, d), lambda i, j: (0, 0))],
        out_specs=pl.BlockSpec((tm, d), lambda i, j: (i, 0)),
        out_shape=jax.ShapeDtypeStruct((m, d), F32),
        scratch_shapes=[pltpu.VMEM((tm, d), BF16), pltpu.VMEM((tm, d), F32)],
        compiler_params=_cparams("parallel", "arbitrary"),
        name="mlp",
    )(x, g.reshape(1, d), w_up, w_down, g_final.reshape(1, d))


def _mlstm_prompt_kernel(q_ref, k_ref, v_ref, o_ref, gi_ref, gf_ref, bi_ref, bf_ref, gain_ref,
                         hg_ref, c_out_ref, n_out_ref, m_out_ref,
                         c_s, n_s, m_s, *, chunk):
    t_blk = q_ref.shape[1]
    n_chunks = t_blk // chunk
    step = pl.program_id(1)

    @pl.when(step == 0)
    def _():
        c_s[...] = jnp.zeros_like(c_s)
        n_s[...] = jnp.zeros_like(n_s)
        m_s[...] = jnp.zeros_like(m_s)

    li = gi_ref[0] + bi_ref[...]
    lf = _log_sigmoid(gf_ref[0] + bf_ref[...])
    row = lax.broadcasted_iota(jnp.int32, (t_blk, t_blk), 0)
    col = lax.broadcasted_iota(jnp.int32, (t_blk, t_blk), 1)
    tri = jnp.where((col <= row) & (row // chunk == col // chunk), 1.0, 0.0).astype(BF16)
    b = _dot01_left(tri, lf)
    c_t = (li - b).T
    ri = lax.broadcasted_iota(jnp.int32, (chunk, chunk), 0)
    ci = lax.broadcasted_iota(jnp.int32, (chunk, chunk), 1)
    causal = ci <= ri

    for c in range(n_chunks):
        r0 = c * chunk
        for h in range(N_HEADS):
            b_col = b[r0:r0 + chunk, h:h + 1]
            li_col = li[r0:r0 + chunk, h:h + 1]
            c_row = c_t[h:h + 1, r0:r0 + chunk]
            m_prev = m_s[h:h + 1, 0:1]
            dmat = jnp.where(causal, b_col + c_row, NEG_INF)
            inter = b_col + m_prev
            m_t = jnp.maximum(inter, jnp.max(dmat, axis=1, keepdims=True))
            w_intra = jnp.exp(dmat - m_t)
            w_inter = jnp.exp(inter - m_t)
            qh = q_ref[0, r0:r0 + chunk, h * DQK:(h + 1) * DQK]
            kh = k_ref[0, r0:r0 + chunk, h * DQK:(h + 1) * DQK]
            vh = v_ref[0, r0:r0 + chunk, h * DV:(h + 1) * DV]
            c_h = c_s[h]
            n_h = n_s[h:h + 1, :]
            s = lax.dot_general(qh, kh, (((1,), (1,)), ((), ())), preferred_element_type=F32)
            a = s * w_intra
            num = (jnp.dot(a.astype(BF16), vh, preferred_element_type=F32)
                   + jnp.dot(qh, c_h.astype(BF16), preferred_element_type=F32) * w_inter)
            qn = jnp.sum(qh.astype(F32) * n_h.astype(BF16).astype(F32), axis=1, keepdims=True)
            den = jnp.sum(a, axis=1, keepdims=True) + w_inter * qn
            scale = jnp.maximum(jnp.abs(den), jnp.exp(-m_t))
            hh = num / scale
            b_last = b[r0 + chunk - 1:r0 + chunk, h:h + 1]
            g_col = b_last - b_col + li_col
            m_new = jnp.maximum(b_last + m_prev, jnp.max(g_col, axis=0, keepdims=True))
            decay = jnp.exp(b_last + m_prev - m_new)
            wk = jnp.exp(g_col - m_new)
            kw = kh.astype(F32) * wk
            c_s[h] = decay * c_h + lax.dot_general(
                kw.astype(BF16), vh, (((0,), (0,)), ((), ())), preferred_element_type=F32)
            n_s[h:h + 1, :] = decay * n_h + jnp.sum(kw, axis=0, keepdims=True)
            m_s[h:h + 1, :] = jnp.broadcast_to(m_new, (1, LANES))
            hn = hh * lax.rsqrt(jnp.mean(hh * hh, axis=1, keepdims=True) + EPS)
            hn = hn * gain_ref[:, h * DV:(h + 1) * DV]
            og = _sigmoid(o_ref[0, r0:r0 + chunk, h * DV:(h + 1) * DV])
            hg_ref[0, r0:r0 + chunk, h * DV:(h + 1) * DV] = (og * hn).astype(hg_ref.dtype)

    @pl.when(step == pl.num_programs(1) - 1)
    def _():
        c_out_ref[0] = c_s[...]
        n_out_ref[0] = n_s[...]
        m_out_ref[0] = m_s[...]


def mlstm_prompt(q, k, v, o, gi, gf, b_i, b_f, gain, batch, seq):
    hq, hv = N_HEADS * DQK, N_HEADS * DV
    t = PROMPT_BLOCK
    r3 = lambda a: a.reshape(batch, seq, a.shape[-1])
    pad8 = lambda a: jnp.pad(a.reshape(1, N_HEADS), ((0, 0), (0, LANES - N_HEADS)))
    tok = lambda w: pl.BlockSpec((1, t, w), lambda bb, s: (bb, s, 0))
    cst = lambda w: pl.BlockSpec((1, w), lambda bb, s: (0, 0))
    hg, c_new, n_new, m_new = pl.pallas_call(
        functools.partial(_mlstm_prompt_kernel, chunk=PROMPT_CHUNK),
        grid=(batch, seq // t),
        in_specs=[tok(hq), tok(hq), tok(hv), tok(hv), tok(LANES), tok(LANES),
                  cst(LANES), cst(LANES), cst(hv)],
        out_specs=[tok(hv),
                   pl.BlockSpec((1, N_HEADS, DQK, DV), lambda bb, s: (bb, 0, 0, 0)),
                   pl.BlockSpec((1, N_HEADS, DQK), lambda bb, s: (bb, 0, 0)),
                   pl.BlockSpec((1, N_HEADS, LANES), lambda bb, s: (bb, 0, 0))],
        out_shape=[jax.ShapeDtypeStruct((batch, seq, hv), BF16),
                   jax.ShapeDtypeStruct((batch, N_HEADS, DQK, DV), F32),
                   jax.ShapeDtypeStruct((batch, N_HEADS, DQK), F32),
                   jax.ShapeDtypeStruct((batch, N_HEADS, LANES), F32)],
        scratch_shapes=[pltpu.VMEM((N_HEADS, DQK, DV), F32),
                        pltpu.VMEM((N_HEADS, DQK), F32),
                        pltpu.VMEM((N_HEADS, LANES), F32)],
        compiler_params=_cparams("parallel", "arbitrary"),
        name="mlstm_prompt",
    )(r3(q), r3(k), r3(v), r3(o), r3(gi), r3(gf), pad8(b_i), pad8(b_f), gain.reshape(1, hv))
    return hg.reshape(batch * seq, hv), c_new, n_new, m_new[:, :, 0]


def _mlstm_sample_kernel(q_ref, k_ref, v_ref, o_ref, gi_ref, gf_ref, m0_ref, n0_ref, c0_ref,
                         bi_ref, bf_ref, gain_ref, seg64_ref, seg128_ref, e64_ref, e128_ref,
                         hg_ref, c_out_ref, n_out_ref, m_out_ref,
                         qc_s, *, seq):
    rows = q_ref.shape[0]
    n_b = rows // seq
    tpos = lax.broadcasted_iota(jnp.int32, (rows, 1), 0) % seq

    def shift(x, d):
        return x if d == 0 else pltpu.roll(x, d, 0)

    def unshift(x, d):
        return x if d == 0 else pltpu.roll(x, rows - d, 0)

    seg64, seg128 = seg64_ref[...], seg128_ref[...]
    e64, e128 = e64_ref[...], e128_ref[...]

    li = gi_ref[...] + bi_ref[...]
    lf = _log_sigmoid(gf_ref[...] + bf_ref[...])
    b = lf
    for d in range(1, seq):
        b = b + jnp.where(tpos >= d, shift(lf, d), 0.0)
    m_prev = m0_ref[...]
    inter = b + m_prev
    dvals = []
    m_t = inter
    for d in range(seq):
        dd = jnp.where(tpos >= d, b - shift(b, d) + shift(li, d), NEG_INF)
        dvals.append(dd)
        m_t = jnp.maximum(m_t, dd)
    w_inter = jnp.exp(inter - m_t)
    w_intra = [jnp.exp(dd - m_t) for dd in dvals]

    q = q_ref[...].astype(BF16).astype(F32)
    k = k_ref[...].astype(BF16).astype(F32)
    v = v_ref[...].astype(BF16).astype(F32)

    den = jnp.zeros((rows, LANES), F32)
    num = jnp.zeros((rows, N_HEADS * DV), F32)
    for d in range(seq):
        a = _dot01(q * shift(k, d), seg64) * w_intra[d]
        den = den + a
        a_exp = jnp.dot(a.astype(BF16), e128, preferred_element_type=F32)
        num = num + a_exp * shift(v, d)

    last = lambda x: functools.reduce(
        lambda acc, d: jnp.where(tpos == seq - 1 - d, unshift(x, d), acc), range(1, seq), x)
    b_last = last(b)
    m_new = last(m_t)
    wk = jnp.exp(b_last - b + li - m_new)
    kw = k * _dot01(wk, e64)
    decay_exp = _dot01(w_inter, e128)

    ksum = kw
    for d in range(1, seq):
        ksum = ksum + shift(kw, d)
    n_rows = _dot01(w_inter, e64) * n0_ref[...] + ksum
    sel = (lax.broadcasted_iota(jnp.int32, (n_b, rows), 1)
           == lax.broadcasted_iota(jnp.int32, (n_b, rows), 0) * seq + (seq - 1))
    sel = jnp.where(sel, 1.0, 0.0).astype(BF16)
    n_out_ref[...] = _dot01_left(sel, n_rows)
    m_out_ref[...] = _dot01_left(sel, m_t)

    prow = lax.broadcasted_iota(jnp.int32, (2 * seq, 2 * DV), 0)
    pcol = lax.broadcasted_iota(jnp.int32, (2 * seq, 2 * DV), 1)
    own = (prow < seq) == (pcol < DV)
    first = lax.broadcasted_iota(jnp.int32, (2 * seq, DV), 0) < seq
    for p in range(n_b // 2):
        r0 = p * 2 * seq
        for h in range(N_HEADS):
            c_a = c0_ref[2 * p, h]
            c_b = c0_ref[2 * p + 1, h]
            q_pair = q[r0:r0 + 2 * seq, h * DQK:(h + 1) * DQK].astype(BF16)
            c_cat = jnp.concatenate([c_a, c_b], axis=1).astype(BF16)
            r = jnp.dot(q_pair, c_cat, preferred_element_type=F32)
            qc_s[r0:r0 + 2 * seq, h * DV:(h + 1) * DV] = jnp.where(first, r[:, :DV], r[:, DV:])
            v_pair = v[r0:r0 + 2 * seq, h * DV:(h + 1) * DV]
            v2 = jnp.where(own, jnp.concatenate([v_pair, v_pair], axis=1), 0.0).astype(BF16)
            kw_pair = kw[r0:r0 + 2 * seq, h * DQK:(h + 1) * DQK].astype(BF16)
            d_c = lax.dot_general(kw_pair, v2, (((0,), (0,)), ((), ())),
                                  preferred_element_type=F32)
            dec_a = decay_exp[r0 + seq - 1:r0 + seq, h * DV:(h + 1) * DV]
            dec_b = decay_exp[r0 + 2 * seq - 1:r0 + 2 * seq, h * DV:(h + 1) * DV]
            c_out_ref[2 * p, h] = dec_a * c_a + d_c[:, :DV]
            c_out_ref[2 * p + 1, h] = dec_b * c_b + d_c[:, DV:]

    qn = _dot01(q * n0_ref[...].astype(BF16).astype(F32), seg64)
    den = den + w_inter * qn
    num = num + qc_s[...] * decay_exp
    inv_scale = 1.0 / jnp.maximum(jnp.abs(den), jnp.exp(-m_t))
    hh = num * _dot01(inv_scale, e128)
    ms = _dot01(hh * hh, seg128) * (1.0 / DV)
    hn = hh * _dot01(lax.rsqrt(ms + EPS), e128) * gain_ref[...]
    hg_ref[...] = (_sigmoid(o_ref[...]) * hn).astype(hg_ref.dtype)


def mlstm_sample(q, k, v, o, gi, gf, b_i, b_f, gain, c0, n0, m0, batch, seq):
    hq, hv = N_HEADS * DQK, N_HEADS * DV
    rows = SAMPLE_BATCH_BLOCK * seq
    pad8 = lambda a: jnp.pad(a.reshape(1, N_HEADS), ((0, 0), (0, LANES - N_HEADS)))
    m0_rows = jnp.pad(jnp.repeat(m0, seq, axis=0), ((0, 0), (0, LANES - N_HEADS)))
    n0_rows = jnp.repeat(n0.reshape(batch, hq), seq, axis=0)
    lane = jnp.arange(LANES)
    seg64 = (jnp.arange(hq)[:, None] // DQK == lane[None, :]).astype(BF16)
    seg128 = (jnp.arange(hv)[:, None] // DV == lane[None, :]).astype(BF16)
    tok = lambda w: pl.BlockSpec((rows, w), lambda i: (i, 0))
    cst = lambda a: pl.BlockSpec(a.shape, lambda i: (0,) * a.ndim)
    consts = [pad8(b_i), pad8(b_f), gain.reshape(1, hv), seg64, seg128, seg64.T, seg128.T]
    hg, c_new, n_new, m_new = pl.pallas_call(
        functools.partial(_mlstm_sample_kernel, seq=seq),
        grid=(batch // SAMPLE_BATCH_BLOCK,),
        in_specs=[tok(hq), tok(hq), tok(hv), tok(hv), tok(LANES), tok(LANES), tok(LANES), tok(hq),
                  pl.BlockSpec((SAMPLE_BATCH_BLOCK, N_HEADS, DQK, DV), lambda i: (i, 0, 0, 0))]
                 + [cst(a) for a in consts],
        out_specs=[tok(hv),
                   pl.BlockSpec((SAMPLE_BATCH_BLOCK, N_HEADS, DQK, DV), lambda i: (i, 0, 0, 0)),
                   pl.BlockSpec((SAMPLE_BATCH_BLOCK, hq), lambda i: (i, 0)),
                   pl.BlockSpec((SAMPLE_BATCH_BLOCK, LANES), lambda i: (i, 0))],
        out_shape=[jax.ShapeDtypeStruct((batch * seq, hv), BF16),
                   jax.ShapeDtypeStruct((batch, N_HEADS, DQK, DV), F32),
                   jax.ShapeDtypeStruct((batch, hq), F32),
                   jax.ShapeDtypeStruct((batch, LANES), F32)],
        scratch_shapes=[pltpu.VMEM((rows, hv), F32)],
        compiler_params=_cparams("parallel"),
        name="mlstm_sample",
    )(q, k, v, o, gi, gf, m0_rows, n0_rows, c0, *consts)
    return hg, c_new, n_new.reshape(batch, N_HEADS, DQK), m_new[:, :N_HEADS]


def _bias_table_kernel(rb_ref, o_ref):
    _, nq, ns = o_ref.shape
    dist = (lax.broadcasted_iota(jnp.int32, (nq, ns), 0) + WINDOW
            - lax.broadcasted_iota(jnp.int32, (nq, ns), 1))
    n = jnp.maximum(dist, 0)
    large = MAX_EXACT + (jnp.log(jnp.maximum(n, 1).astype(F32) / MAX_EXACT)
                         / math.log(MAX_DISTANCE / MAX_EXACT) * (N_BUCKETS - MAX_EXACT)).astype(jnp.int32)
    large = jnp.minimum(large, N_BUCKETS - 1)
    bucket = jnp.where(n < MAX_EXACT, n, large)
    valid = (dist >= 0) & (dist < WINDOW)
    for h in range(Q_HEADS):
        acc = jnp.zeros((nq, ns), F32)
        for bkt in range(N_BUCKETS):
            acc = jnp.where(bucket == bkt, rb_ref[bkt, h], acc)
        o_ref[h] = jnp.where(valid, acc, NEG_INF)


def bias_table(rel_bias, nq, ns):
    return pl.pallas_call(
        _bias_table_kernel,
        in_specs=[pl.BlockSpec(memory_space=pltpu.SMEM)],
        out_specs=pl.BlockSpec(memory_space=pltpu.VMEM),
        out_shape=jax.ShapeDtypeStruct((Q_HEADS, nq, ns), F32),
        name="bias_table",
    )(rel_bias)


def _softmax_sink_pv(s, sink_col, vv):
    m = jnp.maximum(jnp.max(s, axis=1, keepdims=True), sink_col)
    e = jnp.exp(s - m)
    denom = jnp.sum(e, axis=1, keepdims=True) + jnp.exp(sink_col - m)
    return jnp.dot(e.astype(BF16), vv, preferred_element_type=F32) / denom


def _sink_column(sink_ref, kh, rows_per_head):
    g = lax.broadcasted_iota(jnp.int32, (GROUP * rows_per_head, 1), 0) // rows_per_head
    col = jnp.full((GROUP * rows_per_head, 1), sink_ref[kh * GROUP + GROUP - 1], F32)
    for gg in range(GROUP - 1):
        col = jnp.where(g == gg, sink_ref[kh * GROUP + gg], col)
    return col


def _attn_prompt_kernel(sink_ref, q_ref, kp_ref, kc_ref, vp_ref, vc_ref, bias_ref, o_ref):
    blk = pl.program_id(1)
    lim = jnp.where(blk == 0, WINDOW, 0)
    col = lax.broadcasted_iota(jnp.int32, (GROUP * WINDOW, 2 * WINDOW), 1)
    for kh in range(KV_HEADS):
        ks = slice(kh * HEAD_DIM, (kh + 1) * HEAD_DIM)
        kk = jnp.concatenate([kp_ref[0, :, ks], kc_ref[0, :, ks]], axis=0).astype(BF16)
        vv = jnp.concatenate([vp_ref[0, :, ks], vc_ref[0, :, ks]], axis=0).astype(BF16)
        qs = jnp.concatenate(
            [q_ref[0, :, (kh * GROUP + g) * HEAD_DIM:(kh * GROUP + g + 1) * HEAD_DIM]
             for g in range(GROUP)], axis=0)
        s = lax.dot_general(qs, kk, (((1,), (1,)), ((), ())), preferred_element_type=F32)
        s = s * (HEAD_DIM ** -0.5) + bias_ref[kh]
        s = jnp.where(col < lim, NEG_INF, s)
        o = _softmax_sink_pv(s, _sink_column(sink_ref, kh, WINDOW), vv)
        for g in range(GROUP):
            hd = (kh * GROUP + g) * HEAD_DIM
            o_ref[0, :, hd:hd + HEAD_DIM] = o[g * WINDOW:(g + 1) * WINDOW].astype(o_ref.dtype)


def attn_prompt(q, k, v, table, sinks, batch, seq):
    d = Q_HEADS * HEAD_DIM
    dk = KV_HEADS * HEAD_DIM
    nb = seq // WINDOW
    r3 = lambda a: a.reshape(batch, seq, a.shape[-1])
    prev = pl.BlockSpec((1, WINDOW, dk), lambda b, i: (b, jnp.maximum(i - 1, 0), 0))
    cur = pl.BlockSpec((1, WINDOW, dk), lambda b, i: (b, i, 0))
    out = pl.pallas_call(
        _attn_prompt_kernel,
        grid=(batch, nb),
        in_specs=[pl.BlockSpec(memory_space=pltpu.SMEM),
                  pl.BlockSpec((1, WINDOW, d), lambda b, i: (b, i, 0)),
                  prev, cur, prev, cur,
                  pl.BlockSpec((KV_HEADS, GROUP * WINDOW, 2 * WINDOW), lambda b, i: (0, 0, 0))],
        out_specs=pl.BlockSpec((1, WINDOW, d), lambda b, i: (b, i, 0)),
        out_shape=jax.ShapeDtypeStruct((batch, seq, d), BF16),
        compiler_params=_cparams("parallel", "arbitrary"),
        name="attn_prompt",
    )(sinks, r3(q), r3(k), r3(k), r3(v), r3(v),
      table.reshape(KV_HEADS, GROUP * WINDOW, 2 * WINDOW))
    return out.reshape(batch * seq, d)


def _attn_sample_kernel(sink_ref, q_ref, kc_ref, kn_ref, vc_ref, vn_ref, bias_ref, o_ref, *, seq):
    n_b = q_ref.shape[0]
    pad = jnp.zeros((WINDOW - seq, HEAD_DIM), F32)
    for b in range(n_b):
        for kh in range(KV_HEADS):
            ks = slice(kh * HEAD_DIM, (kh + 1) * HEAD_DIM)
            kk = jnp.concatenate([kc_ref[b, :, ks], kn_ref[b, :, ks], pad], axis=0).astype(BF16)
            vv = jnp.concatenate([vc_ref[b, :, ks], vn_ref[b, :, ks], pad], axis=0).astype(BF16)
            qs = jnp.concatenate(
                [q_ref[b, :, (kh * GROUP + g) * HEAD_DIM:(kh * GROUP + g + 1) * HEAD_DIM]
                 for g in range(GROUP)], axis=0).astype(BF16)
            s = lax.dot_general(qs, kk, (((1,), (1,)), ((), ())), preferred_element_type=F32)
            s = s * (HEAD_DIM ** -0.5) + bias_ref[kh]
            o = _softmax_sink_pv(s, _sink_column(sink_ref, kh, seq), vv)
            for g in range(GROUP):
                hd = (kh * GROUP + g) * HEAD_DIM
                o_ref[b, :, hd:hd + HEAD_DIM] = o[g * seq:(g + 1) * seq].astype(o_ref.dtype)


def attn_sample(q, k_new, v_new, cache_k, cache_v, table, sinks, batch, seq):
    d = Q_HEADS * HEAD_DIM
    dk = KV_HEADS * HEAD_DIM
    nbk = ATTN_SAMPLE_BLOCK
    new = pl.BlockSpec((nbk, seq, dk), lambda i: (i, 0, 0))
    old = pl.BlockSpec((nbk, WINDOW, dk), lambda i: (i, 0, 0))
    out = pl.pallas_call(
        functools.partial(_attn_sample_kernel, seq=seq),
        grid=(batch // nbk,),
        in_specs=[pl.BlockSpec(memory_space=pltpu.SMEM),
                  pl.BlockSpec((nbk, seq, d), lambda i: (i, 0, 0)),
                  old, new, old, new,
                  pl.BlockSpec((KV_HEADS, GROUP * seq, 2 * WINDOW), lambda i: (0, 0, 0))],
        out_specs=pl.BlockSpec((nbk, seq, d), lambda i: (i, 0, 0)),
        out_shape=jax.ShapeDtypeStruct((batch, seq, d), F32),
        compiler_params=_cparams("parallel"),
        name="attn_sample",
    )(sinks, q.reshape(batch, seq, d), cache_k.reshape(batch, WINDOW, dk),
      k_new.reshape(batch, seq, dk), cache_v.reshape(batch, WINDOW, dk),
      v_new.reshape(batch, seq, dk), table.reshape(KV_HEADS, GROUP * seq, 2 * WINDOW))
    return out.reshape(batch * seq, d)


def _trunk(x, state, cache, w, tm, tm_mlp):
    batch, seq, d = x.shape
    x = x.reshape(batch * seq, d)
    hq, hv = N_HEADS * DQK, N_HEADS * DV
    depth = w["norm_mix"].shape[0]
    n_a = w["w_in"].shape[0]
    act = BF16 if state is None else F32
    cs, ns, ms = [], [], []
    for l in range(depth):
        if l < n_a:
            q, k, v, o, gi, gf = norm_matmul(
                x, w["norm_mix"][l], w["w_in"][l], (hq, hq, hv, hv, LANES, LANES),
                (act, act, act, F32, F32, F32), tm)
            args = (q, k, v, o, gi, gf, w["b_igate"][l], w["b_fgate"][l], w["mlstm_norm"][l])
            if state is None:
                hg, c_new, n_new, m_new = mlstm_prompt(*args, batch, seq)
            else:
                hg, c_new, n_new, m_new = mlstm_sample(*args, state[0][l], state[1][l], state[2][l],
                                                       batch, seq)
            cs.append(c_new); ns.append(n_new); ms.append(m_new)
            x = matmul_residual(hg, w["w_mlstm_out"][l], x, tm)
        else:
            j = l - n_a
            if j == 0:
                k_new, v_new = norm_matmul(x, w["kv_norm"], w["w_kv"],
                                           (KV_HEADS * HEAD_DIM,) * 2, (F32, F32), tm)
                if cache is None:
                    table = bias_table(w["rel_bias"], WINDOW, 2 * WINDOW)
                else:
                    table = bias_table(w["rel_bias"], seq, 2 * WINDOW)
            (q,) = norm_matmul(x, w["norm_mix"][l], w["w_q"][j], (Q_HEADS * HEAD_DIM,), (act,), tm)
            if cache is None:
                o = attn_prompt(q, k_new, v_new, table, w["attn_sinks"][j], batch, seq)
            else:
                o = attn_sample(q, k_new, v_new, cache[0], cache[1], table, w["attn_sinks"][j],
                                batch, seq)
            x = matmul_residual(o, w["w_attn_out"][j], x, tm)
        x = mlp(x, w["norm_ffn"][l], w["w_up"][l], w["w_down"][l], w["final_norm"],
                l == depth - 1, tm_mlp, 512)
    dk = KV_HEADS * HEAD_DIM
    k3 = k_new.reshape(batch, seq, dk)
    v3 = v_new.reshape(batch, seq, dk)
    if cache is None:
        win_k, win_v = k3[:, -WINDOW:], v3[:, -WINDOW:]
    else:
        win_k = jnp.concatenate([cache[0].reshape(batch, WINDOW, dk)[:, seq:], k3], axis=1)
        win_v = jnp.concatenate([cache[1].reshape(batch, WINDOW, dk)[:, seq:], v3], axis=1)
    shp = (batch, WINDOW, KV_HEADS, HEAD_DIM)
    return (x.reshape(batch, seq, d), jnp.stack(cs), jnp.stack(ns), jnp.stack(ms),
            win_k.reshape(shp), win_v.reshape(shp))


def kernel(x_prompt, x_sample, state_C, state_n, state_m, cache_k, cache_v, norm_mix, norm_ffn,
           w_mlstm_in, b_igate, b_fgate, mlstm_norm, w_mlstm_out, kv_norm, w_kv, w_q, attn_sinks,
           w_attn_out, rel_bias, w_up, w_down, final_norm):
    hq, hv = N_HEADS * DQK, N_HEADS * DV
    w_main = w_mlstm_in[:, :, :2 * hq + 2 * hv]
    col_scale = jnp.concatenate([jnp.ones((hq,), F32), jnp.full((hq,), DQK ** -0.5, F32),
                                 jnp.ones((2 * hv,), F32)])
    lane_pad = ((0, 0), (0, 0), (0, LANES - N_HEADS))
    w_gi = jnp.pad(w_mlstm_in[:, :, 2 * hq + 2 * hv:2 * hq + 2 * hv + N_HEADS], lane_pad)
    w_gf = jnp.pad(w_mlstm_in[:, :, 2 * hq + 2 * hv + N_HEADS:], lane_pad)
    w_in = jnp.concatenate([w_main * col_scale, w_gi, w_gf], axis=-1).astype(BF16)
    w = dict(norm_mix=norm_mix, norm_ffn=norm_ffn, w_in=w_in, b_igate=b_igate, b_fgate=b_fgate,
             mlstm_norm=mlstm_norm, w_mlstm_out=w_mlstm_out.astype(BF16), kv_norm=kv_norm,
             w_kv=w_kv.astype(BF16), w_q=w_q.astype(BF16), attn_sinks=attn_sinks,
             w_attn_out=w_attn_out.astype(BF16), rel_bias=rel_bias, w_up=w_up.astype(BF16),
             w_down=w_down.astype(BF16), final_norm=final_norm)
    y_p, c_p, n_p, m_p, k_p, v_p = _trunk(x_prompt, None, None, w, 512, 1024)
    y_s, c_s, n_s, m_s, k_s, v_s = _trunk(x_sample, (state_C, state_n, state_m),
                                          (cache_k, cache_v), w, 512, 512)
    return (y_p, y_s, c_p, n_p, m_p, k_p, v_p, c_s, n_s, m_s, k_s, v_s)
```

```python
import functools
import math

import jax
import jax.numpy as jnp
from jax import lax
from jax.experimental import pallas as pl
from jax.experimental.pallas import tpu as pltpu

F32 = jnp.float32
BF16 = jnp.bfloat16
EPS = 1e-6
NEG_INF = float("-inf")

N_HEADS = 8
DQK = 64
DV = 128
Q_HEADS = 16
KV_HEADS = 4
GROUP = Q_HEADS // KV_HEADS
HEAD_DIM = 64
WINDOW = 128
N_BUCKETS = 32
MAX_EXACT = N_BUCKETS // 2
MAX_DISTANCE = 128

LANES = 128
VMEM_LIMIT = 48 * 1024 * 1024

PROMPT_CHUNK = 128
PROMPT_BLOCK = 256
SAMPLE_BATCH_BLOCK = 8
ATTN_SAMPLE_BLOCK = 4


def _cparams(*sem):
    return pltpu.CompilerParams(dimension_semantics=sem, vmem_limit_bytes=VMEM_LIMIT)


def _rms(x, g):
    return x * lax.rsqrt(jnp.mean(x * x, axis=-1, keepdims=True) + EPS) * g


def _split3(x):
    hi = x.astype(BF16)
    r = x - hi.astype(F32)
    mid = r.astype(BF16)
    lo = (r - mid.astype(F32)).astype(BF16)
    return hi, mid, lo


def _dot01(x, onehot):
    hi, mid, lo = _split3(x)
    d = lambda a: jnp.dot(a, onehot, preferred_element_type=F32)
    return d(hi) + d(mid) + d(lo)


def _dot01_left(onehot, x):
    hi, mid, lo = _split3(x)
    d = lambda a: jnp.dot(onehot, a, preferred_element_type=F32)
    return d(hi) + d(mid) + d(lo)


def _log_sigmoid(x):
    return jnp.minimum(x, 0.0) - jnp.log1p(jnp.exp(-jnp.abs(x)))


def _sigmoid(x):
    return 1.0 / (1.0 + jnp.exp(-x))


def _norm_mm_kernel(x_ref, g_ref, w_ref, *out_refs, splits):
    y = _rms(x_ref[...], g_ref[...]).astype(BF16)
    r = jnp.dot(y, w_ref[...], preferred_element_type=F32)
    for o_ref, (off, n) in zip(out_refs, splits):
        o_ref[...] = r[:, off:off + n].astype(o_ref.dtype)


def _layer_spec(arr, layer):
    idx = (layer,) + (0,) * (arr.ndim - 1)
    return pl.BlockSpec((None,) + arr.shape[1:], lambda *_: idx)


def norm_matmul(x, g, lg, w, lw, splits, dtypes, tm):
    m, d = x.shape
    assert m % tm == 0 and all(off + n <= w.shape[-1] for off, n in splits)
    return pl.pallas_call(
        functools.partial(_norm_mm_kernel, splits=splits),
        grid=(m // tm,),
        in_specs=[pl.BlockSpec((tm, d), lambda i: (i, 0)),
                  _layer_spec(g, lg), _layer_spec(w, lw)],
        out_specs=[pl.BlockSpec((tm, n), lambda i: (i, 0)) for _, n in splits],
        out_shape=[jax.ShapeDtypeStruct((m, n), dt) for (_, n), dt in zip(splits, dtypes)],
        compiler_params=_cparams("parallel"),
        name="norm_matmul",
    )(x, g, w)


def _mlp_kernel(a_ref, wo_ref, x_ref, g_ref, wu_ref, wd_ref, gf_ref, o_ref, xn_ref, acc_ref, *,
                final_norm):
    j = pl.program_id(1)

    @pl.when(j == 0)
    def _():
        x = x_ref[...] + jnp.dot(a_ref[...].astype(BF16), wo_ref[...], preferred_element_type=F32)
        xn_ref[...] = _rms(x, g_ref[...]).astype(BF16)
        acc_ref[...] = x

    h = jnp.dot(xn_ref[...], wu_ref[...], preferred_element_type=F32)
    h = jnp.maximum(h, 0.0)
    acc_ref[...] += jnp.dot((h * h).astype(BF16), wd_ref[...], preferred_element_type=F32)

    @pl.when(j == pl.num_programs(1) - 1)
    def _():
        y = acc_ref[...]
        if final_norm:
            y = _rms(y, gf_ref[...])
        o_ref[...] = y


def mlp(a, w_o, lo, x, g, w_up, w_down, l, g_final, final_norm, tm, tf):
    m, d = x.shape
    ff = w_up.shape[-1]
    return pl.pallas_call(
        functools.partial(_mlp_kernel, final_norm=final_norm),
        grid=(m // tm, ff // tf),
        in_specs=[pl.BlockSpec((tm, d), lambda i, j: (i, 0)),
                  _layer_spec(w_o, lo),
                  pl.BlockSpec((tm, d), lambda i, j: (i, 0)),
                  _layer_spec(g, l),
                  pl.BlockSpec((None, d, tf), lambda i, j: (l, 0, j)),
                  pl.BlockSpec((None, tf, d), lambda i, j: (l, j, 0)),
                  pl.BlockSpec((1, d), lambda i, j: (0, 0))],
        out_specs=pl.BlockSpec((tm, d), lambda i, j: (i, 0)),
        out_shape=jax.ShapeDtypeStruct((m, d), F32),
        scratch_shapes=[pltpu.VMEM((tm, d), BF16), pltpu.VMEM((tm, d), F32)],
        compiler_params=_cparams("parallel", "arbitrary"),
        name="mlp",
    )(a, w_o, x, g, w_up, w_down, g_final.reshape(1, d))


def _mlstm_inproj_kernel(x_ref, g_ref, w_ref, wkt_ref, q_ref, kt_ref, v_ref, o_ref, gi_ref, gf_ref):
    hq, hv = N_HEADS * DQK, N_HEADS * DV
    y = _rms(x_ref[...], g_ref[...]).astype(BF16)
    r = jnp.dot(y, w_ref[...], preferred_element_type=F32)
    q_ref[...] = r[:, :hq].astype(q_ref.dtype)
    v_ref[...] = r[:, hq:hq + hv].astype(v_ref.dtype)
    o_ref[...] = r[:, hq + hv:hq + 2 * hv]
    gi_ref[...] = r[:, hq + 2 * hv:hq + 2 * hv + LANES]
    gf_ref[...] = r[:, hq + 2 * hv + LANES:]
    kt_ref[...] = lax.dot_general(wkt_ref[...], y, (((1,), (1,)), ((), ())),
                                  preferred_element_type=F32).astype(kt_ref.dtype)


def mlstm_inproj(x, g, w, wkt, l, tm):
    m, d = x.shape
    hq, hv = N_HEADS * DQK, N_HEADS * DV
    row = lambda width: pl.BlockSpec((tm, width), lambda i: (i, 0))
    return pl.pallas_call(
        _mlstm_inproj_kernel,
        grid=(m // tm,),
        in_specs=[row(d), _layer_spec(g, l), _layer_spec(w, l), _layer_spec(wkt, l)],
        out_specs=[row(hq), pl.BlockSpec((hq, tm), lambda i: (0, i)), row(hv), row(hv),
                   row(LANES), row(LANES)],
        out_shape=[jax.ShapeDtypeStruct((m, hq), BF16), jax.ShapeDtypeStruct((hq, m), BF16),
                   jax.ShapeDtypeStruct((m, hv), BF16), jax.ShapeDtypeStruct((m, hv), F32),
                   jax.ShapeDtypeStruct((m, LANES), F32), jax.ShapeDtypeStruct((m, LANES), F32)],
        compiler_params=_cparams("parallel"),
        name="mlstm_inproj",
    )(x, g, w, wkt)


def _chunk_scan(x, pos, op, fill, length):
    k = 1
    while k < length:
        x = op(x, jnp.where(pos >= k, pltpu.roll(x, k, 0), fill))
        k *= 2
    return x


def _mlstm_prompt_kernel(q_ref, kt_ref, v_ref, o_ref, gi_ref, gf_ref, bi_ref, bf_ref, gain_ref,
                         hg_ref, cx_out_ref, m_out_ref, cx_s, m_s, *, chunk):
    t_blk = q_ref.shape[1]
    n_chunks = t_blk // chunk
    step = pl.program_id(1)

    @pl.when(step == 0)
    def _():
        cx_s[...] = jnp.zeros_like(cx_s)
        m_s[...] = jnp.zeros_like(m_s)

    li = gi_ref[0] + bi_ref[...]
    lf = _log_sigmoid(gf_ref[0] + bf_ref[...])
    pos = lax.broadcasted_iota(jnp.int32, (t_blk, 1), 0) % chunk
    b = _chunk_scan(lf, pos, jnp.add, 0.0, chunk)
    c = li - b
    cm = _chunk_scan(c, pos, jnp.maximum, NEG_INF, chunk)

    m_prev = m_s[...]
    xs, w_inters, e_negms, wks = [], [], [], []
    for ck in range(n_chunks):
        sl = slice(ck * chunk, (ck + 1) * chunk)
        m_t = b[sl] + jnp.maximum(m_prev, cm[sl])
        m_new = m_t[chunk - 1:chunk]
        b_last = b[(ck + 1) * chunk - 1:(ck + 1) * chunk]
        xs.append(b[sl] - m_t)
        w_inters.append(jnp.exp(b[sl] + m_prev - m_t))
        e_negms.append(jnp.exp(-m_t))
        wks.append(jnp.exp(c[sl] + (b_last - m_new)))
        m_prev = m_new
    m_s[...] = m_prev
    c_t = c.T
    wk_t = jnp.concatenate(wks, axis=0).T

    ri = lax.broadcasted_iota(jnp.int32, (chunk, chunk), 0)
    ci = lax.broadcasted_iota(jnp.int32, (chunk, chunk), 1)
    causal = ci <= ri
    ones_blk = jnp.ones((chunk, DV), BF16)
    zeros_blk = jnp.zeros((chunk, DV), BF16)
    ones2 = jnp.ones((2 * DV, DV), BF16)
    heads = range(N_HEADS)
    dot = functools.partial(jnp.dot, preferred_element_type=F32)

    def split2(x):
        hi = x.astype(BF16)
        return jnp.concatenate([hi, (x - hi.astype(F32)).astype(BF16)], axis=1)

    cxs = [cx_s[h] for h in heads]
    for ck in range(n_chunks):
        sl = slice(ck * chunk, (ck + 1) * chunk)
        qs = [q_ref[0, sl, h * DQK:(h + 1) * DQK] for h in heads]
        kts = [kt_ref[h * DQK:(h + 1) * DQK, sl] for h in heads]
        vs = [v_ref[0, sl, h * DV:(h + 1) * DV] for h in heads]
        ss = [dot(qs[h], kts[h]) for h in heads]
        qcs = [dot(qs[h], cxs[h].astype(BF16)) for h in heads]
        dcs = [dot((kts[h].astype(F32) * wk_t[h:h + 1, sl]).astype(BF16),
                   jnp.concatenate([vs[h], ones_blk], axis=1)) for h in heads]
        cxs = [w_inters[ck][chunk - 1:chunk, h:h + 1] * cxs[h] + dcs[h] for h in heads]
        avs = []
        for h in heads:
            dm = xs[ck][:, h:h + 1] + c_t[h:h + 1, sl]
            a = ss[h] * jnp.exp(jnp.where(causal, dm, NEG_INF))
            vv = jnp.concatenate([jnp.concatenate([vs[h], ones_blk], axis=1),
                                  jnp.concatenate([zeros_blk, ones_blk], axis=1)], axis=0)
            avs.append(dot(split2(a), vv))
        hhs, mss = [], []
        for h in heads:
            wi = jnp.broadcast_to(w_inters[ck][:, h:h + 1], (chunk, DV))
            en = jnp.broadcast_to(e_negms[ck][:, h:h + 1], (chunk, DV))
            den = avs[h][:, DV:] + wi * qcs[h][:, DV:]
            inv = 1.0 / jnp.maximum(jnp.abs(den), en)
            hh = (avs[h][:, :DV] + qcs[h][:, :DV] * wi) * inv
            hhs.append(hh)
            mss.append(dot(split2(hh * hh), ones2))
        for h in heads:
            hn = hhs[h] * lax.rsqrt(mss[h] * (1.0 / DV) + EPS)
            hn = hn * gain_ref[:, h * DV:(h + 1) * DV]
            og = _sigmoid(o_ref[0, sl, h * DV:(h + 1) * DV])
            hg_ref[0, sl, h * DV:(h + 1) * DV] = (og * hn).astype(hg_ref.dtype)
    for h in heads:
        cx_s[h] = cxs[h]

    @pl.when(step == pl.num_programs(1) - 1)
    def _():
        cx_out_ref[0] = cx_s[...]
        m_out_ref[0] = m_s[...]


def mlstm_prompt(q, kt, v, o, gi, gf, b_i, b_f, gain, batch, seq):
    hq, hv = N_HEADS * DQK, N_HEADS * DV
    t = PROMPT_BLOCK
    nblk = seq // t
    r3 = lambda a: a.reshape(batch, seq, a.shape[-1])
    pad8 = lambda a: jnp.pad(a.reshape(1, N_HEADS), ((0, 0), (0, LANES - N_HEADS)))
    tok = lambda w: pl.BlockSpec((1, t, w), lambda bb, s: (bb, s, 0))
    cst = lambda w: pl.BlockSpec((1, w), lambda bb, s: (0, 0))
    hg, cx, m_new = pl.pallas_call(
        functools.partial(_mlstm_prompt_kernel, chunk=PROMPT_CHUNK),
        grid=(batch, nblk),
        in_specs=[tok(hq), pl.BlockSpec((hq, t), lambda bb, s: (0, bb * nblk + s)),
                  tok(hv), tok(hv), tok(LANES), tok(LANES), cst(LANES), cst(LANES), cst(hv)],
        out_specs=[tok(hv),
                   pl.BlockSpec((1, N_HEADS, DQK, 2 * DV), lambda bb, s: (bb, 0, 0, 0)),
                   pl.BlockSpec((1, 1, LANES), lambda bb, s: (bb, 0, 0))],
        out_shape=[jax.ShapeDtypeStruct((batch, seq, hv), BF16),
                   jax.ShapeDtypeStruct((batch, N_HEADS, DQK, 2 * DV), F32),
                   jax.ShapeDtypeStruct((batch, 1, LANES), F32)],
        scratch_shapes=[pltpu.VMEM((N_HEADS, DQK, 2 * DV), F32), pltpu.VMEM((1, LANES), F32)],
        compiler_params=_cparams("parallel", "arbitrary"),
        name="mlstm_prompt",
    )(r3(q), kt, r3(v), r3(o), r3(gi), r3(gf), pad8(b_i), pad8(b_f), gain.reshape(1, hv))
    return hg.reshape(batch * seq, hv), cx[..., :DV], cx[..., DV], m_new[:, 0, :N_HEADS]


def _mlstm_sample_kernel(q_ref, k_ref, v_ref, o_ref, gi_ref, gf_ref, m0_ref, n0_ref, c0_ref,
                         bi_ref, bf_ref, gain_ref, seg64_ref, seg128_ref, e64_ref, e128_ref,
                         hg_ref, c_out_ref, n_out_ref, m_out_ref,
                         qc_s, *, seq):
    rows = q_ref.shape[0]
    n_b = rows // seq
    tpos = lax.broadcasted_iota(jnp.int32, (rows, 1), 0) % seq

    def shift(x, d):
        return x if d == 0 else pltpu.roll(x, d, 0)

    def unshift(x, d):
        return x if d == 0 else pltpu.roll(x, rows - d, 0)

    seg64, seg128 = seg64_ref[...], seg128_ref[...]
    e64, e128 = e64_ref[...], e128_ref[...]

    li = gi_ref[...] + bi_ref[...]
    lf = _log_sigmoid(gf_ref[...] + bf_ref[...])
    b = lf
    for d in range(1, seq):
        b = b + jnp.where(tpos >= d, shift(lf, d), 0.0)
    m_prev = m0_ref[...]
    inter = b + m_prev
    dvals = []
    m_t = inter
    for d in range(seq):
        dd = jnp.where(tpos >= d, b - shift(b, d) + shift(li, d), NEG_INF)
        dvals.append(dd)
        m_t = jnp.maximum(m_t, dd)
    w_inter = jnp.exp(inter - m_t)
    w_intra = [jnp.exp(dd - m_t) for dd in dvals]

    q = q_ref[...].astype(BF16).astype(F32)
    k = k_ref[...].astype(BF16).astype(F32)
    v = v_ref[...].astype(BF16).astype(F32)

    den = jnp.zeros((rows, LANES), F32)
    num = jnp.zeros((rows, N_HEADS * DV), F32)
    for d in range(seq):
        a = _dot01(q * shift(k, d), seg64) * w_intra[d]
        den = den + a
        a_exp = jnp.dot(a.astype(BF16), e128, preferred_element_type=F32)
        num = num + a_exp * shift(v, d)

    last = lambda x: functools.reduce(
        lambda acc, d: jnp.where(tpos == seq - 1 - d, unshift(x, d), acc), range(1, seq), x)
    b_last = last(b)
    m_new = last(m_t)
    wk = jnp.exp(b_last - b + li - m_new)
    kw = k * _dot01(wk, e64)
    decay_exp = _dot01(w_inter, e128)

    ksum = kw
    for d in range(1, seq):
        ksum = ksum + shift(kw, d)
    n_rows = _dot01(w_inter, e64) * n0_ref[...] + ksum
    sel = (lax.broadcasted_iota(jnp.int32, (n_b, rows), 1)
           == lax.broadcasted_iota(jnp.int32, (n_b, rows), 0) * seq + (seq - 1))
    sel = jnp.where(sel, 1.0, 0.0).astype(BF16)
    n_out_ref[...] = _dot01_left(sel, n_rows)
    m_out_ref[...] = _dot01_left(sel, m_t)

    prow = lax.broadcasted_iota(jnp.int32, (2 * seq, 2 * DV), 0)
    pcol = lax.broadcasted_iota(jnp.int32, (2 * seq, 2 * DV), 1)
    own = (prow < seq) == (pcol < DV)
    first = lax.broadcasted_iota(jnp.int32, (2 * seq, DV), 0) < seq
    for p in range(n_b // 2):
        r0 = p * 2 * seq
        for h in range(N_HEADS):
            c_a = c0_ref[2 * p, h]
            c_b = c0_ref[2 * p + 1, h]
            q_pair = q[r0:r0 + 2 * seq, h * DQK:(h + 1) * DQK].astype(BF16)
            c_cat = jnp.concatenate([c_a, c_b], axis=1).astype(BF16)
            r = jnp.dot(q_pair, c_cat, preferred_element_type=F32)
            qc_s[r0:r0 + 2 * seq, h * DV:(h + 1) * DV] = jnp.where(first, r[:, :DV], r[:, DV:])
            v_pair = v[r0:r0 + 2 * seq, h * DV:(h + 1) * DV]
            v2 = jnp.where(own, jnp.concatenate([v_pair, v_pair], axis=1), 0.0).astype(BF16)
            kw_pair = kw[r0:r0 + 2 * seq, h * DQK:(h + 1) * DQK].astype(BF16)
            d_c = lax.dot_general(kw_pair, v2, (((0,), (0,)), ((), ())),
                                  preferred_element_type=F32)
            dec_a = decay_exp[r0 + seq - 1:r0 + seq, h * DV:(h + 1) * DV]
            dec_b = decay_exp[r0 + 2 * seq - 1:r0 + 2 * seq, h * DV:(h + 1) * DV]
            c_out_ref[2 * p, h] = dec_a * c_a + d_c[:, :DV]
            c_out_ref[2 * p + 1, h] = dec_b * c_b + d_c[:, DV:]

    qn = _dot01(q * n0_ref[...].astype(BF16).astype(F32), seg64)
    den = den + w_inter * qn
    num = num + qc_s[...] * decay_exp
    inv_scale = 1.0 / jnp.maximum(jnp.abs(den), jnp.exp(-m_t))
    hh = num * _dot01(inv_scale, e128)
    ms = _dot01(hh * hh, seg128) * (1.0 / DV)
    hn = hh * _dot01(lax.rsqrt(ms + EPS), e128) * gain_ref[...]
    hg_ref[...] = (_sigmoid(o_ref[...]) * hn).astype(hg_ref.dtype)


def mlstm_sample(q, k, v, o, gi, gf, b_i, b_f, gain, c0, n0, m0, batch, seq):
    hq, hv = N_HEADS * DQK, N_HEADS * DV
    rows = SAMPLE_BATCH_BLOCK * seq
    pad8 = lambda a: jnp.pad(a.reshape(1, N_HEADS), ((0, 0), (0, LANES - N_HEADS)))
    m0_rows = jnp.pad(jnp.repeat(m0, seq, axis=0), ((0, 0), (0, LANES - N_HEADS)))
    n0_rows = jnp.repeat(n0.reshape(batch, hq), seq, axis=0)
    lane = jnp.arange(LANES)
    seg64 = (jnp.arange(hq)[:, None] // DQK == lane[None, :]).astype(BF16)
    seg128 = (jnp.arange(hv)[:, None] // DV == lane[None, :]).astype(BF16)
    tok = lambda w: pl.BlockSpec((rows, w), lambda i: (i, 0))
    cst = lambda a: pl.BlockSpec(a.shape, lambda i: (0,) * a.ndim)
    consts = [pad8(b_i), pad8(b_f), gain.reshape(1, hv), seg64, seg128, seg64.T, seg128.T]
    hg, c_new, n_new, m_new = pl.pallas_call(
        functools.partial(_mlstm_sample_kernel, seq=seq),
        grid=(batch // SAMPLE_BATCH_BLOCK,),
        in_specs=[tok(hq), tok(hq), tok(hv), tok(hv), tok(LANES), tok(LANES), tok(LANES), tok(hq),
                  pl.BlockSpec((SAMPLE_BATCH_BLOCK, N_HEADS, DQK, DV), lambda i: (i, 0, 0, 0))]
                 + [cst(a) for a in consts],
        out_specs=[tok(hv),
                   pl.BlockSpec((SAMPLE_BATCH_BLOCK, N_HEADS, DQK, DV), lambda i: (i, 0, 0, 0)),
                   pl.BlockSpec((SAMPLE_BATCH_BLOCK, hq), lambda i: (i, 0)),
                   pl.BlockSpec((SAMPLE_BATCH_BLOCK, LANES), lambda i: (i, 0))],
        out_shape=[jax.ShapeDtypeStruct((batch * seq, hv), BF16),
                   jax.ShapeDtypeStruct((batch, N_HEADS, DQK, DV), F32),
                   jax.ShapeDtypeStruct((batch, hq), F32),
                   jax.ShapeDtypeStruct((batch, LANES), F32)],
        scratch_shapes=[pltpu.VMEM((rows, hv), F32)],
        compiler_params=_cparams("parallel"),
        name="mlstm_sample",
    )(q, k, v, o, gi, gf, m0_rows, n0_rows, c0, *consts)
    return hg, c_new, n_new.reshape(batch, N_HEADS, DQK), m_new[:, :N_HEADS]


def _bias_table_kernel(rb_ref, o_ref):
    _, nq, ns = o_ref.shape
    dist = (lax.broadcasted_iota(jnp.int32, (nq, ns), 0) + WINDOW
            - lax.broadcasted_iota(jnp.int32, (nq, ns), 1))
    n = jnp.maximum(dist, 0)
    large = MAX_EXACT + (jnp.log(jnp.maximum(n, 1).astype(F32) / MAX_EXACT)
                         / math.log(MAX_DISTANCE / MAX_EXACT) * (N_BUCKETS - MAX_EXACT)).astype(jnp.int32)
    large = jnp.minimum(large, N_BUCKETS - 1)
    bucket = jnp.where(n < MAX_EXACT, n, large)
    valid = (dist >= 0) & (dist < WINDOW)
    for h in range(Q_HEADS):
        acc = jnp.zeros((nq, ns), F32)
        for bkt in range(N_BUCKETS):
            acc = jnp.where(bucket == bkt, rb_ref[bkt, h], acc)
        o_ref[h] = jnp.where(valid, acc, NEG_INF)


def bias_table(rel_bias, nq, ns):
    return pl.pallas_call(
        _bias_table_kernel,
        in_specs=[pl.BlockSpec(memory_space=pltpu.SMEM)],
        out_specs=pl.BlockSpec(memory_space=pltpu.VMEM),
        out_shape=jax.ShapeDtypeStruct((Q_HEADS, nq, ns), F32),
        name="bias_table",
    )(rel_bias)


def _softmax_sink_pv(s, sink_col, vv):
    m = jnp.maximum(jnp.max(s, axis=1, keepdims=True), sink_col)
    e = jnp.exp(s - m)
    denom = jnp.sum(e, axis=1, keepdims=True) + jnp.exp(sink_col - m)
    return jnp.dot(e.astype(BF16), vv, preferred_element_type=F32) / denom


def _sink_column(sink_ref, kh, rows_per_head):
    g = lax.broadcasted_iota(jnp.int32, (GROUP * rows_per_head, 1), 0) // rows_per_head
    col = jnp.full((GROUP * rows_per_head, 1), sink_ref[kh * GROUP + GROUP - 1], F32)
    for gg in range(GROUP - 1):
        col = jnp.where(g == gg, sink_ref[kh * GROUP + gg], col)
    return col


def _attn_prompt_kernel(sink_ref, q_ref, kp_ref, kc_ref, vp_ref, vc_ref, bias_ref, o_ref):
    blk = pl.program_id(1)
    lim = jnp.where(blk == 0, WINDOW, 0)
    dead = lax.broadcasted_iota(jnp.int32, (GROUP * WINDOW, 2 * WINDOW), 1) < lim
    kvh = range(KV_HEADS)
    kslice = lambda kh: slice(kh * HEAD_DIM, (kh + 1) * HEAD_DIM)
    ss = []
    for kh in kvh:
        kk = jnp.concatenate([kp_ref[0, :, kslice(kh)], kc_ref[0, :, kslice(kh)]], axis=0)
        qs = jnp.concatenate(
            [q_ref[0, :, (kh * GROUP + g) * HEAD_DIM:(kh * GROUP + g + 1) * HEAD_DIM]
             for g in range(GROUP)], axis=0)
        ss.append(lax.dot_general(qs, kk, (((1,), (1,)), ((), ())), preferred_element_type=F32))
    pvs, denoms = [], []
    for kh in kvh:
        s = ss[kh] * (HEAD_DIM ** -0.5) + bias_ref[kh]
        s = jnp.where(dead, NEG_INF, s)
        sink = _sink_column(sink_ref, kh, WINDOW)
        m = jnp.maximum(jnp.max(s, axis=1, keepdims=True), sink)
        e = jnp.exp(s - m)
        denoms.append(jnp.sum(e, axis=1, keepdims=True) + jnp.exp(sink - m))
        vv = jnp.concatenate([vp_ref[0, :, kslice(kh)], vc_ref[0, :, kslice(kh)]], axis=0)
        pvs.append(jnp.dot(e.astype(BF16), vv, preferred_element_type=F32))
    for kh in kvh:
        o = pvs[kh] / denoms[kh]
        for g in range(GROUP):
            hd = (kh * GROUP + g) * HEAD_DIM
            o_ref[0, :, hd:hd + HEAD_DIM] = o[g * WINDOW:(g + 1) * WINDOW].astype(o_ref.dtype)


def attn_prompt(q, k, v, table, sinks, batch, seq):
    d = Q_HEADS * HEAD_DIM
    dk = KV_HEADS * HEAD_DIM
    nb = seq // WINDOW
    r3 = lambda a: a.reshape(batch, seq, a.shape[-1])
    prev = pl.BlockSpec((1, WINDOW, dk), lambda b, i: (b, jnp.maximum(i - 1, 0), 0))
    cur = pl.BlockSpec((1, WINDOW, dk), lambda b, i: (b, i, 0))
    out = pl.pallas_call(
        _attn_prompt_kernel,
        grid=(batch, nb),
        in_specs=[pl.BlockSpec(memory_space=pltpu.SMEM),
                  pl.BlockSpec((1, WINDOW, d), lambda b, i: (b, i, 0)),
                  prev, cur, prev, cur,
                  pl.BlockSpec((KV_HEADS, GROUP * WINDOW, 2 * WINDOW), lambda b, i: (0, 0, 0))],
        out_specs=pl.BlockSpec((1, WINDOW, d), lambda b, i: (b, i, 0)),
        out_shape=jax.ShapeDtypeStruct((batch, seq, d), BF16),
        compiler_params=_cparams("parallel", "arbitrary"),
        name="attn_prompt",
    )(sinks, r3(q), r3(k), r3(k), r3(v), r3(v),
      table.reshape(KV_HEADS, GROUP * WINDOW, 2 * WINDOW))
    return out.reshape(batch * seq, d)


def _attn_sample_kernel(sink_ref, q_ref, kc_ref, kn_ref, vc_ref, vn_ref, bias_ref, o_ref, *, seq):
    n_b = q_ref.shape[0]
    pad = jnp.zeros((WINDOW - seq, HEAD_DIM), F32)
    for b in range(n_b):
        for kh in range(KV_HEADS):
            ks = slice(kh * HEAD_DIM, (kh + 1) * HEAD_DIM)
            kk = jnp.concatenate([kc_ref[b, :, ks], kn_ref[b, :, ks], pad], axis=0).astype(BF16)
            vv = jnp.concatenate([vc_ref[b, :, ks], vn_ref[b, :, ks], pad], axis=0).astype(BF16)
            qs = jnp.concatenate(
                [q_ref[b, :, (kh * GROUP + g) * HEAD_DIM:(kh * GROUP + g + 1) * HEAD_DIM]
                 for g in range(GROUP)], axis=0).astype(BF16)
            s = lax.dot_general(qs, kk, (((1,), (1,)), ((), ())), preferred_element_type=F32)
            s = s * (HEAD_DIM ** -0.5) + bias_ref[kh]
            o = _softmax_sink_pv(s, _sink_column(sink_ref, kh, seq), vv)
            for g in range(GROUP):
                hd = (kh * GROUP + g) * HEAD_DIM
                o_ref[b, :, hd:hd + HEAD_DIM] = o[g * seq:(g + 1) * seq].astype(o_ref.dtype)


def attn_sample(q, k_new, v_new, cache_k, cache_v, table, sinks, batch, seq):
    d = Q_HEADS * HEAD_DIM
    dk = KV_HEADS * HEAD_DIM
    nbk = ATTN_SAMPLE_BLOCK
    new = pl.BlockSpec((nbk, seq, dk), lambda i: (i, 0, 0))
    old = pl.BlockSpec((nbk, WINDOW, dk), lambda i: (i, 0, 0))
    out = pl.pallas_call(
        functools.partial(_attn_sample_kernel, seq=seq),
        grid=(batch // nbk,),
        in_specs=[pl.BlockSpec(memory_space=pltpu.SMEM),
                  pl.BlockSpec((nbk, seq, d), lambda i: (i, 0, 0)),
                  old, new, old, new,
                  pl.BlockSpec((KV_HEADS, GROUP * seq, 2 * WINDOW), lambda i: (0, 0, 0))],
        out_specs=pl.BlockSpec((nbk, seq, d), lambda i: (i, 0, 0)),
        out_shape=jax.ShapeDtypeStruct((batch, seq, d), F32),
        compiler_params=_cparams("parallel"),
        name="attn_sample",
    )(sinks, q.reshape(batch, seq, d), cache_k.reshape(batch, WINDOW, dk),
      k_new.reshape(batch, seq, dk), cache_v.reshape(batch, WINDOW, dk),
      v_new.reshape(batch, seq, dk), table.reshape(KV_HEADS, GROUP * seq, 2 * WINDOW))
    return out.reshape(batch * seq, d)


def _trunk(x, state, cache, w, tm, tm_mlp):
    batch, seq, d = x.shape
    x = x.reshape(batch * seq, d)
    hq, hv = N_HEADS * DQK, N_HEADS * DV
    depth = w["norm_mix"].shape[0]
    n_a = w["w_in"].shape[0]
    act = BF16 if state is None else F32
    cs, ns, ms = [], [], []
    for l in range(depth):
        if l < n_a:
            gates = (w["b_igate"][l], w["b_fgate"][l], w["mlstm_norm"][l])
            if state is None:
                q, kt, v, o, gi, gf = mlstm_inproj(x, w["norm_mix"], w["w_in_prompt"], w["w_kt"], l, tm)
                a, c_new, n_new, m_new = mlstm_prompt(q, kt, v, o, gi, gf, *gates, batch, seq)
            else:
                cols = ((0, hq), (hq, hq), (2 * hq, hv), (2 * hq + hv, hv),
                        (2 * hq + 2 * hv, LANES), (2 * hq + 2 * hv + LANES, LANES))
                q, k, v, o, gi, gf = norm_matmul(x, w["norm_mix"], l, w["w_in"], l, cols,
                                                 (F32,) * 6, tm)
                a, c_new, n_new, m_new = mlstm_sample(q, k, v, o, gi, gf, *gates, state[0][l],
                                                      state[1][l], state[2][l], batch, seq)
            cs.append(c_new); ns.append(n_new); ms.append(m_new)
            w_o, lo = w["w_mlstm_out"], l
        else:
            j = l - n_a
            if j == 0:
                dk = KV_HEADS * HEAD_DIM
                k_new, v_new, k16, v16 = norm_matmul(
                    x, w["kv_norm"], 0, w["w_kv"], 0, ((0, dk), (dk, dk), (0, dk), (dk, dk)),
                    (F32, F32, BF16, BF16), tm)
                table = bias_table(w["rel_bias"], WINDOW if cache is None else seq, 2 * WINDOW)
            (q,) = norm_matmul(x, w["norm_mix"], l, w["w_q"], j, ((0, Q_HEADS * HEAD_DIM),),
                               (act,), tm)
            if cache is None:
                a = attn_prompt(q, k16, v16, table, w["attn_sinks"][j], batch, seq)
            else:
                a = attn_sample(q, k_new, v_new, cache[0], cache[1], table, w["attn_sinks"][j],
                                batch, seq)
            w_o, lo = w["w_attn_out"], j
        x = mlp(a, w_o, lo, x, w["norm_ffn"], w["w_up"], w["w_down"], l, w["final_norm"],
                l == depth - 1, tm_mlp, 512)
    dk = KV_HEADS * HEAD_DIM
    k3 = k_new.reshape(batch, seq, dk)
    v3 = v_new.reshape(batch, seq, dk)
    if cache is None:
        win_k, win_v = k3[:, -WINDOW:], v3[:, -WINDOW:]
    else:
        win_k = jnp.concatenate([cache[0].reshape(batch, WINDOW, dk)[:, seq:], k3], axis=1)
        win_v = jnp.concatenate([cache[1].reshape(batch, WINDOW, dk)[:, seq:], v3], axis=1)
    shp = (batch, WINDOW, KV_HEADS, HEAD_DIM)
    return (x.reshape(batch, seq, d), jnp.stack(cs), jnp.stack(ns), jnp.stack(ms),
            win_k.reshape(shp), win_v.reshape(shp))


def kernel(x_prompt, x_sample, state_C, state_n, state_m, cache_k, cache_v, norm_mix, norm_ffn,
           w_mlstm_in, b_igate, b_fgate, mlstm_norm, w_mlstm_out, kv_norm, w_kv, w_q, attn_sinks,
           w_attn_out, rel_bias, w_up, w_down, final_norm):
    hq, hv = N_HEADS * DQK, N_HEADS * DV
    lane_pad = ((0, 0), (0, 0), (0, LANES - N_HEADS))
    w_mq = w_mlstm_in[:, :, :hq].astype(BF16)
    w_mk = (w_mlstm_in[:, :, hq:2 * hq] * DQK ** -0.5).astype(BF16)
    w_vo = w_mlstm_in[:, :, 2 * hq:2 * hq + 2 * hv].astype(BF16)
    w_gi = jnp.pad(w_mlstm_in[:, :, 2 * hq + 2 * hv:2 * hq + 2 * hv + N_HEADS], lane_pad).astype(BF16)
    w_gf = jnp.pad(w_mlstm_in[:, :, 2 * hq + 2 * hv + N_HEADS:], lane_pad).astype(BF16)
    w_in = jnp.concatenate([w_mq, w_mk, w_vo, w_gi, w_gf], axis=-1)
    w_in_prompt = jnp.concatenate([w_mq, w_vo, w_gi, w_gf], axis=-1)
    w_kt = jnp.swapaxes(w_mk, 1, 2)
    w = dict(norm_mix=norm_mix[:, None, :], norm_ffn=norm_ffn[:, None, :], w_in=w_in,
             w_in_prompt=w_in_prompt, w_kt=w_kt, b_igate=b_igate, b_fgate=b_fgate,
             mlstm_norm=mlstm_norm, w_mlstm_out=w_mlstm_out.astype(BF16),
             kv_norm=kv_norm[None, None, :],
             w_kv=w_kv.astype(BF16)[None], w_q=w_q.astype(BF16), attn_sinks=attn_sinks,
             w_attn_out=w_attn_out.astype(BF16), rel_bias=rel_bias, w_up=w_up.astype(BF16),
             w_down=w_down.astype(BF16), final_norm=final_norm)
    y_p, c_p, n_p, m_p, k_p, v_p = _trunk(x_prompt, None, None, w, 512, 1024)
    y_s, c_s, n_s, m_s, k_s, v_s = _trunk(x_sample, (state_C, state_n, state_m),
                                          (cache_k, cache_v), w, 512, 512)
    return (y_p, y_s, c_p, n_p, m_p, k_p, v_p, c_s, n_s, m_s, k_s, v_s)
```

```python
import functools
import math

import jax
import jax.numpy as jnp
from jax import lax
from jax.experimental import pallas as pl
from jax.experimental.pallas import tpu as pltpu

F32 = jnp.float32
BF16 = jnp.bfloat16
EPS = 1e-6
NEG_INF = float("-inf")

N_HEADS = 8
DQK = 64
DV = 128
Q_HEADS = 16
KV_HEADS = 4
GROUP = Q_HEADS // KV_HEADS
HEAD_DIM = 64
WINDOW = 128
N_BUCKETS = 32
MAX_EXACT = N_BUCKETS // 2
MAX_DISTANCE = 128

LANES = 128
VMEM_LIMIT = 48 * 1024 * 1024

PROMPT_CHUNK = 128
PROMPT_BLOCK = 256
SAMPLE_BATCH_BLOCK = 8
ATTN_SAMPLE_BLOCK = 8
MLP_ROW_CHUNK = 512
MLP_FF_BLOCK = 1024


def _cparams(*sem):
    return pltpu.CompilerParams(dimension_semantics=sem, vmem_limit_bytes=VMEM_LIMIT)


def _rms(x, g):
    return x * lax.rsqrt(jnp.mean(x * x, axis=-1, keepdims=True) + EPS) * g


def _split3(x):
    hi = x.astype(BF16)
    r = x - hi.astype(F32)
    mid = r.astype(BF16)
    lo = (r - mid.astype(F32)).astype(BF16)
    return hi, mid, lo


def _dot01(x, onehot):
    hi, mid, lo = _split3(x)
    d = lambda a: jnp.dot(a, onehot, preferred_element_type=F32)
    return d(hi) + d(mid) + d(lo)


def _dot01_left(onehot, x):
    hi, mid, lo = _split3(x)
    d = lambda a: jnp.dot(onehot, a, preferred_element_type=F32)
    return d(hi) + d(mid) + d(lo)


def _log_sigmoid(x):
    return jnp.minimum(x, 0.0) - jnp.log1p(jnp.exp(-jnp.abs(x)))


def _sigmoid(x):
    return 1.0 / (1.0 + jnp.exp(-x))


def _norm_mm_kernel(x_ref, g_ref, w_ref, *out_refs, splits):
    y = _rms(x_ref[...], g_ref[...]).astype(BF16)
    r = jnp.dot(y, w_ref[...], preferred_element_type=F32)
    for o_ref, (off, n) in zip(out_refs, splits):
        o_ref[...] = r[:, off:off + n].astype(o_ref.dtype)


def _layer_spec(arr, layer):
    idx = (layer,) + (0,) * (arr.ndim - 1)
    return pl.BlockSpec((None,) + arr.shape[1:], lambda *_: idx)


def norm_matmul(x, g, lg, w, lw, splits, dtypes, tm):
    m, d = x.shape
    assert m % tm == 0 and all(off + n <= w.shape[-1] for off, n in splits)
    return pl.pallas_call(
        functools.partial(_norm_mm_kernel, splits=splits),
        grid=(m // tm,),
        in_specs=[pl.BlockSpec((tm, d), lambda i: (i, 0)),
                  _layer_spec(g, lg), _layer_spec(w, lw)],
        out_specs=[pl.BlockSpec((tm, n), lambda i: (i, 0)) for _, n in splits],
        out_shape=[jax.ShapeDtypeStruct((m, n), dt) for (_, n), dt in zip(splits, dtypes)],
        compiler_params=_cparams("parallel"),
        name="norm_matmul",
    )(x, g, w)


def _mlp_kernel(a_ref, wo_ref, x_ref, g_ref, wu_ref, wd_ref, gf_ref, o_ref, xn_ref, *,
                final_norm, row_chunk):
    j = pl.program_id(1)
    tm = x_ref.shape[0]

    @pl.when(j == 0)
    def _():
        for r in range(0, tm, row_chunk):
            rows = slice(r, r + row_chunk)
            x = x_ref[rows, :] + jnp.dot(a_ref[rows, :].astype(BF16), wo_ref[...],
                                         preferred_element_type=F32)
            xn_ref[rows, :] = _rms(x, g_ref[...]).astype(BF16)
            o_ref[rows, :] = x

    h = jnp.dot(xn_ref[...], wu_ref[...], preferred_element_type=F32)
    h = jnp.maximum(h, 0.0)
    o_ref[...] += jnp.dot((h * h).astype(BF16), wd_ref[...], preferred_element_type=F32)

    if final_norm:
        @pl.when(j == pl.num_programs(1) - 1)
        def _():
            o_ref[...] = _rms(o_ref[...], gf_ref[...])


def mlp(a, w_o, lo, x, g, w_up, w_down, l, g_final, final_norm, tm, tf):
    m, d = x.shape
    ff = w_up.shape[-1]
    return pl.pallas_call(
        functools.partial(_mlp_kernel, final_norm=final_norm, row_chunk=min(tm, MLP_ROW_CHUNK)),
        grid=(m // tm, ff // tf),
        in_specs=[pl.BlockSpec((tm, d), lambda i, j: (i, 0)),
                  _layer_spec(w_o, lo),
                  pl.BlockSpec((tm, d), lambda i, j: (i, 0)),
                  _layer_spec(g, l),
                  pl.BlockSpec((None, d, tf), lambda i, j: (l, 0, j)),
                  pl.BlockSpec((None, tf, d), lambda i, j: (l, j, 0)),
                  pl.BlockSpec((1, d), lambda i, j: (0, 0))],
        out_specs=pl.BlockSpec((tm, d), lambda i, j: (i, 0)),
        out_shape=jax.ShapeDtypeStruct((m, d), F32),
        scratch_shapes=[pltpu.VMEM((tm, d), BF16)],
        compiler_params=_cparams("parallel", "arbitrary"),
        name="mlp",
    )(a, w_o, x, g, w_up, w_down, g_final.reshape(1, d))


def _mlstm_inproj_kernel(x_ref, g_ref, w_ref, q_ref, k_ref, v_ref, o_ref, gate_ref, *, k_transposed):
    hq, hv = N_HEADS * DQK, N_HEADS * DV
    tm = x_ref.shape[0]
    y = _rms(x_ref[...], g_ref[...]).astype(BF16)
    r = jnp.dot(y, w_ref[...], preferred_element_type=F32)
    q_ref[...] = r[:, :hq].astype(q_ref.dtype)
    k = r[:, hq:2 * hq] * DQK ** -0.5
    k_ref[...] = (k.T if k_transposed else k).astype(k_ref.dtype)
    v_ref[...] = r[:, 2 * hq:2 * hq + hv].astype(v_ref.dtype)
    o_ref[...] = r[:, 2 * hq + hv:2 * hq + 2 * hv]
    gate_ref[...] = jnp.concatenate(
        [r[:, 2 * hq + 2 * hv:], jnp.zeros((tm, LANES - 2 * N_HEADS), F32)], axis=1)


def mlstm_inproj(x, g, w, l, tm, k_transposed, act):
    m, d = x.shape
    hq, hv = N_HEADS * DQK, N_HEADS * DV
    assert w.shape[-1] == 2 * hq + 2 * hv + 2 * N_HEADS
    row = lambda width: pl.BlockSpec((tm, width), lambda i: (i, 0))
    k_spec = pl.BlockSpec((hq, tm), lambda i: (0, i)) if k_transposed else row(hq)
    k_shape = (hq, m) if k_transposed else (m, hq)
    return pl.pallas_call(
        functools.partial(_mlstm_inproj_kernel, k_transposed=k_transposed),
        grid=(m // tm,),
        in_specs=[row(d), _layer_spec(g, l), _layer_spec(w, l)],
        out_specs=[row(hq), k_spec, row(hv), row(hv), row(LANES)],
        out_shape=[jax.ShapeDtypeStruct((m, hq), act), jax.ShapeDtypeStruct(k_shape, act),
                   jax.ShapeDtypeStruct((m, hv), act), jax.ShapeDtypeStruct((m, hv), F32),
                   jax.ShapeDtypeStruct((m, LANES), F32)],
        compiler_params=_cparams("parallel"),
        name="mlstm_inproj",
    )(x, g, w)


def _chunk_scan(x, pos, op, fill, length):
    k = 1
    while k < length:
        x = op(x, jnp.where(pos >= k, pltpu.roll(x, k, 0), fill))
        k *= 2
    return x


def _mlstm_prompt_kernel(q_ref, kt_ref, v_ref, o_ref, gate_ref, bi_ref, bf_ref, gain_ref,
                         hg_ref, cx_out_ref, m_out_ref, cx_s, m_s, *, chunk):
    t_blk = q_ref.shape[1]
    n_chunks = t_blk // chunk
    step = pl.program_id(1)

    @pl.when(step == 0)
    def _():
        cx_s[...] = jnp.zeros_like(cx_s)
        m_s[...] = jnp.zeros_like(m_s)

    gates = gate_ref[0]
    li = gates + bi_ref[...]
    lf = _log_sigmoid(pltpu.roll(gates, LANES - N_HEADS, 1) + bf_ref[...])
    pos = lax.broadcasted_iota(jnp.int32, (t_blk, 1), 0) % chunk
    b = _chunk_scan(lf, pos, jnp.add, 0.0, chunk)
    c = li - b
    cm = _chunk_scan(c, pos, jnp.maximum, NEG_INF, chunk)

    m_prev = m_s[...]
    xs, w_inters, e_negms, wks = [], [], [], []
    for ck in range(n_chunks):
        sl = slice(ck * chunk, (ck + 1) * chunk)
        m_t = b[sl] + jnp.maximum(m_prev, cm[sl])
        m_new = m_t[chunk - 1:chunk]
        b_last = b[(ck + 1) * chunk - 1:(ck + 1) * chunk]
        xs.append(b[sl] - m_t)
        w_inters.append(jnp.exp(b[sl] + m_prev - m_t))
        e_negms.append(jnp.exp(-m_t))
        wks.append(jnp.exp(c[sl] + (b_last - m_new)))
        m_prev = m_new
    m_s[...] = m_prev
    c_t = c.T
    wk_t = jnp.concatenate(wks, axis=0).T

    ri = lax.broadcasted_iota(jnp.int32, (chunk, chunk), 0)
    ci = lax.broadcasted_iota(jnp.int32, (chunk, chunk), 1)
    causal = ci <= ri
    ones_blk = jnp.ones((chunk, DV), BF16)
    zeros_blk = jnp.zeros((chunk, DV), BF16)
    ones2 = jnp.ones((2 * DV, DV), BF16)
    heads = range(N_HEADS)
    dot = functools.partial(jnp.dot, preferred_element_type=F32)

    def split2(x):
        hi = x.astype(BF16)
        return jnp.concatenate([hi, (x - hi.astype(F32)).astype(BF16)], axis=1)

    cxs = [cx_s[h] for h in heads]
    for ck in range(n_chunks):
        sl = slice(ck * chunk, (ck + 1) * chunk)
        qs = [q_ref[0, sl, h * DQK:(h + 1) * DQK] for h in heads]
        kts = [kt_ref[h * DQK:(h + 1) * DQK, sl] for h in heads]
        vs = [v_ref[0, sl, h * DV:(h + 1) * DV] for h in heads]
        ss = [dot(qs[h], kts[h]) for h in heads]
        qcs = [dot(qs[h], cxs[h].astype(BF16)) for h in heads]
        dcs = [dot((kts[h].astype(F32) * wk_t[h:h + 1, sl]).astype(BF16),
                   jnp.concatenate([vs[h], ones_blk], axis=1)) for h in heads]
        cxs = [w_inters[ck][chunk - 1:chunk, h:h + 1] * cxs[h] + dcs[h] for h in heads]
        avs = []
        for h in heads:
            dm = xs[ck][:, h:h + 1] + c_t[h:h + 1, sl]
            a = ss[h] * jnp.exp(jnp.where(causal, dm, NEG_INF))
            vv = jnp.concatenate([jnp.concatenate([vs[h], ones_blk], axis=1),
                                  jnp.concatenate([zeros_blk, ones_blk], axis=1)], axis=0)
            avs.append(dot(split2(a), vv))
        hhs, mss = [], []
        for h in heads:
            wi = jnp.broadcast_to(w_inters[ck][:, h:h + 1], (chunk, DV))
            en = jnp.broadcast_to(e_negms[ck][:, h:h + 1], (chunk, DV))
            den = avs[h][:, DV:] + wi * qcs[h][:, DV:]
            inv = 1.0 / jnp.maximum(jnp.abs(den), en)
            hh = (avs[h][:, :DV] + qcs[h][:, :DV] * wi) * inv
            hhs.append(hh)
            mss.append(dot(split2(hh * hh), ones2))
        for h in heads:
            hn = hhs[h] * lax.rsqrt(mss[h] * (1.0 / DV) + EPS)
            hn = hn * gain_ref[:, h * DV:(h + 1) * DV]
            og = _sigmoid(o_ref[0, sl, h * DV:(h + 1) * DV])
            hg_ref[0, sl, h * DV:(h + 1) * DV] = (og * hn).astype(hg_ref.dtype)
    for h in heads:
        cx_s[h] = cxs[h]

    @pl.when(step == pl.num_programs(1) - 1)
    def _():
        cx_out_ref[0] = cx_s[...]
        m_out_ref[0] = m_s[...]


def mlstm_prompt(q, kt, v, o, gate, b_i, b_f, gain, batch, seq):
    hq, hv = N_HEADS * DQK, N_HEADS * DV
    t = PROMPT_BLOCK
    nblk = seq // t
    r3 = lambda a: a.reshape(batch, seq, a.shape[-1])
    pad8 = lambda a: jnp.pad(a.reshape(1, N_HEADS), ((0, 0), (0, LANES - N_HEADS)))
    tok = lambda w: pl.BlockSpec((1, t, w), lambda bb, s: (bb, s, 0))
    cst = lambda w: pl.BlockSpec((1, w), lambda bb, s: (0, 0))
    hg, cx, m_new = pl.pallas_call(
        functools.partial(_mlstm_prompt_kernel, chunk=PROMPT_CHUNK),
        grid=(batch, nblk),
        in_specs=[tok(hq), pl.BlockSpec((hq, t), lambda bb, s: (0, bb * nblk + s)),
                  tok(hv), tok(hv), tok(LANES), cst(LANES), cst(LANES), cst(hv)],
        out_specs=[tok(hv),
                   pl.BlockSpec((1, N_HEADS, DQK, 2 * DV), lambda bb, s: (bb, 0, 0, 0)),
                   pl.BlockSpec((1, 1, LANES), lambda bb, s: (bb, 0, 0))],
        out_shape=[jax.ShapeDtypeStruct((batch, seq, hv), BF16),
                   jax.ShapeDtypeStruct((batch, N_HEADS, DQK, 2 * DV), F32),
                   jax.ShapeDtypeStruct((batch, 1, LANES), F32)],
        scratch_shapes=[pltpu.VMEM((N_HEADS, DQK, 2 * DV), F32), pltpu.VMEM((1, LANES), F32)],
        compiler_params=_cparams("parallel", "arbitrary"),
        name="mlstm_prompt",
    )(r3(q), kt, r3(v), r3(o), r3(gate), pad8(b_i), pad8(b_f), gain.reshape(1, hv))
    return hg.reshape(batch * seq, hv), cx[..., :DV], cx[..., DV], m_new[:, 0, :N_HEADS]


def _mlstm_sample_kernel(q_ref, k_ref, v_ref, o_ref, gate_ref, m0_ref, n0_ref, c0_ref, c_carry_ref,
                         bi_ref, bf_ref, gain_ref, seg64_ref, seg128_ref, e64_ref, e128_ref,
                         hg_ref, c_out_ref, n_out_ref, m_out_ref,
                         qc_s, *, seq):
    del c_carry_ref
    rows = q_ref.shape[0]
    n_b = rows // seq
    tpos = lax.broadcasted_iota(jnp.int32, (rows, 1), 0) % seq

    def shift(x, d):
        return x if d == 0 else pltpu.roll(x, d, 0)

    def unshift(x, d):
        return x if d == 0 else pltpu.roll(x, rows - d, 0)

    seg64, seg128 = seg64_ref[...], seg128_ref[...]
    e64, e128 = e64_ref[...], e128_ref[...]

    gates = gate_ref[...]
    li = gates + bi_ref[...]
    lf = _log_sigmoid(pltpu.roll(gates, LANES - N_HEADS, 1) + bf_ref[...])
    b = lf
    for d in range(1, seq):
        b = b + jnp.where(tpos >= d, shift(lf, d), 0.0)
    m_prev = m0_ref[...]
    inter = b + m_prev
    dvals = []
    m_t = inter
    for d in range(seq):
        dd = jnp.where(tpos >= d, b - shift(b, d) + shift(li, d), NEG_INF)
        dvals.append(dd)
        m_t = jnp.maximum(m_t, dd)
    w_inter = jnp.exp(inter - m_t)
    w_intra = [jnp.exp(dd - m_t) for dd in dvals]

    q = q_ref[...].astype(BF16).astype(F32)
    k = k_ref[...].astype(BF16).astype(F32)
    v = v_ref[...].astype(BF16).astype(F32)

    den = jnp.zeros((rows, LANES), F32)
    num = jnp.zeros((rows, N_HEADS * DV), F32)
    for d in range(seq):
        a = _dot01(q * shift(k, d), seg64) * w_intra[d]
        den = den + a
        a_exp = jnp.dot(a.astype(BF16), e128, preferred_element_type=F32)
        num = num + a_exp * shift(v, d)

    last = lambda x: functools.reduce(
        lambda acc, d: jnp.where(tpos == seq - 1 - d, unshift(x, d), acc), range(1, seq), x)
    b_last = last(b)
    m_new = last(m_t)
    wk = jnp.exp(b_last - b + li - m_new)
    kw = k * _dot01(wk, e64)
    decay_exp = _dot01(w_inter, e128)

    ksum = kw
    for d in range(1, seq):
        ksum = ksum + shift(kw, d)
    n_rows = _dot01(w_inter, e64) * n0_ref[...] + ksum
    sel = (lax.broadcasted_iota(jnp.int32, (n_b, rows), 1)
           == lax.broadcasted_iota(jnp.int32, (n_b, rows), 0) * seq + (seq - 1))
    sel = jnp.where(sel, 1.0, 0.0).astype(BF16)
    n_out_ref[...] = _dot01_left(sel, n_rows)
    m_out_ref[...] = _dot01_left(sel, m_t)

    prow = lax.broadcasted_iota(jnp.int32, (2 * seq, 2 * DV), 0)
    pcol = lax.broadcasted_iota(jnp.int32, (2 * seq, 2 * DV), 1)
    own = (prow < seq) == (pcol < DV)
    first = lax.broadcasted_iota(jnp.int32, (2 * seq, DV), 0) < seq
    for p in range(n_b // 2):
        r0 = p * 2 * seq
        for h in range(N_HEADS):
            c_a = c0_ref[2 * p, h]
            c_b = c0_ref[2 * p + 1, h]
            q_pair = q[r0:r0 + 2 * seq, h * DQK:(h + 1) * DQK].astype(BF16)
            c_cat = jnp.concatenate([c_a, c_b], axis=1).astype(BF16)
            r = jnp.dot(q_pair, c_cat, preferred_element_type=F32)
            qc_s[r0:r0 + 2 * seq, h * DV:(h + 1) * DV] = jnp.where(first, r[:, :DV], r[:, DV:])
            v_pair = v[r0:r0 + 2 * seq, h * DV:(h + 1) * DV]
            v2 = jnp.where(own, jnp.concatenate([v_pair, v_pair], axis=1), 0.0).astype(BF16)
            kw_pair = kw[r0:r0 + 2 * seq, h * DQK:(h + 1) * DQK].astype(BF16)
            d_c = lax.dot_general(kw_pair, v2, (((0,), (0,)), ((), ())),
                                  preferred_element_type=F32)
            dec_a = decay_exp[r0 + seq - 1:r0 + seq, h * DV:(h + 1) * DV]
            dec_b = decay_exp[r0 + 2 * seq - 1:r0 + 2 * seq, h * DV:(h + 1) * DV]
            c_out_ref[2 * p, h] = dec_a * c_a + d_c[:, :DV]
            c_out_ref[2 * p + 1, h] = dec_b * c_b + d_c[:, DV:]

    qn = _dot01(q * n0_ref[...].astype(BF16).astype(F32), seg64)
    den = den + w_inter * qn
    num = num + qc_s[...] * decay_exp
    inv_scale = 1.0 / jnp.maximum(jnp.abs(den), jnp.exp(-m_t))
    hh = num * _dot01(inv_scale, e128)
    ms = _dot01(hh * hh, seg128) * (1.0 / DV)
    hn = hh * _dot01(lax.rsqrt(ms + EPS), e128) * gain_ref[...]
    hg_ref[...] = (_sigmoid(o_ref[...]) * hn).astype(hg_ref.dtype)


def mlstm_sample(q, k, v, o, gate, b_i, b_f, gain, state_c, c_carry, l, n0, m0, batch, seq):
    hq, hv = N_HEADS * DQK, N_HEADS * DV
    c_blk = pl.BlockSpec((None, SAMPLE_BATCH_BLOCK, N_HEADS, DQK, DV), lambda i: (l, i, 0, 0, 0))
    rows = SAMPLE_BATCH_BLOCK * seq
    pad8 = lambda a: jnp.pad(a.reshape(1, N_HEADS), ((0, 0), (0, LANES - N_HEADS)))
    m0_rows = jnp.pad(jnp.repeat(m0, seq, axis=0), ((0, 0), (0, LANES - N_HEADS)))
    n0_rows = jnp.repeat(n0.reshape(batch, hq), seq, axis=0)
    lane = jnp.arange(LANES)
    seg64 = (jnp.arange(hq)[:, None] // DQK == lane[None, :]).astype(BF16)
    seg128 = (jnp.arange(hv)[:, None] // DV == lane[None, :]).astype(BF16)
    tok = lambda w: pl.BlockSpec((rows, w), lambda i: (i, 0))
    cst = lambda a: pl.BlockSpec(a.shape, lambda i: (0,) * a.ndim)
    consts = [pad8(b_i), pad8(b_f), gain.reshape(1, hv), seg64, seg128, seg64.T, seg128.T]
    hg, c_new, n_new, m_new = pl.pallas_call(
        functools.partial(_mlstm_sample_kernel, seq=seq),
        grid=(batch // SAMPLE_BATCH_BLOCK,),
        in_specs=[tok(hq), tok(hq), tok(hv), tok(hv), tok(LANES), tok(LANES), tok(hq),
                  c_blk, pl.BlockSpec(memory_space=pl.ANY)]
                 + [cst(a) for a in consts],
        out_specs=[tok(hv), c_blk,
                   pl.BlockSpec((SAMPLE_BATCH_BLOCK, hq), lambda i: (i, 0)),
                   pl.BlockSpec((SAMPLE_BATCH_BLOCK, LANES), lambda i: (i, 0))],
        out_shape=[jax.ShapeDtypeStruct((batch * seq, hv), BF16),
                   jax.ShapeDtypeStruct(state_c.shape, F32),
                   jax.ShapeDtypeStruct((batch, hq), F32),
                   jax.ShapeDtypeStruct((batch, LANES), F32)],
        scratch_shapes=[pltpu.VMEM((rows, hv), F32)],
        input_output_aliases={8: 1} if l > 0 else {},
        compiler_params=_cparams("parallel"),
        name="mlstm_sample",
    )(q, k, v, o, gate, m0_rows, n0_rows, state_c, c_carry, *consts)
    return hg, c_new, n_new.reshape(batch, N_HEADS, DQK), m_new[:, :N_HEADS]


def _bias_table_kernel(rb_ref, o_ref):
    _, nq, ns = o_ref.shape
    dist = (lax.broadcasted_iota(jnp.int32, (nq, ns), 0) + WINDOW
            - lax.broadcasted_iota(jnp.int32, (nq, ns), 1))
    n = jnp.maximum(dist, 0)
    large = MAX_EXACT + (jnp.log(jnp.maximum(n, 1).astype(F32) / MAX_EXACT)
                         / math.log(MAX_DISTANCE / MAX_EXACT) * (N_BUCKETS - MAX_EXACT)).astype(jnp.int32)
    large = jnp.minimum(large, N_BUCKETS - 1)
    bucket = jnp.where(n < MAX_EXACT, n, large)
    valid = (dist >= 0) & (dist < WINDOW)
    for h in range(Q_HEADS):
        acc = jnp.zeros((nq, ns), F32)
        for bkt in range(N_BUCKETS):
            acc = jnp.where(bucket == bkt, rb_ref[bkt, h], acc)
        o_ref[h] = jnp.where(valid, acc, NEG_INF)


def bias_table(rel_bias, nq, ns):
    return pl.pallas_call(
        _bias_table_kernel,
        in_specs=[pl.BlockSpec(memory_space=pltpu.SMEM)],
        out_specs=pl.BlockSpec(memory_space=pltpu.VMEM),
        out_shape=jax.ShapeDtypeStruct((Q_HEADS, nq, ns), F32),
        name="bias_table",
    )(rel_bias)


def _softmax_sink_pv(s, sink_col, vv):
    m = jnp.maximum(jnp.max(s, axis=1, keepdims=True), sink_col)
    e = jnp.exp(s - m)
    denom = jnp.sum(e, axis=1, keepdims=True) + jnp.exp(sink_col - m)
    return jnp.dot(e.astype(BF16), vv, preferred_element_type=F32) / denom


def _sink_column(sink_ref, kh, rows_per_head):
    g = lax.broadcasted_iota(jnp.int32, (GROUP * rows_per_head, 1), 0) // rows_per_head
    col = jnp.full((GROUP * rows_per_head, 1), sink_ref[kh * GROUP + GROUP - 1], F32)
    for gg in range(GROUP - 1):
        col = jnp.where(g == gg, sink_ref[kh * GROUP + gg], col)
    return col


def _attn_prompt_kernel(sink_ref, q_ref, kp_ref, kc_ref, vp_ref, vc_ref, bias_ref, o_ref):
    blk = pl.program_id(1)
    lim = jnp.where(blk == 0, WINDOW, 0)
    dead = lax.broadcasted_iota(jnp.int32, (GROUP * WINDOW, 2 * WINDOW), 1) < lim
    kvh = range(KV_HEADS)
    kslice = lambda kh: slice(kh * HEAD_DIM, (kh + 1) * HEAD_DIM)
    ss = []
    for kh in kvh:
        kk = jnp.concatenate([kp_ref[0, :, kslice(kh)], kc_ref[0, :, kslice(kh)]], axis=0)
        qs = jnp.concatenate(
            [q_ref[0, :, (kh * GROUP + g) * HEAD_DIM:(kh * GROUP + g + 1) * HEAD_DIM]
             for g in range(GROUP)], axis=0)
        ss.append(lax.dot_general(qs, kk, (((1,), (1,)), ((), ())), preferred_element_type=F32))
    pvs, denoms = [], []
    for kh in kvh:
        s = ss[kh] * (HEAD_DIM ** -0.5) + bias_ref[kh]
        s = jnp.where(dead, NEG_INF, s)
        sink = _sink_column(sink_ref, kh, WINDOW)
        m = jnp.maximum(jnp.max(s, axis=1, keepdims=True), sink)
        e = jnp.exp(s - m)
        denoms.append(jnp.sum(e, axis=1, keepdims=True) + jnp.exp(sink - m))
        vv = jnp.concatenate([vp_ref[0, :, kslice(kh)], vc_ref[0, :, kslice(kh)]], axis=0)
        pvs.append(jnp.dot(e.astype(BF16), vv, preferred_element_type=F32))
    for kh in kvh:
        o = pvs[kh] / denoms[kh]
        for g in range(GROUP):
            hd = (kh * GROUP + g) * HEAD_DIM
            o_ref[0, :, hd:hd + HEAD_DIM] = o[g * WINDOW:(g + 1) * WINDOW].astype(o_ref.dtype)


def attn_prompt(q, k, v, table, sinks, batch, seq):
    d = Q_HEADS * HEAD_DIM
    dk = KV_HEADS * HEAD_DIM
    nb = seq // WINDOW
    r3 = lambda a: a.reshape(batch, seq, a.shape[-1])
    prev = pl.BlockSpec((1, WINDOW, dk), lambda b, i: (b, jnp.maximum(i - 1, 0), 0))
    cur = pl.BlockSpec((1, WINDOW, dk), lambda b, i: (b, i, 0))
    out = pl.pallas_call(
        _attn_prompt_kernel,
        grid=(batch, nb),
        in_specs=[pl.BlockSpec(memory_space=pltpu.SMEM),
                  pl.BlockSpec((1, WINDOW, d), lambda b, i: (b, i, 0)),
                  prev, cur, prev, cur,
                  pl.BlockSpec((KV_HEADS, GROUP * WINDOW, 2 * WINDOW), lambda b, i: (0, 0, 0))],
        out_specs=pl.BlockSpec((1, WINDOW, d), lambda b, i: (b, i, 0)),
        out_shape=jax.ShapeDtypeStruct((batch, seq, d), BF16),
        compiler_params=_cparams("parallel", "arbitrary"),
        name="attn_prompt",
    )(sinks, r3(q), r3(k), r3(k), r3(v), r3(v),
      table.reshape(KV_HEADS, GROUP * WINDOW, 2 * WINDOW))
    return out.reshape(batch * seq, d)


def _attn_sample_kernel(sink_ref, q_ref, kc_ref, kn_ref, vc_ref, vn_ref, bias_ref, o_ref,
                        *cache_out_refs, seq):
    n_b = q_ref.shape[0]
    dk = KV_HEADS * HEAD_DIM
    pad = jnp.zeros((WINDOW - seq, dk), F32)
    pairs = [(b, kh) for b in range(n_b) for kh in range(KV_HEADS)]
    kslice = lambda kh: slice(kh * HEAD_DIM, (kh + 1) * HEAD_DIM)
    kks = [jnp.concatenate([kc_ref[b], kn_ref[b], pad], axis=0).astype(BF16) for b in range(n_b)]
    vvs = [jnp.concatenate([vc_ref[b], vn_ref[b], pad], axis=0).astype(BF16) for b in range(n_b)]
    ss = []
    for b, kh in pairs:
        qs = jnp.concatenate(
            [q_ref[b, :, (kh * GROUP + g) * HEAD_DIM:(kh * GROUP + g + 1) * HEAD_DIM]
             for g in range(GROUP)], axis=0).astype(BF16)
        ss.append(lax.dot_general(qs, kks[b][:, kslice(kh)], (((1,), (1,)), ((), ())),
                                  preferred_element_type=F32))
    sinks = [_sink_column(sink_ref, kh, seq) for kh in range(KV_HEADS)]
    pvs, denoms = [], []
    for i, (b, kh) in enumerate(pairs):
        s = ss[i] * (HEAD_DIM ** -0.5) + bias_ref[kh]
        m = jnp.maximum(jnp.max(s, axis=1, keepdims=True), sinks[kh])
        e = jnp.exp(s - m)
        denoms.append(jnp.sum(e, axis=1, keepdims=True) + jnp.exp(sinks[kh] - m))
        pvs.append(jnp.dot(e.astype(BF16), vvs[b][:, kslice(kh)], preferred_element_type=F32))
    for i, (b, kh) in enumerate(pairs):
        o = pvs[i] / denoms[i]
        for g in range(GROUP):
            hd = (kh * GROUP + g) * HEAD_DIM
            o_ref[b, :, hd:hd + HEAD_DIM] = o[g * seq:(g + 1) * seq].astype(o_ref.dtype)
    for out_ref, old_ref, new_ref in zip(cache_out_refs, (kc_ref, vc_ref), (kn_ref, vn_ref)):
        for b in range(n_b):
            out_ref[b, :WINDOW - seq, :] = old_ref[b, seq:, :]
            out_ref[b, WINDOW - seq:, :] = new_ref[b]


def attn_sample(q, k_new, v_new, cache_k, cache_v, table, sinks, batch, seq, write_cache):
    d = Q_HEADS * HEAD_DIM
    dk = KV_HEADS * HEAD_DIM
    nbk = ATTN_SAMPLE_BLOCK
    new = pl.BlockSpec((nbk, seq, dk), lambda i: (i, 0, 0))
    old = pl.BlockSpec((nbk, WINDOW, dk), lambda i: (i, 0, 0))
    n_cache = 2 if write_cache else 0
    out = pl.pallas_call(
        functools.partial(_attn_sample_kernel, seq=seq),
        grid=(batch // nbk,),
        in_specs=[pl.BlockSpec(memory_space=pltpu.SMEM),
                  pl.BlockSpec((nbk, seq, d), lambda i: (i, 0, 0)),
                  old, new, old, new,
                  pl.BlockSpec((KV_HEADS, GROUP * seq, 2 * WINDOW), lambda i: (0, 0, 0))],
        out_specs=[pl.BlockSpec((nbk, seq, d), lambda i: (i, 0, 0))] + [old] * n_cache,
        out_shape=[jax.ShapeDtypeStruct((batch, seq, d), F32)]
                  + [jax.ShapeDtypeStruct((batch, WINDOW, dk), F32)] * n_cache,
        compiler_params=_cparams("parallel"),
        name="attn_sample",
    )(sinks, q.reshape(batch, seq, d), cache_k.reshape(batch, WINDOW, dk),
      k_new.reshape(batch, seq, dk), cache_v.reshape(batch, WINDOW, dk),
      v_new.reshape(batch, seq, dk), table.reshape(KV_HEADS, GROUP * seq, 2 * WINDOW))
    return (out[0].reshape(batch * seq, d),) + tuple(out[1:])


def _trunk(x, state, cache, w, tm, tm_mlp):
    batch, seq, d = x.shape
    x = x.reshape(batch * seq, d)
    hq, hv = N_HEADS * DQK, N_HEADS * DV
    depth = w["norm_mix"].shape[0]
    n_a = w["w_in"].shape[0]
    prompt = state is None
    act = BF16 if prompt else F32
    cs, ns, ms = [], [], []
    c_stack = None if prompt else state[0]
    for l in range(depth):
        if l < n_a:
            gates = (w["b_igate"][l], w["b_fgate"][l], w["mlstm_norm"][l])
            q, k, v, o, gate = mlstm_inproj(x, w["norm_mix"], w["w_in"], l, tm, prompt, act)
            if prompt:
                a, c_new, n_new, m_new = mlstm_prompt(q, k, v, o, gate, *gates, batch, seq)
                cs.append(c_new)
            else:
                a, c_stack, n_new, m_new = mlstm_sample(q, k, v, o, gate, *gates, state[0], c_stack,
                                                        l, state[1][l], state[2][l], batch, seq)
            ns.append(n_new); ms.append(m_new)
            w_o, lo = w["w_mlstm_out"], l
        else:
            j = l - n_a
            if j == 0:
                dk = KV_HEADS * HEAD_DIM
                k_new, v_new, k16, v16 = norm_matmul(
                    x, w["kv_norm"], 0, w["w_kv"], 0, ((0, dk), (dk, dk), (0, dk), (dk, dk)),
                    (F32, F32, BF16, BF16), tm)
                table = bias_table(w["rel_bias"], WINDOW if cache is None else seq, 2 * WINDOW)
            (q,) = norm_matmul(x, w["norm_mix"], l, w["w_q"], j, ((0, Q_HEADS * HEAD_DIM),),
                               (act,), tm)
            if prompt:
                a = attn_prompt(q, k16, v16, table, w["attn_sinks"][j], batch, seq)
            elif j == 0:
                a, win_k, win_v = attn_sample(q, k_new, v_new, cache[0], cache[1], table,
                                              w["attn_sinks"][j], batch, seq, True)
            else:
                (a,) = attn_sample(q, k_new, v_new, cache[0], cache[1], table,
                                   w["attn_sinks"][j], batch, seq, False)
            w_o, lo = w["w_attn_out"], j
        x = mlp(a, w_o, lo, x, w["norm_ffn"], w["w_up"], w["w_down"], l, w["final_norm"],
                l == depth - 1, tm_mlp, MLP_FF_BLOCK)
    if prompt:
        dk = KV_HEADS * HEAD_DIM
        win_k = k_new.reshape(batch, seq, dk)[:, -WINDOW:]
        win_v = v_new.reshape(batch, seq, dk)[:, -WINDOW:]
        c_stack = jnp.stack(cs)
    shp = (batch, WINDOW, KV_HEADS, HEAD_DIM)
    return (x.reshape(batch, seq, d), c_stack, jnp.stack(ns), jnp.stack(ms),
            win_k.reshape(shp), win_v.reshape(shp))


def kernel(x_prompt, x_sample, state_C, state_n, state_m, cache_k, cache_v, norm_mix, norm_ffn,
           w_mlstm_in, b_igate, b_fgate, mlstm_norm, w_mlstm_out, kv_norm, w_kv, w_q, attn_sinks,
           w_attn_out, rel_bias, w_up, w_down, final_norm):
    w = dict(norm_mix=norm_mix[:, None, :], norm_ffn=norm_ffn[:, None, :],
             w_in=w_mlstm_in.astype(BF16), b_igate=b_igate, b_fgate=b_fgate,
             mlstm_norm=mlstm_norm, w_mlstm_out=w_mlstm_out.astype(BF16),
             kv_norm=kv_norm[None, None, :],
             w_kv=w_kv.astype(BF16)[None], w_q=w_q.astype(BF16), attn_sinks=attn_sinks,
             w_attn_out=w_attn_out.astype(BF16), rel_bias=rel_bias, w_up=w_up.astype(BF16),
             w_down=w_down.astype(BF16), final_norm=final_norm)
    y_p, c_p, n_p, m_p, k_p, v_p = _trunk(x_prompt, None, None, w, 512, 1024)
    y_s, c_s, n_s, m_s, k_s, v_s = _trunk(x_sample, (state_C, state_n, state_m),
                                          (cache_k, cache_v), w, 512, 512)
    return (y_p, y_s, c_p, n_p, m_p, k_p, v_p, c_s, n_s, m_s, k_s, v_s)
```

```python
import functools
import math

import jax
import jax.numpy as jnp
from jax import lax
from jax.experimental import pallas as pl
from jax.experimental.pallas import tpu as pltpu

F32 = jnp.float32
BF16 = jnp.bfloat16
EPS = 1e-6
NEG_INF = float("-inf")

N_HEADS = 8
DQK = 64
DV = 128
Q_HEADS = 16
KV_HEADS = 4
GROUP = Q_HEADS // KV_HEADS
HEAD_DIM = 64
WINDOW = 128
N_BUCKETS = 32
MAX_EXACT = N_BUCKETS // 2
MAX_DISTANCE = 128

LANES = 128
VMEM_LIMIT = 48 * 1024 * 1024

PROMPT_CHUNK = 128
PROMPT_BLOCK = 256
SAMPLE_BATCH_BLOCK = 8
ATTN_SAMPLE_BLOCK = 8
ATTN_PROMPT_BLOCKS = 2
MLP_ROW_CHUNK = 512
MLP_FF_BLOCK = 1024


def _cparams(*sem):
    return pltpu.CompilerParams(dimension_semantics=sem, vmem_limit_bytes=VMEM_LIMIT)


def _rms(x, g):
    return x * lax.rsqrt(jnp.mean(x * x, axis=-1, keepdims=True) + EPS) * g


def _split3(x):
    hi = x.astype(BF16)
    r = x - hi.astype(F32)
    mid = r.astype(BF16)
    lo = (r - mid.astype(F32)).astype(BF16)
    return hi, mid, lo


def _dot01(x, onehot):
    hi, mid, lo = _split3(x)
    d = lambda a: jnp.dot(a, onehot, preferred_element_type=F32)
    return d(hi) + d(mid) + d(lo)


def _dot01_left(onehot, x):
    hi, mid, lo = _split3(x)
    d = lambda a: jnp.dot(onehot, a, preferred_element_type=F32)
    return d(hi) + d(mid) + d(lo)


def _log_sigmoid(x):
    return jnp.minimum(x, 0.0) - jnp.log1p(jnp.exp(-jnp.abs(x)))


def _sigmoid(x):
    return 1.0 / (1.0 + jnp.exp(-x))


def _norm_mm_kernel(x_ref, g_ref, w_ref, *out_refs, splits):
    y = _rms(x_ref[...], g_ref[...]).astype(BF16)
    r = jnp.dot(y, w_ref[...], preferred_element_type=F32)
    for o_ref, (off, n) in zip(out_refs, splits):
        o_ref[...] = r[:, off:off + n].astype(o_ref.dtype)


def _layer_spec(arr, layer):
    idx = (layer,) + (0,) * (arr.ndim - 1)
    return pl.BlockSpec((None,) + arr.shape[1:], lambda *_: idx)


def norm_matmul(x, g, lg, w, lw, splits, dtypes, tm):
    m, d = x.shape
    assert m % tm == 0 and all(off + n <= w.shape[-1] for off, n in splits)
    return pl.pallas_call(
        functools.partial(_norm_mm_kernel, splits=splits),
        grid=(m // tm,),
        in_specs=[pl.BlockSpec((tm, d), lambda i: (i, 0)),
                  _layer_spec(g, lg), _layer_spec(w, lw)],
        out_specs=[pl.BlockSpec((tm, n), lambda i: (i, 0)) for _, n in splits],
        out_shape=[jax.ShapeDtypeStruct((m, n), dt) for (_, n), dt in zip(splits, dtypes)],
        compiler_params=_cparams("parallel"),
        name="norm_matmul",
    )(x, g, w)


def _mlp_kernel(a_ref, wo_ref, x_ref, g_ref, wu_ref, wd_ref, gf_ref, o_ref, xn_ref, *,
                final_norm, row_chunk):
    j = pl.program_id(1)
    tm = x_ref.shape[0]

    @pl.when(j == 0)
    def _():
        for r in range(0, tm, row_chunk):
            rows = slice(r, r + row_chunk)
            x = x_ref[rows, :] + jnp.dot(a_ref[rows, :].astype(BF16), wo_ref[...],
                                         preferred_element_type=F32)
            xn_ref[rows, :] = _rms(x, g_ref[...]).astype(BF16)
            o_ref[rows, :] = x

    h = jnp.dot(xn_ref[...], wu_ref[...], preferred_element_type=F32)
    h = jnp.maximum(h, 0.0)
    o_ref[...] += jnp.dot((h * h).astype(BF16), wd_ref[...], preferred_element_type=F32)

    if final_norm:
        @pl.when(j == pl.num_programs(1) - 1)
        def _():
            o_ref[...] = _rms(o_ref[...], gf_ref[...])


def mlp(a, w_o, lo, x, g, w_up, w_down, l, g_final, final_norm, tm, tf):
    m, d = x.shape
    ff = w_up.shape[-1]
    return pl.pallas_call(
        functools.partial(_mlp_kernel, final_norm=final_norm, row_chunk=min(tm, MLP_ROW_CHUNK)),
        grid=(m // tm, ff // tf),
        in_specs=[pl.BlockSpec((tm, d), lambda i, j: (i, 0)),
                  _layer_spec(w_o, lo),
                  pl.BlockSpec((tm, d), lambda i, j: (i, 0)),
                  _layer_spec(g, l),
                  pl.BlockSpec((None, d, tf), lambda i, j: (l, 0, j)),
                  pl.BlockSpec((None, tf, d), lambda i, j: (l, j, 0)),
                  pl.BlockSpec((1, d), lambda i, j: (0, 0))],
        out_specs=pl.BlockSpec((tm, d), lambda i, j: (i, 0)),
        out_shape=jax.ShapeDtypeStruct((m, d), F32),
        scratch_shapes=[pltpu.VMEM((tm, d), BF16)],
        compiler_params=_cparams("parallel", "arbitrary"),
        name="mlp",
    )(a, w_o, x, g, w_up, w_down, g_final.reshape(1, d))


def _mlstm_inproj_kernel(x_ref, g_ref, w_ref, q_ref, k_ref, v_ref, o_ref, gate_ref, *, k_transposed):
    hq, hv = N_HEADS * DQK, N_HEADS * DV
    tm = x_ref.shape[0]
    y = _rms(x_ref[...], g_ref[...]).astype(BF16)
    r = jnp.dot(y, w_ref[...], preferred_element_type=F32)
    q_ref[...] = r[:, :hq].astype(q_ref.dtype)
    k = r[:, hq:2 * hq] * DQK ** -0.5
    k_ref[...] = (k.T if k_transposed else k).astype(k_ref.dtype)
    v_ref[...] = r[:, 2 * hq:2 * hq + hv].astype(v_ref.dtype)
    o_ref[...] = r[:, 2 * hq + hv:2 * hq + 2 * hv]
    gate_ref[...] = jnp.concatenate(
        [r[:, 2 * hq + 2 * hv:], jnp.zeros((tm, LANES - 2 * N_HEADS), F32)], axis=1)


def mlstm_inproj(x, g, w, l, tm, k_transposed, act):
    m, d = x.shape
    hq, hv = N_HEADS * DQK, N_HEADS * DV
    assert w.shape[-1] == 2 * hq + 2 * hv + 2 * N_HEADS
    row = lambda width: pl.BlockSpec((tm, width), lambda i: (i, 0))
    k_spec = pl.BlockSpec((hq, tm), lambda i: (0, i)) if k_transposed else row(hq)
    k_shape = (hq, m) if k_transposed else (m, hq)
    return pl.pallas_call(
        functools.partial(_mlstm_inproj_kernel, k_transposed=k_transposed),
        grid=(m // tm,),
        in_specs=[row(d), _layer_spec(g, l), _layer_spec(w, l)],
        out_specs=[row(hq), k_spec, row(hv), row(hv), row(LANES)],
        out_shape=[jax.ShapeDtypeStruct((m, hq), act), jax.ShapeDtypeStruct(k_shape, act),
                   jax.ShapeDtypeStruct((m, hv), act), jax.ShapeDtypeStruct((m, hv), F32),
                   jax.ShapeDtypeStruct((m, LANES), F32)],
        compiler_params=_cparams("parallel"),
        name="mlstm_inproj",
    )(x, g, w)


def _chunk_scan(x, pos, op, fill, length):
    k = 1
    while k < length:
        x = op(x, jnp.where(pos >= k, pltpu.roll(x, k, 0), fill))
        k *= 2
    return x


def _mlstm_prompt_kernel(q_ref, kt_ref, v_ref, o_ref, gate_ref, bi_ref, bf_ref, gain_ref,
                         hg_ref, cx_out_ref, m_out_ref, cx_s, m_s, *, chunk):
    t_blk = q_ref.shape[1]
    n_chunks = t_blk // chunk
    step = pl.program_id(1)

    @pl.when(step == 0)
    def _():
        cx_s[...] = jnp.zeros_like(cx_s)
        m_s[...] = jnp.zeros_like(m_s)

    gates = gate_ref[0]
    li = gates + bi_ref[...]
    lf = _log_sigmoid(pltpu.roll(gates, LANES - N_HEADS, 1) + bf_ref[...])
    pos = lax.broadcasted_iota(jnp.int32, (t_blk, 1), 0) % chunk
    b = _chunk_scan(lf, pos, jnp.add, 0.0, chunk)
    c = li - b
    cm = _chunk_scan(c, pos, jnp.maximum, NEG_INF, chunk)

    m_prev = m_s[...]
    xs, w_inters, e_negms, wks = [], [], [], []
    for ck in range(n_chunks):
        sl = slice(ck * chunk, (ck + 1) * chunk)
        m_t = b[sl] + jnp.maximum(m_prev, cm[sl])
        m_new = m_t[chunk - 1:chunk]
        b_last = b[(ck + 1) * chunk - 1:(ck + 1) * chunk]
        xs.append(b[sl] - m_t)
        w_inters.append(jnp.exp(b[sl] + m_prev - m_t))
        e_negms.append(jnp.exp(-m_t))
        wks.append(jnp.exp(c[sl] + (b_last - m_new)))
        m_prev = m_new
    m_s[...] = m_prev
    c_t = c.T
    wk_t = jnp.concatenate(wks, axis=0).T

    ri = lax.broadcasted_iota(jnp.int32, (chunk, chunk), 0)
    ci = lax.broadcasted_iota(jnp.int32, (chunk, chunk), 1)
    causal = ci <= ri
    ones_blk = jnp.ones((chunk, DV), BF16)
    ones_sq = jnp.ones((DV, DV), BF16)
    heads = range(N_HEADS)
    dot = functools.partial(jnp.dot, preferred_element_type=F32)

    cxs = [cx_s[h] for h in heads]
    for ck in range(n_chunks):
        sl = slice(ck * chunk, (ck + 1) * chunk)
        qs = [q_ref[0, sl, h * DQK:(h + 1) * DQK] for h in heads]
        kts = [kt_ref[h * DQK:(h + 1) * DQK, sl] for h in heads]
        vs = [v_ref[0, sl, h * DV:(h + 1) * DV] for h in heads]
        ss = [dot(qs[h], kts[h]) for h in heads]
        qcs = [dot(qs[h], cxs[h].astype(BF16)) for h in heads]
        dcs = [dot((kts[h].astype(F32) * wk_t[h:h + 1, sl]).astype(BF16),
                   jnp.concatenate([vs[h], ones_blk], axis=1)) for h in heads]
        cxs = [w_inters[ck][chunk - 1:chunk, h:h + 1] * cxs[h] + dcs[h] for h in heads]
        avs = []
        for h in heads:
            dm = xs[ck][:, h:h + 1] + c_t[h:h + 1, sl]
            a = ss[h] * jnp.exp(jnp.where(causal, dm, NEG_INF))
            avs.append(dot(a.astype(BF16), jnp.concatenate([vs[h], ones_blk], axis=1)))
        hhs, mss = [], []
        for h in heads:
            wi = jnp.broadcast_to(w_inters[ck][:, h:h + 1], (chunk, DV))
            en = jnp.broadcast_to(e_negms[ck][:, h:h + 1], (chunk, DV))
            den = avs[h][:, DV:] + wi * qcs[h][:, DV:]
            inv = 1.0 / jnp.maximum(jnp.abs(den), en)
            hh = (avs[h][:, :DV] + qcs[h][:, :DV] * wi) * inv
            hhs.append(hh)
            mss.append(dot((hh * hh).astype(BF16), ones_sq))
        for h in heads:
            hn = hhs[h] * lax.rsqrt(mss[h] * (1.0 / DV) + EPS)
            hn = hn * gain_ref[:, h * DV:(h + 1) * DV]
            og = _sigmoid(o_ref[0, sl, h * DV:(h + 1) * DV])
            hg_ref[0, sl, h * DV:(h + 1) * DV] = (og * hn).astype(hg_ref.dtype)
    for h in heads:
        cx_s[h] = cxs[h]

    @pl.when(step == pl.num_programs(1) - 1)
    def _():
        cx_out_ref[0] = cx_s[...]
        m_out_ref[0] = m_s[...]


def mlstm_prompt(q, kt, v, o, gate, b_i, b_f, gain, batch, seq):
    hq, hv = N_HEADS * DQK, N_HEADS * DV
    t = PROMPT_BLOCK
    nblk = seq // t
    r3 = lambda a: a.reshape(batch, seq, a.shape[-1])
    pad8 = lambda a: jnp.pad(a.reshape(1, N_HEADS), ((0, 0), (0, LANES - N_HEADS)))
    tok = lambda w: pl.BlockSpec((1, t, w), lambda bb, s: (bb, s, 0))
    cst = lambda w: pl.BlockSpec((1, w), lambda bb, s: (0, 0))
    hg, cx, m_new = pl.pallas_call(
        functools.partial(_mlstm_prompt_kernel, chunk=PROMPT_CHUNK),
        grid=(batch, nblk),
        in_specs=[tok(hq), pl.BlockSpec((hq, t), lambda bb, s: (0, bb * nblk + s)),
                  tok(hv), tok(hv), tok(LANES), cst(LANES), cst(LANES), cst(hv)],
        out_specs=[tok(hv),
                   pl.BlockSpec((1, N_HEADS, DQK, 2 * DV), lambda bb, s: (bb, 0, 0, 0)),
                   pl.BlockSpec((1, 1, LANES), lambda bb, s: (bb, 0, 0))],
        out_shape=[jax.ShapeDtypeStruct((batch, seq, hv), BF16),
                   jax.ShapeDtypeStruct((batch, N_HEADS, DQK, 2 * DV), F32),
                   jax.ShapeDtypeStruct((batch, 1, LANES), F32)],
        scratch_shapes=[pltpu.VMEM((N_HEADS, DQK, 2 * DV), F32), pltpu.VMEM((1, LANES), F32)],
        compiler_params=_cparams("parallel", "arbitrary"),
        name="mlstm_prompt",
    )(r3(q), kt, r3(v), r3(o), r3(gate), pad8(b_i), pad8(b_f), gain.reshape(1, hv))
    return hg.reshape(batch * seq, hv), cx[..., :DV], cx[..., DV], m_new[:, 0, :N_HEADS]


def _mlstm_sample_kernel(q_ref, k_ref, v_ref, o_ref, gate_ref, m0_ref, n0_ref, c0_ref, c_carry_ref,
                         bi_ref, bf_ref, gain_ref, seg64_ref, seg128_ref, e64_ref, e128_ref,
                         hg_ref, c_out_ref, n_out_ref, m_out_ref,
                         qc_s, *, seq):
    del c_carry_ref
    rows = q_ref.shape[0]
    n_b = rows // seq
    tpos = lax.broadcasted_iota(jnp.int32, (rows, 1), 0) % seq

    def shift(x, d):
        return x if d == 0 else pltpu.roll(x, d, 0)

    def unshift(x, d):
        return x if d == 0 else pltpu.roll(x, rows - d, 0)

    seg64, seg128 = seg64_ref[...], seg128_ref[...]
    e64, e128 = e64_ref[...], e128_ref[...]

    gates = gate_ref[...]
    li = gates + bi_ref[...]
    lf = _log_sigmoid(pltpu.roll(gates, LANES - N_HEADS, 1) + bf_ref[...])
    b = lf
    for d in range(1, seq):
        b = b + jnp.where(tpos >= d, shift(lf, d), 0.0)
    m_prev = m0_ref[...]
    inter = b + m_prev
    dvals = []
    m_t = inter
    for d in range(seq):
        dd = jnp.where(tpos >= d, b - shift(b, d) + shift(li, d), NEG_INF)
        dvals.append(dd)
        m_t = jnp.maximum(m_t, dd)
    w_inter = jnp.exp(inter - m_t)
    w_intra = [jnp.exp(dd - m_t) for dd in dvals]

    q = q_ref[...].astype(BF16).astype(F32)
    k = k_ref[...].astype(BF16).astype(F32)
    v = v_ref[...].astype(BF16).astype(F32)

    den = jnp.zeros((rows, LANES), F32)
    num = jnp.zeros((rows, N_HEADS * DV), F32)
    for d in range(seq):
        a = _dot01(q * shift(k, d), seg64) * w_intra[d]
        den = den + a
        a_exp = jnp.dot(a.astype(BF16), e128, preferred_element_type=F32)
        num = num + a_exp * shift(v, d)

    last = lambda x: functools.reduce(
        lambda acc, d: jnp.where(tpos == seq - 1 - d, unshift(x, d), acc), range(1, seq), x)
    b_last = last(b)
    m_new = last(m_t)
    wk = jnp.exp(b_last - b + li - m_new)
    kw = k * _dot01(wk, e64)
    decay_exp = _dot01(w_inter, e128)

    ksum = kw
    for d in range(1, seq):
        ksum = ksum + shift(kw, d)
    n_rows = _dot01(w_inter, e64) * n0_ref[...] + ksum
    sel = (lax.broadcasted_iota(jnp.int32, (n_b, rows), 1)
           == lax.broadcasted_iota(jnp.int32, (n_b, rows), 0) * seq + (seq - 1))
    sel = jnp.where(sel, 1.0, 0.0).astype(BF16)
    n_out_ref[...] = _dot01_left(sel, n_rows)
    m_out_ref[...] = _dot01_left(sel, m_t)

    prow = lax.broadcasted_iota(jnp.int32, (2 * seq, 2 * DV), 0)
    pcol = lax.broadcasted_iota(jnp.int32, (2 * seq, 2 * DV), 1)
    own = (prow < seq) == (pcol < DV)
    first = lax.broadcasted_iota(jnp.int32, (2 * seq, DV), 0) < seq
    for p in range(n_b // 2):
        r0 = p * 2 * seq
        for h in range(N_HEADS):
            c_a = c0_ref[2 * p, h]
            c_b = c0_ref[2 * p + 1, h]
            q_pair = q[r0:r0 + 2 * seq, h * DQK:(h + 1) * DQK].astype(BF16)
            c_cat = jnp.concatenate([c_a, c_b], axis=1).astype(BF16)
            r = jnp.dot(q_pair, c_cat, preferred_element_type=F32)
            qc_s[r0:r0 + 2 * seq, h * DV:(h + 1) * DV] = jnp.where(first, r[:, :DV], r[:, DV:])
            v_pair = v[r0:r0 + 2 * seq, h * DV:(h + 1) * DV]
            v2 = jnp.where(own, jnp.concatenate([v_pair, v_pair], axis=1), 0.0).astype(BF16)
            kw_pair = kw[r0:r0 + 2 * seq, h * DQK:(h + 1) * DQK].astype(BF16)
            d_c = lax.dot_general(kw_pair, v2, (((0,), (0,)), ((), ())),
                                  preferred_element_type=F32)
            dec_a = decay_exp[r0 + seq - 1:r0 + seq, h * DV:(h + 1) * DV]
            dec_b = decay_exp[r0 + 2 * seq - 1:r0 + 2 * seq, h * DV:(h + 1) * DV]
            c_out_ref[2 * p, h] = dec_a * c_a + d_c[:, :DV]
            c_out_ref[2 * p + 1, h] = dec_b * c_b + d_c[:, DV:]

    qn = _dot01(q * n0_ref[...].astype(BF16).astype(F32), seg64)
    den = den + w_inter * qn
    num = num + qc_s[...] * decay_exp
    inv_scale = 1.0 / jnp.maximum(jnp.abs(den), jnp.exp(-m_t))
    hh = num * _dot01(inv_scale, e128)
    ms = _dot01(hh * hh, seg128) * (1.0 / DV)
    hn = hh * _dot01(lax.rsqrt(ms + EPS), e128) * gain_ref[...]
    hg_ref[...] = (_sigmoid(o_ref[...]) * hn).astype(hg_ref.dtype)


def mlstm_sample(q, k, v, o, gate, b_i, b_f, gain, state_c, c_carry, l, n0, m0, batch, seq):
    hq, hv = N_HEADS * DQK, N_HEADS * DV
    c_blk = pl.BlockSpec((None, SAMPLE_BATCH_BLOCK, N_HEADS, DQK, DV), lambda i: (l, i, 0, 0, 0))
    rows = SAMPLE_BATCH_BLOCK * seq
    pad8 = lambda a: jnp.pad(a.reshape(1, N_HEADS), ((0, 0), (0, LANES - N_HEADS)))
    m0_rows = jnp.pad(jnp.repeat(m0, seq, axis=0), ((0, 0), (0, LANES - N_HEADS)))
    n0_rows = jnp.repeat(n0.reshape(batch, hq), seq, axis=0)
    lane = jnp.arange(LANES)
    seg64 = (jnp.arange(hq)[:, None] // DQK == lane[None, :]).astype(BF16)
    seg128 = (jnp.arange(hv)[:, None] // DV == lane[None, :]).astype(BF16)
    tok = lambda w: pl.BlockSpec((rows, w), lambda i: (i, 0))
    cst = lambda a: pl.BlockSpec(a.shape, lambda i: (0,) * a.ndim)
    consts = [pad8(b_i), pad8(b_f), gain.reshape(1, hv), seg64, seg128, seg64.T, seg128.T]
    hg, c_new, n_new, m_new = pl.pallas_call(
        functools.partial(_mlstm_sample_kernel, seq=seq),
        grid=(batch // SAMPLE_BATCH_BLOCK,),
        in_specs=[tok(hq), tok(hq), tok(hv), tok(hv), tok(LANES), tok(LANES), tok(hq),
                  c_blk, pl.BlockSpec(memory_space=pl.ANY)]
                 + [cst(a) for a in consts],
        out_specs=[tok(hv), c_blk,
                   pl.BlockSpec((SAMPLE_BATCH_BLOCK, hq), lambda i: (i, 0)),
                   pl.BlockSpec((SAMPLE_BATCH_BLOCK, LANES), lambda i: (i, 0))],
        out_shape=[jax.ShapeDtypeStruct((batch * seq, hv), BF16),
                   jax.ShapeDtypeStruct(state_c.shape, F32),
                   jax.ShapeDtypeStruct((batch, hq), F32),
                   jax.ShapeDtypeStruct((batch, LANES), F32)],
        scratch_shapes=[pltpu.VMEM((rows, hv), F32)],
        input_output_aliases={8: 1} if l > 0 else {},
        compiler_params=_cparams("parallel"),
        name="mlstm_sample",
    )(q, k, v, o, gate, m0_rows, n0_rows, state_c, c_carry, *consts)
    return hg, c_new, n_new.reshape(batch, N_HEADS, DQK), m_new[:, :N_HEADS]


def _bias_table_kernel(rb_ref, sink_ref, o_ref, *, sink_col):
    n_layers, _, nq, ns = o_ref.shape
    col = lax.broadcasted_iota(jnp.int32, (nq, ns), 1)
    dist = lax.broadcasted_iota(jnp.int32, (nq, ns), 0) + WINDOW - col
    n = jnp.maximum(dist, 0)
    large = MAX_EXACT + (jnp.log(jnp.maximum(n, 1).astype(F32) / MAX_EXACT)
                         / math.log(MAX_DISTANCE / MAX_EXACT) * (N_BUCKETS - MAX_EXACT)).astype(jnp.int32)
    large = jnp.minimum(large, N_BUCKETS - 1)
    bucket = jnp.where(n < MAX_EXACT, n, large)
    valid = (dist >= 0) & (dist < WINDOW)
    for h in range(Q_HEADS):
        acc = jnp.zeros((nq, ns), F32)
        for bkt in range(N_BUCKETS):
            acc = jnp.where(bucket == bkt, rb_ref[bkt, h], acc)
        acc = jnp.where(valid, acc, NEG_INF)
        for layer in range(n_layers):
            o_ref[layer, h] = jnp.where(col == sink_col, sink_ref[layer, h], acc)


def bias_table(rel_bias, sinks, nq, ns, sink_col):
    return pl.pallas_call(
        functools.partial(_bias_table_kernel, sink_col=sink_col),
        in_specs=[pl.BlockSpec(memory_space=pltpu.SMEM), pl.BlockSpec(memory_space=pltpu.SMEM)],
        out_specs=pl.BlockSpec(memory_space=pltpu.VMEM),
        out_shape=jax.ShapeDtypeStruct((sinks.shape[0], Q_HEADS, nq, ns), F32),
        name="bias_table",
    )(rel_bias, sinks)


PROMPT_SINK_KEY = 0


def _attn_prompt_kernel(q_ref, kp_ref, kc_ref, vp_ref, vc_ref, bias_ref, o_ref):
    dk = KV_HEADS * HEAD_DIM
    kvh = range(KV_HEADS)
    sink_row = lax.broadcasted_iota(jnp.int32, (2 * WINDOW, dk), 0) == PROMPT_SINK_KEY
    lane_head = lax.broadcasted_iota(jnp.int32, (2 * WINDOW, dk), 1) // HEAD_DIM
    out_head = lax.broadcasted_iota(jnp.int32, (GROUP * WINDOW, dk), 1) // HEAD_DIM
    zero = jnp.zeros((), BF16)
    first_of_sequence = pl.program_id(1) == 0
    for j in range(q_ref.shape[1] // WINDOW):
        rows = slice(j * WINDOW, (j + 1) * WINDOW)
        before = slice((j - 1) * WINDOW, j * WINDOW)
        k_prev, v_prev = (kp_ref[0], vp_ref[0]) if j == 0 else (kc_ref[0, before], vc_ref[0, before])
        variant = jnp.where(first_of_sequence, 0, 1) if j == 0 else 1
        k_all = jnp.where(sink_row, zero, jnp.concatenate([k_prev, kc_ref[0, rows]], axis=0))
        v_all = jnp.where(sink_row, zero, jnp.concatenate([v_prev, vc_ref[0, rows]], axis=0))
        k_bd = jnp.concatenate([jnp.where(lane_head == kh, k_all, zero) for kh in kvh], axis=0)
        v_bd = jnp.concatenate([jnp.where(lane_head == kh, v_all, zero) for kh in kvh], axis=0)
        q_cat = jnp.concatenate([q_ref[0, rows, g * dk:(g + 1) * dk] for g in range(GROUP)],
                                axis=0)
        s = lax.dot_general(q_cat, k_bd, (((1,), (1,)), ((), ())), preferred_element_type=F32)
        es, invs = [], []
        for kh in kvh:
            sk = s[:, kh * 2 * WINDOW:(kh + 1) * 2 * WINDOW] + bias_ref[variant, kh]
            e = jnp.exp(sk - jnp.max(sk, axis=1, keepdims=True))
            invs.append(1.0 / jnp.sum(e, axis=1, keepdims=True))
            es.append(e.astype(BF16))
        pv = jnp.dot(jnp.concatenate(es, axis=1), v_bd, preferred_element_type=F32)
        inv = invs[KV_HEADS - 1]
        for kh in range(KV_HEADS - 1):
            inv = jnp.where(out_head == kh, invs[kh], inv)
        o = pv * inv
        for g in range(GROUP):
            o_ref[0, rows, g * dk:(g + 1) * dk] = o[g * WINDOW:(g + 1) * WINDOW].astype(o_ref.dtype)


def attn_prompt(q, k, v, table, batch, seq):
    d = Q_HEADS * HEAD_DIM
    dk = KV_HEADS * HEAD_DIM
    nq = ATTN_PROMPT_BLOCKS
    r3 = lambda a: a.reshape(batch, seq, a.shape[-1])
    prev = pl.BlockSpec((1, WINDOW, dk), lambda b, i: (b, jnp.maximum(i * nq - 1, 0), 0))
    cur = pl.BlockSpec((1, nq * WINDOW, dk), lambda b, i: (b, i, 0))
    table = table.reshape(KV_HEADS, GROUP * WINDOW, 2 * WINDOW)
    key = jnp.arange(2 * WINDOW)
    no_prev = jnp.where((key < WINDOW) & (key != PROMPT_SINK_KEY), NEG_INF, table)
    tables = jnp.stack([no_prev, table])
    out = pl.pallas_call(
        _attn_prompt_kernel,
        grid=(batch, seq // (nq * WINDOW)),
        in_specs=[pl.BlockSpec((1, nq * WINDOW, d), lambda b, i: (b, i, 0)),
                  prev, cur, prev, cur,
                  pl.BlockSpec(tables.shape, lambda b, i: (0, 0, 0, 0))],
        out_specs=pl.BlockSpec((1, nq * WINDOW, d), lambda b, i: (b, i, 0)),
        out_shape=jax.ShapeDtypeStruct((batch, seq, d), BF16),
        compiler_params=_cparams("parallel", "arbitrary"),
        name="attn_prompt",
    )(r3(q), r3(k), r3(k), r3(v), r3(v), tables)
    return out.reshape(batch * seq, d)


def _attn_sample_kernel(q_ref, kc_ref, kn_ref, vc_ref, vn_ref, bias_ref, o_ref,
                        *cache_out_refs, seq):
    n_b = q_ref.shape[0]
    dk = KV_HEADS * HEAD_DIM
    pad = jnp.zeros((WINDOW - seq, dk), F32)
    pairs = [(b, kh) for b in range(n_b) for kh in range(KV_HEADS)]
    kslice = lambda kh: slice(kh * HEAD_DIM, (kh + 1) * HEAD_DIM)
    head_lanes = lambda kh, g: slice((g * KV_HEADS + kh) * HEAD_DIM, (g * KV_HEADS + kh + 1) * HEAD_DIM)
    kks = [jnp.concatenate([kc_ref[b], kn_ref[b], pad], axis=0).astype(BF16) for b in range(n_b)]
    vvs = [jnp.concatenate([vc_ref[b], vn_ref[b], pad], axis=0).astype(BF16) for b in range(n_b)]
    ss = []
    for b, kh in pairs:
        qs = jnp.concatenate([q_ref[b, :, head_lanes(kh, g)] for g in range(GROUP)],
                             axis=0).astype(BF16)
        ss.append(lax.dot_general(qs, kks[b][:, kslice(kh)], (((1,), (1,)), ((), ())),
                                  preferred_element_type=F32))
    pvs, denoms = [], []
    for i, (b, kh) in enumerate(pairs):
        s = ss[i] + bias_ref[kh]
        e = jnp.exp(s - jnp.max(s, axis=1, keepdims=True))
        denoms.append(jnp.sum(e, axis=1, keepdims=True))
        pvs.append(jnp.dot(e.astype(BF16), vvs[b][:, kslice(kh)], preferred_element_type=F32))
    for i, (b, kh) in enumerate(pairs):
        o = pvs[i] / denoms[i]
        for g in range(GROUP):
            o_ref[b, :, head_lanes(kh, g)] = o[g * seq:(g + 1) * seq].astype(o_ref.dtype)
    for out_ref, old_ref, new_ref in zip(cache_out_refs, (kc_ref, vc_ref), (kn_ref, vn_ref)):
        for b in range(n_b):
            out_ref[b, :WINDOW - seq, :] = old_ref[b, seq:, :]
            out_ref[b, WINDOW - seq:, :] = new_ref[b]


def attn_sample(q, k_new, v_new, cache_k, cache_v, table, batch, seq, write_cache):
    d = Q_HEADS * HEAD_DIM
    dk = KV_HEADS * HEAD_DIM
    nbk = ATTN_SAMPLE_BLOCK
    new = pl.BlockSpec((nbk, seq, dk), lambda i: (i, 0, 0))
    old = pl.BlockSpec((nbk, WINDOW, dk), lambda i: (i, 0, 0))
    n_cache = 2 if write_cache else 0
    out = pl.pallas_call(
        functools.partial(_attn_sample_kernel, seq=seq),
        grid=(batch // nbk,),
        in_specs=[pl.BlockSpec((nbk, seq, d), lambda i: (i, 0, 0)),
                  old, new, old, new,
                  pl.BlockSpec((KV_HEADS, GROUP * seq, 2 * WINDOW), lambda i: (0, 0, 0))],
        out_specs=[pl.BlockSpec((nbk, seq, d), lambda i: (i, 0, 0))] + [old] * n_cache,
        out_shape=[jax.ShapeDtypeStruct((batch, seq, d), F32)]
                  + [jax.ShapeDtypeStruct((batch, WINDOW, dk), F32)] * n_cache,
        compiler_params=_cparams("parallel"),
        name="attn_sample",
    )(q.reshape(batch, seq, d), cache_k.reshape(batch, WINDOW, dk),
      k_new.reshape(batch, seq, dk), cache_v.reshape(batch, WINDOW, dk),
      v_new.reshape(batch, seq, dk), table.reshape(KV_HEADS, GROUP * seq, 2 * WINDOW))
    return (out[0].reshape(batch * seq, d),) + tuple(out[1:])


def _trunk(x, state, cache, w, tm, tm_mlp):
    batch, seq, d = x.shape
    x = x.reshape(batch * seq, d)
    hq, hv = N_HEADS * DQK, N_HEADS * DV
    depth = w["norm_mix"].shape[0]
    n_a = w["w_in"].shape[0]
    prompt = state is None
    act = BF16 if prompt else F32
    cs, ns, ms = [], [], []
    c_stack = None if prompt else state[0]
    for l in range(depth):
        if l < n_a:
            gates = (w["b_igate"][l], w["b_fgate"][l], w["mlstm_norm"][l])
            q, k, v, o, gate = mlstm_inproj(x, w["norm_mix"], w["w_in"], l, tm, prompt, act)
            if prompt:
                a, c_new, n_new, m_new = mlstm_prompt(q, k, v, o, gate, *gates, batch, seq)
                cs.append(c_new)
            else:
                a, c_stack, n_new, m_new = mlstm_sample(q, k, v, o, gate, *gates, state[0], c_stack,
                                                        l, state[1][l], state[2][l], batch, seq)
            ns.append(n_new); ms.append(m_new)
            w_o, lo = w["w_mlstm_out"], l
        else:
            j = l - n_a
            if j == 0:
                dk = KV_HEADS * HEAD_DIM
                k_new, v_new, k16, v16 = norm_matmul(
                    x, w["kv_norm"], 0, w["w_kv"], 0, ((0, dk), (dk, dk), (0, dk), (dk, dk)),
                    (F32, F32, BF16, BF16), tm)
                tables = bias_table(w["rel_bias"], w["attn_sinks"], WINDOW if prompt else seq,
                                    2 * WINDOW, PROMPT_SINK_KEY if prompt else 2 * WINDOW - 1)
            (q,) = norm_matmul(x, w["norm_mix"], l, w["w_q"], j, ((0, Q_HEADS * HEAD_DIM),),
                               (act,), tm)
            if prompt:
                a = attn_prompt(q, k16, v16, tables[j], batch, seq)
            elif j == 0:
                a, win_k, win_v = attn_sample(q, k_new, v_new, cache[0], cache[1], tables[j],
                                              batch, seq, True)
            else:
                (a,) = attn_sample(q, k_new, v_new, cache[0], cache[1], tables[j],
                                   batch, seq, False)
            w_o, lo = w["w_attn_out"], j
        x = mlp(a, w_o, lo, x, w["norm_ffn"], w["w_up"], w["w_down"], l, w["final_norm"],
                l == depth - 1, tm_mlp, MLP_FF_BLOCK)
    if prompt:
        dk = KV_HEADS * HEAD_DIM
        win_k = k_new.reshape(batch, seq, dk)[:, -WINDOW:]
        win_v = v_new.reshape(batch, seq, dk)[:, -WINDOW:]
        c_stack = jnp.stack(cs)
    shp = (batch, WINDOW, KV_HEADS, HEAD_DIM)
    return (x.reshape(batch, seq, d), c_stack, jnp.stack(ns), jnp.stack(ms),
            win_k.reshape(shp), win_v.reshape(shp))


def kernel(x_prompt, x_sample, state_C, state_n, state_m, cache_k, cache_v, norm_mix, norm_ffn,
           w_mlstm_in, b_igate, b_fgate, mlstm_norm, w_mlstm_out, kv_norm, w_kv, w_q, attn_sinks,
           w_attn_out, rel_bias, w_up, w_down, final_norm):
    n_b, d = w_q.shape[0], w_q.shape[1]
    heads = (KV_HEADS, GROUP, HEAD_DIM)
    w_q_perm = (w_q * HEAD_DIM ** -0.5).astype(BF16).reshape((n_b, d) + heads)
    w_q_perm = w_q_perm.transpose(0, 1, 3, 2, 4).reshape(n_b, d, d)
    w_ao_perm = w_attn_out.astype(BF16).reshape((n_b,) + heads + (d,))
    w_ao_perm = w_ao_perm.transpose(0, 2, 1, 3, 4).reshape(n_b, d, d)
    w = dict(norm_mix=norm_mix[:, None, :], norm_ffn=norm_ffn[:, None, :],
             w_in=w_mlstm_in.astype(BF16), b_igate=b_igate, b_fgate=b_fgate,
             mlstm_norm=mlstm_norm, w_mlstm_out=w_mlstm_out.astype(BF16),
             kv_norm=kv_norm[None, None, :], w_kv=w_kv.astype(BF16)[None],
             w_q=w_q_perm, attn_sinks=attn_sinks, w_attn_out=w_ao_perm, rel_bias=rel_bias,
             w_up=w_up.astype(BF16), w_down=w_down.astype(BF16), final_norm=final_norm)
    y_p, c_p, n_p, m_p, k_p, v_p = _trunk(x_prompt, None, None, w, 512, 1024)
    y_s, c_s, n_s, m_s, k_s, v_s = _trunk(x_sample, (state_C, state_n, state_m),
                                          (cache_k, cache_v), w, 512, 512)
    return (y_p, y_s, c_p, n_p, m_p, k_p, v_p, c_s, n_s, m_s, k_s, v_s)
```

```python
import functools
import math

import jax
import jax.numpy as jnp
from jax import lax
from jax.experimental import pallas as pl
from jax.experimental.pallas import tpu as pltpu

F32 = jnp.float32
BF16 = jnp.bfloat16
EPS = 1e-6
NEG_INF = float("-inf")

N_HEADS = 8
DQK = 64
DV = 128
Q_HEADS = 16
KV_HEADS = 4
GROUP = Q_HEADS // KV_HEADS
HEAD_DIM = 64
WINDOW = 128
N_BUCKETS = 32
MAX_EXACT = N_BUCKETS // 2
MAX_DISTANCE = 128

LANES = 128
VMEM_LIMIT = 48 * 1024 * 1024

PROMPT_CHUNK = 128
PROMPT_BLOCK = 256
SAMPLE_BATCH_BLOCK = 8
ATTN_SAMPLE_BLOCK = 8
ATTN_PROMPT_BLOCKS = 2
MLP_ROW_CHUNK = 512
MLP_FF_BLOCK = 1024


def _cparams(*sem):
    return pltpu.CompilerParams(dimension_semantics=sem, vmem_limit_bytes=VMEM_LIMIT)


def _rms(x, g):
    return x * lax.rsqrt(jnp.mean(x * x, axis=-1, keepdims=True) + EPS) * g


def _split3(x):
    hi = x.astype(BF16)
    r = x - hi.astype(F32)
    mid = r.astype(BF16)
    lo = (r - mid.astype(F32)).astype(BF16)
    return hi, mid, lo


def _dot01(x, onehot):
    hi, mid, lo = _split3(x)
    d = lambda a: jnp.dot(a, onehot, preferred_element_type=F32)
    return d(hi) + d(mid) + d(lo)


def _dot01_left(onehot, x):
    hi, mid, lo = _split3(x)
    d = lambda a: jnp.dot(onehot, a, preferred_element_type=F32)
    return d(hi) + d(mid) + d(lo)


def _log_sigmoid(x):
    return jnp.minimum(x, 0.0) - jnp.log1p(jnp.exp(-jnp.abs(x)))


def _sigmoid(x):
    return 1.0 / (1.0 + jnp.exp(-x))


def _norm_mm_kernel(x_ref, g_ref, w_ref, *out_refs, splits):
    y = _rms(x_ref[...], g_ref[...]).astype(BF16)
    r = jnp.dot(y, w_ref[...], preferred_element_type=F32)
    for o_ref, (off, n) in zip(out_refs, splits):
        o_ref[...] = r[:, off:off + n].astype(o_ref.dtype)


def _layer_spec(arr, layer):
    idx = (layer,) + (0,) * (arr.ndim - 1)
    return pl.BlockSpec((None,) + arr.shape[1:], lambda *_: idx)


def norm_matmul(x, g, lg, w, lw, splits, dtypes, tm):
    m, d = x.shape
    assert m % tm == 0 and all(off + n <= w.shape[-1] for off, n in splits)
    return pl.pallas_call(
        functools.partial(_norm_mm_kernel, splits=splits),
        grid=(m // tm,),
        in_specs=[pl.BlockSpec((tm, d), lambda i: (i, 0)),
                  _layer_spec(g, lg), _layer_spec(w, lw)],
        out_specs=[pl.BlockSpec((tm, n), lambda i: (i, 0)) for _, n in splits],
        out_shape=[jax.ShapeDtypeStruct((m, n), dt) for (_, n), dt in zip(splits, dtypes)],
        compiler_params=_cparams("parallel"),
        name="norm_matmul",
    )(x, g, w)


def _mlp_kernel(a_ref, wo_ref, x_ref, g_ref, wu_ref, wd_ref, gf_ref, o_ref, xn_ref, *,
                final_norm, row_chunk):
    j = pl.program_id(1)
    tm = x_ref.shape[0]

    @pl.when(j == 0)
    def _():
        for r in range(0, tm, row_chunk):
            rows = slice(r, r + row_chunk)
            x = x_ref[rows, :] + jnp.dot(a_ref[rows, :].astype(BF16), wo_ref[...],
                                         preferred_element_type=F32)
            xn_ref[rows, :] = _rms(x, g_ref[...]).astype(BF16)
            o_ref[rows, :] = x

    h = jnp.dot(xn_ref[...], wu_ref[...], preferred_element_type=F32)
    h = jnp.maximum(h, 0.0)
    o_ref[...] += jnp.dot((h * h).astype(BF16), wd_ref[...], preferred_element_type=F32)

    if final_norm:
        @pl.when(j == pl.num_programs(1) - 1)
        def _():
            o_ref[...] = _rms(o_ref[...], gf_ref[...])


def mlp(a, w_o, lo, x, g, w_up, w_down, l, g_final, final_norm, tm, tf):
    m, d = x.shape
    ff = w_up.shape[-1]
    return pl.pallas_call(
        functools.partial(_mlp_kernel, final_norm=final_norm, row_chunk=min(tm, MLP_ROW_CHUNK)),
        grid=(m // tm, ff // tf),
        in_specs=[pl.BlockSpec((tm, d), lambda i, j: (i, 0)),
                  _layer_spec(w_o, lo),
                  pl.BlockSpec((tm, d), lambda i, j: (i, 0)),
                  _layer_spec(g, l),
                  pl.BlockSpec((None, d, tf), lambda i, j: (l, 0, j)),
                  pl.BlockSpec((None, tf, d), lambda i, j: (l, j, 0)),
                  pl.BlockSpec((1, d), lambda i, j: (0, 0))],
        out_specs=pl.BlockSpec((tm, d), lambda i, j: (i, 0)),
        out_shape=jax.ShapeDtypeStruct((m, d), F32),
        scratch_shapes=[pltpu.VMEM((tm, d), BF16)],
        compiler_params=_cparams("parallel", "arbitrary"),
        name="mlp",
    )(a, w_o, x, g, w_up, w_down, g_final.reshape(1, d))


def _mlstm_inproj_kernel(x_ref, g_ref, w_ref, q_ref, k_ref, v_ref, o_ref, gate_ref, *, k_transposed):
    hq, hv = N_HEADS * DQK, N_HEADS * DV
    tm = x_ref.shape[0]
    y = _rms(x_ref[...], g_ref[...]).astype(BF16)
    r = jnp.dot(y, w_ref[...], preferred_element_type=F32)
    q_ref[...] = r[:, :hq].astype(q_ref.dtype)
    k = r[:, hq:2 * hq] * DQK ** -0.5
    k_ref[...] = (k.T if k_transposed else k).astype(k_ref.dtype)
    v_ref[...] = r[:, 2 * hq:2 * hq + hv].astype(v_ref.dtype)
    o_ref[...] = r[:, 2 * hq + hv:2 * hq + 2 * hv]
    gate_ref[...] = jnp.concatenate(
        [r[:, 2 * hq + 2 * hv:], jnp.zeros((tm, LANES - 2 * N_HEADS), F32)], axis=1)


def mlstm_inproj(x, g, w, l, tm, k_transposed, act):
    m, d = x.shape
    hq, hv = N_HEADS * DQK, N_HEADS * DV
    assert w.shape[-1] == 2 * hq + 2 * hv + 2 * N_HEADS
    row = lambda width: pl.BlockSpec((tm, width), lambda i: (i, 0))
    k_spec = pl.BlockSpec((hq, tm), lambda i: (0, i)) if k_transposed else row(hq)
    k_shape = (hq, m) if k_transposed else (m, hq)
    return pl.pallas_call(
        functools.partial(_mlstm_inproj_kernel, k_transposed=k_transposed),
        grid=(m // tm,),
        in_specs=[row(d), _layer_spec(g, l), _layer_spec(w, l)],
        out_specs=[row(hq), k_spec, row(hv), row(hv), row(LANES)],
        out_shape=[jax.ShapeDtypeStruct((m, hq), act), jax.ShapeDtypeStruct(k_shape, act),
                   jax.ShapeDtypeStruct((m, hv), act), jax.ShapeDtypeStruct((m, hv), F32),
                   jax.ShapeDtypeStruct((m, LANES), F32)],
        compiler_params=_cparams("parallel"),
        name="mlstm_inproj",
    )(x, g, w)


def _chunk_scan(x, pos, op, fill, length):
    k = 1
    while k < length:
        x = op(x, jnp.where(pos >= k, pltpu.roll(x, k, 0), fill))
        k *= 2
    return x


def _mlstm_prompt_kernel(q_ref, kt_ref, v_ref, o_ref, gate_ref, bi_ref, bf_ref, gain_ref,
                         hg_ref, cx_out_ref, m_out_ref, cx_s, m_s, *, chunk):
    t_blk = q_ref.shape[1]
    n_chunks = t_blk // chunk
    step = pl.program_id(1)

    @pl.when(step == 0)
    def _():
        cx_s[...] = jnp.zeros_like(cx_s)
        m_s[...] = jnp.zeros_like(m_s)

    gates = gate_ref[0]
    li = gates + bi_ref[...]
    lf = _log_sigmoid(pltpu.roll(gates, LANES - N_HEADS, 1) + bf_ref[...])
    pos = lax.broadcasted_iota(jnp.int32, (t_blk, 1), 0) % chunk
    b = _chunk_scan(lf, pos, jnp.add, 0.0, chunk)
    c = li - b
    cm = _chunk_scan(c, pos, jnp.maximum, NEG_INF, chunk)

    m_prev = m_s[...]
    xs, w_inters, e_negms, wks = [], [], [], []
    for ck in range(n_chunks):
        sl = slice(ck * chunk, (ck + 1) * chunk)
        m_t = b[sl] + jnp.maximum(m_prev, cm[sl])
        m_new = m_t[chunk - 1:chunk]
        b_last = b[(ck + 1) * chunk - 1:(ck + 1) * chunk]
        xs.append(b[sl] - m_t)
        w_inters.append(jnp.exp(b[sl] + m_prev - m_t))
        e_negms.append(jnp.exp(-m_t))
        wks.append(jnp.exp(c[sl] + (b_last - m_new)))
        m_prev = m_new
    m_s[...] = m_prev
    c_t = c.T
    wk_t = jnp.concatenate(wks, axis=0).T

    ri = lax.broadcasted_iota(jnp.int32, (chunk, chunk), 0)
    ci = lax.broadcasted_iota(jnp.int32, (chunk, chunk), 1)
    causal = ci <= ri
    ones_blk = jnp.ones((chunk, DV), BF16)
    ones_sq = jnp.ones((DV, DV), BF16)
    heads = range(N_HEADS)
    dot = functools.partial(jnp.dot, preferred_element_type=F32)

    cxs = [cx_s[h] for h in heads]
    for ck in range(n_chunks):
        sl = slice(ck * chunk, (ck + 1) * chunk)
        qs = [q_ref[0, sl, h * DQK:(h + 1) * DQK] for h in heads]
        kts = [kt_ref[h * DQK:(h + 1) * DQK, sl] for h in heads]
        vs = [v_ref[0, sl, h * DV:(h + 1) * DV] for h in heads]
        cx16s = [cxs[h].astype(BF16) for h in heads]
        kws = [(kts[h].astype(F32) * wk_t[h:h + 1, sl]).astype(BF16) for h in heads]
        vones = [jnp.concatenate([vs[h], ones_blk], axis=1) for h in heads]
        ss = [dot(qs[h], kts[h]) for h in heads]
        qcs = [dot(qs[h], cx16s[h]) for h in heads]
        dcs = [dot(kws[h], vones[h]) for h in heads]
        cxs = [w_inters[ck][chunk - 1:chunk, h:h + 1] * cxs[h] + dcs[h] for h in heads]
        a16s = []
        for h in heads:
            dm = xs[ck][:, h:h + 1] + c_t[h:h + 1, sl]
            a16s.append((ss[h] * jnp.exp(jnp.where(causal, dm, NEG_INF))).astype(BF16))
        avs = [dot(a16s[h], vones[h]) for h in heads]
        hhs = []
        for h in heads:
            wi = jnp.broadcast_to(w_inters[ck][:, h:h + 1], (chunk, DV))
            en = jnp.broadcast_to(e_negms[ck][:, h:h + 1], (chunk, DV))
            den = avs[h][:, DV:] + wi * qcs[h][:, DV:]
            inv = 1.0 / jnp.maximum(jnp.abs(den), en)
            hhs.append((avs[h][:, :DV] + qcs[h][:, :DV] * wi) * inv)
        sq16s = [(hhs[h] * hhs[h]).astype(BF16) for h in heads]
        mss = [dot(sq16s[h], ones_sq) for h in heads]
        for h in heads:
            hn = hhs[h] * lax.rsqrt(mss[h] * (1.0 / DV) + EPS)
            hn = hn * gain_ref[:, h * DV:(h + 1) * DV]
            og = _sigmoid(o_ref[0, sl, h * DV:(h + 1) * DV])
            hg_ref[0, sl, h * DV:(h + 1) * DV] = (og * hn).astype(hg_ref.dtype)
    for h in heads:
        cx_s[h] = cxs[h]

    @pl.when(step == pl.num_programs(1) - 1)
    def _():
        cx_out_ref[0] = cx_s[...]
        m_out_ref[0] = m_s[...]


def mlstm_prompt(q, kt, v, o, gate, b_i, b_f, gain, batch, seq):
    hq, hv = N_HEADS * DQK, N_HEADS * DV
    t = PROMPT_BLOCK
    nblk = seq // t
    r3 = lambda a: a.reshape(batch, seq, a.shape[-1])
    pad8 = lambda a: jnp.pad(a.reshape(1, N_HEADS), ((0, 0), (0, LANES - N_HEADS)))
    tok = lambda w: pl.BlockSpec((1, t, w), lambda bb, s: (bb, s, 0))
    cst = lambda w: pl.BlockSpec((1, w), lambda bb, s: (0, 0))
    hg, cx, m_new = pl.pallas_call(
        functools.partial(_mlstm_prompt_kernel, chunk=PROMPT_CHUNK),
        grid=(batch, nblk),
        in_specs=[tok(hq), pl.BlockSpec((hq, t), lambda bb, s: (0, bb * nblk + s)),
                  tok(hv), tok(hv), tok(LANES), cst(LANES), cst(LANES), cst(hv)],
        out_specs=[tok(hv),
                   pl.BlockSpec((1, N_HEADS, DQK, 2 * DV), lambda bb, s: (bb, 0, 0, 0)),
                   pl.BlockSpec((1, 1, LANES), lambda bb, s: (bb, 0, 0))],
        out_shape=[jax.ShapeDtypeStruct((batch, seq, hv), BF16),
                   jax.ShapeDtypeStruct((batch, N_HEADS, DQK, 2 * DV), F32),
                   jax.ShapeDtypeStruct((batch, 1, LANES), F32)],
        scratch_shapes=[pltpu.VMEM((N_HEADS, DQK, 2 * DV), F32), pltpu.VMEM((1, LANES), F32)],
        compiler_params=_cparams("parallel", "arbitrary"),
        name="mlstm_prompt",
    )(r3(q), kt, r3(v), r3(o), r3(gate), pad8(b_i), pad8(b_f), gain.reshape(1, hv))
    return hg.reshape(batch * seq, hv), cx[..., :DV], cx[..., DV], m_new[:, 0, :N_HEADS]


def _mlstm_sample_kernel(q_ref, k_ref, v_ref, o_ref, gate_ref, m0_ref, n0_ref, c0_ref, c_carry_ref,
                         bi_ref, bf_ref, gain_ref, seg64_ref, seg128_ref, e64_ref, e128_ref,
                         hg_ref, c_out_ref, n_out_ref, m_out_ref,
                         qc_s, *, seq):
    del c_carry_ref
    spans_layers = len(c_out_ref.shape) == 5
    c_new_ref = c_out_ref.at[0] if spans_layers else c_out_ref
    if spans_layers:
        for later in range(1, c_out_ref.shape[0]):
            c_out_ref[later] = jnp.zeros(c_out_ref.shape[1:], F32)
    rows = q_ref.shape[0]
    n_b = rows // seq
    tpos = lax.broadcasted_iota(jnp.int32, (rows, 1), 0) % seq
    dot = functools.partial(jnp.dot, preferred_element_type=F32)

    def shift(x, d):
        return x if d == 0 else pltpu.roll(x, d, 0)

    def unshift(x, d):
        return x if d == 0 else pltpu.roll(x, rows - d, 0)

    def split_cat(x):
        return jnp.concatenate(_split3(x), axis=1)


    q = q_ref[...].astype(BF16).astype(F32)
    k = k_ref[...].astype(BF16).astype(F32)
    v = v_ref[...].astype(BF16).astype(F32)

    units = [(p, h) for p in range(n_b // 2) for h in range(N_HEADS)]
    prow = lambda p: slice(p * 2 * seq, (p + 1) * 2 * seq)
    q_pairs = [q[prow(p), h * DQK:(h + 1) * DQK].astype(BF16) for p, h in units]
    c_cats = [jnp.concatenate([c0_ref[2 * p, h], c0_ref[2 * p + 1, h]], axis=1).astype(BF16)
              for p, h in units]
    qcs = [dot(q_pairs[i], c_cats[i]) for i in range(len(units))]

    gates = gate_ref[...]
    li = gates + bi_ref[...]
    lf = _log_sigmoid(pltpu.roll(gates, LANES - N_HEADS, 1) + bf_ref[...])
    b = lf
    for d in range(1, seq):
        b = b + jnp.where(tpos >= d, shift(lf, d), 0.0)
    inter = b + m0_ref[...]
    dvals = []
    m_t = inter
    for d in range(seq):
        dd = jnp.where(tpos >= d, b - shift(b, d) + shift(li, d), NEG_INF)
        dvals.append(dd)
        m_t = jnp.maximum(m_t, dd)
    w_inter = jnp.exp(inter - m_t)
    w_intra = [jnp.exp(dd - m_t) for dd in dvals]
    last = lambda x: functools.reduce(
        lambda acc, d: jnp.where(tpos == seq - 1 - d, unshift(x, d), acc), range(1, seq), x)
    wk = jnp.exp(last(b) - b + li - last(m_t))

    qk_in = [split_cat(q * shift(k, d)) for d in range(seq)]
    qn_in = split_cat(q * n0_ref[...].astype(BF16).astype(F32))
    wk_in, wi_in = split_cat(wk), split_cat(w_inter)
    seg64, e64, e128 = seg64_ref[...], e64_ref[...], e128_ref[...]
    qks = [dot(x, seg64) for x in qk_in]
    qn = dot(qn_in, seg64)
    wk_exp = dot(wk_in, e64)
    decay64 = dot(wi_in, e64)
    decay_exp = dot(wi_in, e128)

    a_s = [qks[d] * w_intra[d] for d in range(seq)]
    e128_1 = e128[:LANES]
    a_exps = [dot(a.astype(BF16), e128_1) for a in a_s]
    den = w_inter * qn
    num = jnp.zeros((rows, N_HEADS * DV), F32)
    for d in range(seq):
        den = den + a_s[d]
        num = num + a_exps[d] * shift(v, d)

    kw = k * wk_exp
    ksum = kw
    for d in range(1, seq):
        ksum = ksum + shift(kw, d)
    n_rows = decay64 * n0_ref[...] + ksum
    own = ((lax.broadcasted_iota(jnp.int32, (2 * seq, 2 * DV), 0) < seq)
           == (lax.broadcasted_iota(jnp.int32, (2 * seq, 2 * DV), 1) < DV))
    kw_pairs = [kw[prow(p), h * DQK:(h + 1) * DQK].astype(BF16) for p, h in units]
    v2s = []
    for p, h in units:
        v_pair = v[prow(p), h * DV:(h + 1) * DV]
        v2s.append(jnp.where(own, jnp.concatenate([v_pair, v_pair], axis=1), 0.0).astype(BF16))
    sel3 = (lax.broadcasted_iota(jnp.int32, (n_b, 3 * rows), 1) % rows
            == lax.broadcasted_iota(jnp.int32, (n_b, 3 * rows), 0) * seq + (seq - 1))
    sel3 = jnp.where(sel3, 1.0, 0.0).astype(BF16)
    n_in = jnp.concatenate(_split3(n_rows), axis=0)
    m_in = jnp.concatenate(_split3(m_t), axis=0)
    d_cs = [lax.dot_general(kw_pairs[i], v2s[i], (((0,), (0,)), ((), ())),
                            preferred_element_type=F32) for i in range(len(units))]
    n_out_ref[...] = dot(sel3, n_in)
    m_out_ref[...] = dot(sel3, m_in)

    first = lax.broadcasted_iota(jnp.int32, (2 * seq, DV), 0) < seq
    for i, (p, h) in enumerate(units):
        r0 = p * 2 * seq
        qc_s[prow(p), h * DV:(h + 1) * DV] = jnp.where(first, qcs[i][:, :DV], qcs[i][:, DV:])
        dec_a = decay_exp[r0 + seq - 1:r0 + seq, h * DV:(h + 1) * DV]
        dec_b = decay_exp[r0 + 2 * seq - 1:r0 + 2 * seq, h * DV:(h + 1) * DV]
        c_new_ref[2 * p, h] = dec_a * c0_ref[2 * p, h] + d_cs[i][:, :DV]
        c_new_ref[2 * p + 1, h] = dec_b * c0_ref[2 * p + 1, h] + d_cs[i][:, DV:]

    num = num + qc_s[...] * decay_exp
    inv_scale = 1.0 / jnp.maximum(jnp.abs(den), jnp.exp(-m_t))
    hh = num * dot(split_cat(inv_scale), e128)
    ms = dot(split_cat(hh * hh), seg128_ref[...]) * (1.0 / DV)
    hn = hh * dot(split_cat(lax.rsqrt(ms + EPS)), e128) * gain_ref[...]
    hg_ref[...] = (_sigmoid(o_ref[...]) * hn).astype(hg_ref.dtype)


def mlstm_sample(q, k, v, o, gate, b_i, b_f, gain, state_c, c_carry, l, n0, m0, batch, seq):
    hq, hv = N_HEADS * DQK, N_HEADS * DV
    c_blk = pl.BlockSpec((None, SAMPLE_BATCH_BLOCK, N_HEADS, DQK, DV), lambda i: (l, i, 0, 0, 0))
    c_all = pl.BlockSpec((state_c.shape[0], SAMPLE_BATCH_BLOCK, N_HEADS, DQK, DV),
                         lambda i: (0, i, 0, 0, 0))
    rows = SAMPLE_BATCH_BLOCK * seq
    pad8 = lambda a: jnp.pad(a.reshape(1, N_HEADS), ((0, 0), (0, LANES - N_HEADS)))
    m0_rows = jnp.pad(jnp.repeat(m0, seq, axis=0), ((0, 0), (0, LANES - N_HEADS)))
    n0_rows = jnp.repeat(n0.reshape(batch, hq), seq, axis=0)
    lane = jnp.arange(LANES)
    seg64 = (jnp.arange(hq)[:, None] // DQK == lane[None, :]).astype(BF16)
    seg128 = (jnp.arange(hv)[:, None] // DV == lane[None, :]).astype(BF16)
    tok = lambda w: pl.BlockSpec((rows, w), lambda i: (i, 0))
    cst = lambda a: pl.BlockSpec(a.shape, lambda i: (0,) * a.ndim)
    x3 = lambda a: jnp.tile(a, (3, 1))
    consts = [pad8(b_i), pad8(b_f), gain.reshape(1, hv),
              x3(seg64), x3(seg128), x3(seg64.T), x3(seg128.T)]
    hg, c_new, n_new, m_new = pl.pallas_call(
        functools.partial(_mlstm_sample_kernel, seq=seq),
        grid=(batch // SAMPLE_BATCH_BLOCK,),
        in_specs=[tok(hq), tok(hq), tok(hv), tok(hv), tok(LANES), tok(LANES), tok(hq),
                  c_blk, pl.BlockSpec(memory_space=pl.ANY)]
                 + [cst(a) for a in consts],
        out_specs=[tok(hv), c_all if l == 0 else c_blk,
                   pl.BlockSpec((SAMPLE_BATCH_BLOCK, hq), lambda i: (i, 0)),
                   pl.BlockSpec((SAMPLE_BATCH_BLOCK, LANES), lambda i: (i, 0))],
        out_shape=[jax.ShapeDtypeStruct((batch * seq, hv), BF16),
                   jax.ShapeDtypeStruct(state_c.shape, F32),
                   jax.ShapeDtypeStruct((batch, hq), F32),
                   jax.ShapeDtypeStruct((batch, LANES), F32)],
        scratch_shapes=[pltpu.VMEM((rows, hv), F32)],
        input_output_aliases={8: 1} if l > 0 else {},
        compiler_params=_cparams("parallel"),
        name="mlstm_sample",
    )(q, k, v, o, gate, m0_rows, n0_rows, state_c, c_carry, *consts)
    return hg, c_new, n_new.reshape(batch, N_HEADS, DQK), m_new[:, :N_HEADS]


def _bias_table_kernel(rb_ref, sink_ref, o_ref, *, sink_col):
    n_layers, _, nq, ns = o_ref.shape
    col = lax.broadcasted_iota(jnp.int32, (nq, ns), 1)
    dist = lax.broadcasted_iota(jnp.int32, (nq, ns), 0) + WINDOW - col
    n = jnp.maximum(dist, 0)
    large = MAX_EXACT + (jnp.log(jnp.maximum(n, 1).astype(F32) / MAX_EXACT)
                         / math.log(MAX_DISTANCE / MAX_EXACT) * (N_BUCKETS - MAX_EXACT)).astype(jnp.int32)
    large = jnp.minimum(large, N_BUCKETS - 1)
    bucket = jnp.where(n < MAX_EXACT, n, large)
    valid = (dist >= 0) & (dist < WINDOW)
    for h in range(Q_HEADS):
        acc = jnp.zeros((nq, ns), F32)
        for bkt in range(N_BUCKETS):
            acc = jnp.where(bucket == bkt, rb_ref[bkt, h], acc)
        acc = jnp.where(valid, acc, NEG_INF)
        for layer in range(n_layers):
            o_ref[layer, h] = jnp.where(col == sink_col, sink_ref[layer, h], acc)


def bias_table(rel_bias, sinks, nq, ns, sink_col):
    return pl.pallas_call(
        functools.partial(_bias_table_kernel, sink_col=sink_col),
        in_specs=[pl.BlockSpec(memory_space=pltpu.SMEM), pl.BlockSpec(memory_space=pltpu.SMEM)],
        out_specs=pl.BlockSpec(memory_space=pltpu.VMEM),
        out_shape=jax.ShapeDtypeStruct((sinks.shape[0], Q_HEADS, nq, ns), F32),
        name="bias_table",
    )(rel_bias, sinks)


PROMPT_SINK_KEY = 0


def _attn_prompt_kernel(q_ref, kp_ref, kc_ref, vp_ref, vc_ref, bias_ref, o_ref):
    dk = KV_HEADS * HEAD_DIM
    kvh = range(KV_HEADS)
    sink_row = lax.broadcasted_iota(jnp.int32, (2 * WINDOW, dk), 0) == PROMPT_SINK_KEY
    lane_head = lax.broadcasted_iota(jnp.int32, (2 * WINDOW, dk), 1) // HEAD_DIM
    out_head = lax.broadcasted_iota(jnp.int32, (GROUP * WINDOW, dk), 1) // HEAD_DIM
    zero = jnp.zeros((), BF16)
    first_of_sequence = pl.program_id(1) == 0
    for j in range(q_ref.shape[1] // WINDOW):
        rows = slice(j * WINDOW, (j + 1) * WINDOW)
        before = slice((j - 1) * WINDOW, j * WINDOW)
        k_prev, v_prev = (kp_ref[0], vp_ref[0]) if j == 0 else (kc_ref[0, before], vc_ref[0, before])
        variant = jnp.where(first_of_sequence, 0, 1) if j == 0 else 1
        k_all = jnp.where(sink_row, zero, jnp.concatenate([k_prev, kc_ref[0, rows]], axis=0))
        v_all = jnp.where(sink_row, zero, jnp.concatenate([v_prev, vc_ref[0, rows]], axis=0))
        k_bd = jnp.concatenate([jnp.where(lane_head == kh, k_all, zero) for kh in kvh], axis=0)
        v_bd = jnp.concatenate([jnp.where(lane_head == kh, v_all, zero) for kh in kvh], axis=0)
        q_cat = jnp.concatenate([q_ref[0, rows, g * dk:(g + 1) * dk] for g in range(GROUP)],
                                axis=0)
        s = lax.dot_general(q_cat, k_bd, (((1,), (1,)), ((), ())), preferred_element_type=F32)
        es, invs = [], []
        for kh in kvh:
            sk = s[:, kh * 2 * WINDOW:(kh + 1) * 2 * WINDOW] + bias_ref[variant, kh]
            e = jnp.exp(sk - jnp.max(sk, axis=1, keepdims=True))
            invs.append(1.0 / jnp.sum(e, axis=1, keepdims=True))
            es.append(e.astype(BF16))
        pv = jnp.dot(jnp.concatenate(es, axis=1), v_bd, preferred_element_type=F32)
        inv = invs[KV_HEADS - 1]
        for kh in range(KV_HEADS - 1):
            inv = jnp.where(out_head == kh, invs[kh], inv)
        o = pv * inv
        for g in range(GROUP):
            o_ref[0, rows, g * dk:(g + 1) * dk] = o[g * WINDOW:(g + 1) * WINDOW].astype(o_ref.dtype)


def attn_prompt(q, k, v, table, batch, seq):
    d = Q_HEADS * HEAD_DIM
    dk = KV_HEADS * HEAD_DIM
    nq = ATTN_PROMPT_BLOCKS
    r3 = lambda a: a.reshape(batch, seq, a.shape[-1])
    prev = pl.BlockSpec((1, WINDOW, dk), lambda b, i: (b, jnp.maximum(i * nq - 1, 0), 0))
    cur = pl.BlockSpec((1, nq * WINDOW, dk), lambda b, i: (b, i, 0))
    table = table.reshape(KV_HEADS, GROUP * WINDOW, 2 * WINDOW)
    key = jnp.arange(2 * WINDOW)
    no_prev = jnp.where((key < WINDOW) & (key != PROMPT_SINK_KEY), NEG_INF, table)
    tables = jnp.stack([no_prev, table])
    out = pl.pallas_call(
        _attn_prompt_kernel,
        grid=(batch, seq // (nq * WINDOW)),
        in_specs=[pl.BlockSpec((1, nq * WINDOW, d), lambda b, i: (b, i, 0)),
                  prev, cur, prev, cur,
                  pl.BlockSpec(tables.shape, lambda b, i: (0, 0, 0, 0))],
        out_specs=pl.BlockSpec((1, nq * WINDOW, d), lambda b, i: (b, i, 0)),
        out_shape=jax.ShapeDtypeStruct((batch, seq, d), BF16),
        compiler_params=_cparams("parallel", "arbitrary"),
        name="attn_prompt",
    )(r3(q), r3(k), r3(k), r3(v), r3(v), tables)
    return out.reshape(batch * seq, d)


def _attn_sample_kernel(q_ref, kc_ref, kn_ref, vc_ref, vn_ref, bias_ref, o_ref,
                        *cache_out_refs, seq):
    n_b = q_ref.shape[0]
    dk = KV_HEADS * HEAD_DIM
    pad = jnp.zeros((WINDOW - seq, dk), F32)
    pairs = [(b, kh) for b in range(n_b) for kh in range(KV_HEADS)]
    kslice = lambda kh: slice(kh * HEAD_DIM, (kh + 1) * HEAD_DIM)
    head_lanes = lambda kh, g: slice((g * KV_HEADS + kh) * HEAD_DIM, (g * KV_HEADS + kh + 1) * HEAD_DIM)
    kks = [jnp.concatenate([kc_ref[b], kn_ref[b], pad], axis=0).astype(BF16) for b in range(n_b)]
    vvs = [jnp.concatenate([vc_ref[b], vn_ref[b], pad], axis=0).astype(BF16) for b in range(n_b)]
    ss = []
    for b, kh in pairs:
        qs = jnp.concatenate([q_ref[b, :, head_lanes(kh, g)] for g in range(GROUP)],
                             axis=0).astype(BF16)
        ss.append(lax.dot_general(qs, kks[b][:, kslice(kh)], (((1,), (1,)), ((), ())),
                                  preferred_element_type=F32))
    es, denoms = [], []
    for i, (b, kh) in enumerate(pairs):
        s = ss[i] + bias_ref[kh]
        e = jnp.exp(s - jnp.max(s, axis=1, keepdims=True))
        denoms.append(jnp.sum(e, axis=1, keepdims=True))
        es.append(e.astype(BF16))
    pvs = [jnp.dot(es[i], vvs[b][:, kslice(kh)], preferred_element_type=F32)
           for i, (b, kh) in enumerate(pairs)]
    for i, (b, kh) in enumerate(pairs):
        o = pvs[i] / denoms[i]
        for g in range(GROUP):
            o_ref[b, :, head_lanes(kh, g)] = o[g * seq:(g + 1) * seq].astype(o_ref.dtype)
    for out_ref, old_ref, new_ref in zip(cache_out_refs, (kc_ref, vc_ref), (kn_ref, vn_ref)):
        for b in range(n_b):
            out_ref[b, :WINDOW - seq, :] = old_ref[b, seq:, :]
            out_ref[b, WINDOW - seq:, :] = new_ref[b]


def attn_sample(q, k_new, v_new, cache_k, cache_v, table, batch, seq, write_cache):
    d = Q_HEADS * HEAD_DIM
    dk = KV_HEADS * HEAD_DIM
    nbk = ATTN_SAMPLE_BLOCK
    new = pl.BlockSpec((nbk, seq, dk), lambda i: (i, 0, 0))
    old = pl.BlockSpec((nbk, WINDOW, dk), lambda i: (i, 0, 0))
    n_cache = 2 if write_cache else 0
    out = pl.pallas_call(
        functools.partial(_attn_sample_kernel, seq=seq),
        grid=(batch // nbk,),
        in_specs=[pl.BlockSpec((nbk, seq, d), lambda i: (i, 0, 0)),
                  old, new, old, new,
                  pl.BlockSpec((KV_HEADS, GROUP * seq, 2 * WINDOW), lambda i: (0, 0, 0))],
        out_specs=[pl.BlockSpec((nbk, seq, d), lambda i: (i, 0, 0))] + [old] * n_cache,
        out_shape=[jax.ShapeDtypeStruct((batch, seq, d), F32)]
                  + [jax.ShapeDtypeStruct((batch, WINDOW, dk), F32)] * n_cache,
        compiler_params=_cparams("parallel"),
        name="attn_sample",
    )(q.reshape(batch, seq, d), cache_k.reshape(batch, WINDOW, dk),
      k_new.reshape(batch, seq, dk), cache_v.reshape(batch, WINDOW, dk),
      v_new.reshape(batch, seq, dk), table.reshape(KV_HEADS, GROUP * seq, 2 * WINDOW))
    return (out[0].reshape(batch * seq, d),) + tuple(out[1:])


def _trunk(x, state, cache, w, tm, tm_mlp):
    batch, seq, d = x.shape
    x = x.reshape(batch * seq, d)
    hq, hv = N_HEADS * DQK, N_HEADS * DV
    depth = w["norm_mix"].shape[0]
    n_a = w["w_in"].shape[0]
    prompt = state is None
    act = BF16 if prompt else F32
    cs, ns, ms = [], [], []
    c_stack = None if prompt else state[0]
    for l in range(depth):
        if l < n_a:
            gates = (w["b_igate"][l], w["b_fgate"][l], w["mlstm_norm"][l])
            q, k, v, o, gate = mlstm_inproj(x, w["norm_mix"], w["w_in"], l, tm, prompt, act)
            if prompt:
                a, c_new, n_new, m_new = mlstm_prompt(q, k, v, o, gate, *gates, batch, seq)
                cs.append(c_new)
            else:
                a, c_stack, n_new, m_new = mlstm_sample(q, k, v, o, gate, *gates, state[0], c_stack,
                                                        l, state[1][l], state[2][l], batch, seq)
            ns.append(n_new); ms.append(m_new)
            w_o, lo = w["w_mlstm_out"], l
        else:
            j = l - n_a
            if j == 0:
                dk = KV_HEADS * HEAD_DIM
                k_new, v_new, k16, v16 = norm_matmul(
                    x, w["kv_norm"], 0, w["w_kv"], 0, ((0, dk), (dk, dk), (0, dk), (dk, dk)),
                    (F32, F32, BF16, BF16), tm)
                tables = bias_table(w["rel_bias"], w["attn_sinks"], WINDOW if prompt else seq,
                                    2 * WINDOW, PROMPT_SINK_KEY if prompt else 2 * WINDOW - 1)
            (q,) = norm_matmul(x, w["norm_mix"], l, w["w_q"], j, ((0, Q_HEADS * HEAD_DIM),),
                               (act,), tm)
            if prompt:
                a = attn_prompt(q, k16, v16, tables[j], batch, seq)
            elif j == 0:
                a, win_k, win_v = attn_sample(q, k_new, v_new, cache[0], cache[1], tables[j],
                                              batch, seq, True)
            else:
                (a,) = attn_sample(q, k_new, v_new, cache[0], cache[1], tables[j],
                                   batch, seq, False)
            w_o, lo = w["w_attn_out"], j
        x = mlp(a, w_o, lo, x, w["norm_ffn"], w["w_up"], w["w_down"], l, w["final_norm"],
                l == depth - 1, tm_mlp, MLP_FF_BLOCK)
    if prompt:
        dk = KV_HEADS * HEAD_DIM
        win_k = k_new.reshape(batch, seq, dk)[:, -WINDOW:]
        win_v = v_new.reshape(batch, seq, dk)[:, -WINDOW:]
        c_stack = jnp.stack(cs)
    shp = (batch, WINDOW, KV_HEADS, HEAD_DIM)
    return (x.reshape(batch, seq, d), c_stack, jnp.stack(ns), jnp.stack(ms),
            win_k.reshape(shp), win_v.reshape(shp))


def kernel(x_prompt, x_sample, state_C, state_n, state_m, cache_k, cache_v, norm_mix, norm_ffn,
           w_mlstm_in, b_igate, b_fgate, mlstm_norm, w_mlstm_out, kv_norm, w_kv, w_q, attn_sinks,
           w_attn_out, rel_bias, w_up, w_down, final_norm):
    n_b, d = w_q.shape[0], w_q.shape[1]
    heads = (KV_HEADS, GROUP, HEAD_DIM)
    w_q_perm = (w_q * HEAD_DIM ** -0.5).astype(BF16).reshape((n_b, d) + heads)
    w_q_perm = w_q_perm.transpose(0, 1, 3, 2, 4).reshape(n_b, d, d)
    w_ao_perm = w_attn_out.astype(BF16).reshape((n_b,) + heads + (d,))
    w_ao_perm = w_ao_perm.transpose(0, 2, 1, 3, 4).reshape(n_b, d, d)
    w = dict(norm_mix=norm_mix[:, None, :], norm_ffn=norm_ffn[:, None, :],
             w_in=w_mlstm_in.astype(BF16), b_igate=b_igate, b_fgate=b_fgate,
             mlstm_norm=mlstm_norm, w_mlstm_out=w_mlstm_out.astype(BF16),
             kv_norm=kv_norm[None, None, :], w_kv=w_kv.astype(BF16)[None],
             w_q=w_q_perm, attn_sinks=attn_sinks, w_attn_out=w_ao_perm, rel_bias=rel_bias,
             w_up=w_up.astype(BF16), w_down=w_down.astype(BF16), final_norm=final_norm)
    y_p, c_p, n_p, m_p, k_p, v_p = _trunk(x_prompt, None, None, w, 512, 1024)
    y_s, c_s, n_s, m_s, k_s, v_s = _trunk(x_sample, (state_C, state_n, state_m),
                                          (cache_k, cache_v), w, 512, 512)
    return (y_p, y_s, c_p, n_p, m_p, k_p, v_p, c_s, n_s, m_s, k_s, v_s)
```

```python
import functools
import math

import jax
import jax.numpy as jnp
from jax import lax
from jax.experimental import pallas as pl
from jax.experimental.pallas import tpu as pltpu

F32 = jnp.float32
BF16 = jnp.bfloat16
EPS = 1e-6
NEG_INF = float("-inf")

N_HEADS = 8
DQK = 64
DV = 128
Q_HEADS = 16
KV_HEADS = 4
GROUP = Q_HEADS // KV_HEADS
HEAD_DIM = 64
WINDOW = 128
N_BUCKETS = 32
MAX_EXACT = N_BUCKETS // 2
MAX_DISTANCE = 128

LANES = 128
VMEM_LIMIT = 48 * 1024 * 1024

PROMPT_CHUNK = 128
PROMPT_BLOCK = 512
SAMPLE_BATCH_BLOCK = 8
ATTN_SAMPLE_BLOCK = 8
ATTN_PROMPT_BLOCKS = 4
MLP_ROW_CHUNK = 512
MLP_FF_BLOCK = 1024


def _cparams(*sem):
    return pltpu.CompilerParams(dimension_semantics=sem, vmem_limit_bytes=VMEM_LIMIT)


def _rms(x, g):
    return x * lax.rsqrt(jnp.mean(x * x, axis=-1, keepdims=True) + EPS) * g


def _split3(x):
    hi = x.astype(BF16)
    r = x - hi.astype(F32)
    mid = r.astype(BF16)
    lo = (r - mid.astype(F32)).astype(BF16)
    return hi, mid, lo


def _dot01(x, onehot):
    hi, mid, lo = _split3(x)
    d = lambda a: jnp.dot(a, onehot, preferred_element_type=F32)
    return d(hi) + d(mid) + d(lo)


def _dot01_left(onehot, x):
    hi, mid, lo = _split3(x)
    d = lambda a: jnp.dot(onehot, a, preferred_element_type=F32)
    return d(hi) + d(mid) + d(lo)


def _log_sigmoid(x):
    return jnp.minimum(x, 0.0) - jnp.log1p(jnp.exp(-jnp.abs(x)))


def _sigmoid(x):
    return 1.0 / (1.0 + jnp.exp(-x))


def _norm_mm_kernel(x_ref, *refs, splits):
    n_proj = len(splits)
    out_refs = iter(refs[2 * n_proj:])
    x = x_ref[...]
    xhat = x * lax.rsqrt(jnp.mean(x * x, axis=-1, keepdims=True) + EPS)
    ys = [(xhat * refs[2 * p][...]).astype(BF16) for p in range(n_proj)]
    rs = [jnp.dot(ys[p], refs[2 * p + 1][...], preferred_element_type=F32) for p in range(n_proj)]
    for p in range(n_proj):
        for off, n in splits[p]:
            o_ref = next(out_refs)
            o_ref[...] = rs[p][:, off:off + n].astype(o_ref.dtype)


def _layer_spec(arr, layer):
    idx = (layer,) + (0,) * (arr.ndim - 1)
    return pl.BlockSpec((None,) + arr.shape[1:], lambda *_: idx)


def norm_matmul(x, projections, tm):
    m, d = x.shape
    assert m % tm == 0
    params, param_specs, out_specs, out_shapes = [], [], [], []
    for g, lg, w, lw, splits, dtypes in projections:
        assert all(off + n <= w.shape[-1] for off, n in splits)
        params += [g, w]
        param_specs += [_layer_spec(g, lg), _layer_spec(w, lw)]
        out_specs += [pl.BlockSpec((tm, n), lambda i: (i, 0)) for _, n in splits]
        out_shapes += [jax.ShapeDtypeStruct((m, n), dt) for (_, n), dt in zip(splits, dtypes)]
    outs = pl.pallas_call(
        functools.partial(_norm_mm_kernel, splits=tuple(p[4] for p in projections)),
        grid=(m // tm,),
        in_specs=[pl.BlockSpec((tm, d), lambda i: (i, 0))] + param_specs,
        out_specs=out_specs,
        out_shape=out_shapes,
        compiler_params=_cparams("parallel"),
        name="norm_matmul",
    )(x, *params)
    outs, grouped = list(outs), []
    for p in projections:
        grouped.append(outs[:len(p[4])])
        outs = outs[len(p[4]):]
    return grouped


def _mlp_kernel(a_ref, wo_ref, x_ref, g_ref, wu_ref, wd_ref, gf_ref, o_ref, xn_ref, *,
                final_norm, row_chunk):
    j = pl.program_id(1)
    tm = x_ref.shape[0]

    @pl.when(j == 0)
    def _():
        for r in range(0, tm, row_chunk):
            rows = slice(r, r + row_chunk)
            x = x_ref[rows, :] + jnp.dot(a_ref[rows, :].astype(BF16), wo_ref[...],
                                         preferred_element_type=F32)
            xn_ref[rows, :] = _rms(x, g_ref[...]).astype(BF16)
            o_ref[rows, :] = x

    h = jnp.dot(xn_ref[...], wu_ref[...], preferred_element_type=F32)
    h = jnp.maximum(h, 0.0)
    o_ref[...] += jnp.dot((h * h).astype(BF16), wd_ref[...], preferred_element_type=F32)

    if final_norm:
        @pl.when(j == pl.num_programs(1) - 1)
        def _():
            o_ref[...] = _rms(o_ref[...], gf_ref[...])


def mlp(a, w_o, lo, x, g, w_up, w_down, l, g_final, final_norm, tm, tf):
    m, d = x.shape
    ff = w_up.shape[-1]
    return pl.pallas_call(
        functools.partial(_mlp_kernel, final_norm=final_norm, row_chunk=min(tm, MLP_ROW_CHUNK)),
        grid=(m // tm, ff // tf),
        in_specs=[pl.BlockSpec((tm, d), lambda i, j: (i, 0)),
                  _layer_spec(w_o, lo),
                  pl.BlockSpec((tm, d), lambda i, j: (i, 0)),
                  _layer_spec(g, l),
                  pl.BlockSpec((None, d, tf), lambda i, j: (l, 0, j)),
                  pl.BlockSpec((None, tf, d), lambda i, j: (l, j, 0)),
                  pl.BlockSpec((1, d), lambda i, j: (0, 0))],
        out_specs=pl.BlockSpec((tm, d), lambda i, j: (i, 0)),
        out_shape=jax.ShapeDtypeStruct((m, d), F32),
        scratch_shapes=[pltpu.VMEM((tm, d), BF16)],
        compiler_params=_cparams("parallel", "arbitrary"),
        name="mlp",
    )(a, w_o, x, g, w_up, w_down, g_final.reshape(1, d))


def _mlstm_inproj_kernel(x_ref, g_ref, w_ref, q_ref, k_ref, v_ref, o_ref, gate_ref, *, k_transposed):
    hq, hv = N_HEADS * DQK, N_HEADS * DV
    tm = x_ref.shape[0]
    y = _rms(x_ref[...], g_ref[...]).astype(BF16)
    r = jnp.dot(y, w_ref[...], preferred_element_type=F32)
    q_ref[...] = r[:, :hq].astype(q_ref.dtype)
    k = r[:, hq:2 * hq] * DQK ** -0.5
    k_ref[...] = (k.T if k_transposed else k).astype(k_ref.dtype)
    v_ref[...] = r[:, 2 * hq:2 * hq + hv].astype(v_ref.dtype)
    o_ref[...] = r[:, 2 * hq + hv:2 * hq + 2 * hv]
    gate_ref[...] = jnp.concatenate(
        [r[:, 2 * hq + 2 * hv:], jnp.zeros((tm, LANES - 2 * N_HEADS), F32)], axis=1)


def mlstm_inproj(x, g, w, l, tm, k_transposed, act):
    m, d = x.shape
    hq, hv = N_HEADS * DQK, N_HEADS * DV
    assert w.shape[-1] == 2 * hq + 2 * hv + 2 * N_HEADS
    row = lambda width: pl.BlockSpec((tm, width), lambda i: (i, 0))
    k_spec = pl.BlockSpec((hq, tm), lambda i: (0, i)) if k_transposed else row(hq)
    k_shape = (hq, m) if k_transposed else (m, hq)
    return pl.pallas_call(
        functools.partial(_mlstm_inproj_kernel, k_transposed=k_transposed),
        grid=(m // tm,),
        in_specs=[row(d), _layer_spec(g, l), _layer_spec(w, l)],
        out_specs=[row(hq), k_spec, row(hv), row(hv), row(LANES)],
        out_shape=[jax.ShapeDtypeStruct((m, hq), act), jax.ShapeDtypeStruct(k_shape, act),
                   jax.ShapeDtypeStruct((m, hv), act), jax.ShapeDtypeStruct((m, hv), F32),
                   jax.ShapeDtypeStruct((m, LANES), F32)],
        compiler_params=_cparams("parallel"),
        name="mlstm_inproj",
    )(x, g, w)


def _chunk_scan(x, pos, op, fill, length):
    k = 1
    while k < length:
        x = op(x, jnp.where(pos >= k, pltpu.roll(x, k, 0), fill))
        k *= 2
    return x


def _mlstm_prompt_kernel(q_ref, kt_ref, v_ref, o_ref, gate_ref, bi_ref, bf_ref, gain_ref,
                         hg_ref, cx_out_ref, m_out_ref, cx_s, m_s, *, chunk):
    t_blk = q_ref.shape[1]
    n_chunks = t_blk // chunk
    step = pl.program_id(1)

    @pl.when(step == 0)
    def _():
        cx_s[...] = jnp.zeros_like(cx_s)
        m_s[...] = jnp.zeros_like(m_s)

    gates = gate_ref[0]
    li = gates + bi_ref[...]
    lf = _log_sigmoid(pltpu.roll(gates, LANES - N_HEADS, 1) + bf_ref[...])
    pos = lax.broadcasted_iota(jnp.int32, (t_blk, 1), 0) % chunk
    b = _chunk_scan(lf, pos, jnp.add, 0.0, chunk)
    c = li - b
    cm = _chunk_scan(c, pos, jnp.maximum, NEG_INF, chunk)

    m_prev = m_s[...]
    xs, w_inters, e_negms, wks = [], [], [], []
    for ck in range(n_chunks):
        sl = slice(ck * chunk, (ck + 1) * chunk)
        m_t = b[sl] + jnp.maximum(m_prev, cm[sl])
        m_new = m_t[chunk - 1:chunk]
        b_last = b[(ck + 1) * chunk - 1:(ck + 1) * chunk]
        xs.append(b[sl] - m_t)
        w_inters.append(jnp.exp(b[sl] + m_prev - m_t))
        e_negms.append(jnp.exp(-m_t))
        wks.append(jnp.exp(c[sl] + (b_last - m_new)))
        m_prev = m_new
    m_s[...] = m_prev
    c_t = c.T
    wk_t = jnp.concatenate(wks, axis=0).T

    ri = lax.broadcasted_iota(jnp.int32, (chunk, chunk), 0)
    ci = lax.broadcasted_iota(jnp.int32, (chunk, chunk), 1)
    causal = ci <= ri
    ones_blk = jnp.ones((chunk, DV), BF16)
    ones_sq = jnp.ones((DV, DV), BF16)
    heads = range(N_HEADS)
    dot = functools.partial(jnp.dot, preferred_element_type=F32)

    cxs = [cx_s[h] for h in heads]
    for ck in range(n_chunks):
        sl = slice(ck * chunk, (ck + 1) * chunk)
        qs = [q_ref[0, sl, h * DQK:(h + 1) * DQK] for h in heads]
        kts = [kt_ref[h * DQK:(h + 1) * DQK, sl] for h in heads]
        vs = [v_ref[0, sl, h * DV:(h + 1) * DV] for h in heads]
        cx16s = [cxs[h].astype(BF16) for h in heads]
        kws = [(kts[h].astype(F32) * wk_t[h:h + 1, sl]).astype(BF16) for h in heads]
        vones = [jnp.concatenate([vs[h], ones_blk], axis=1) for h in heads]
        ss = [dot(qs[h], kts[h]) for h in heads]
        qcs = [dot(qs[h], cx16s[h]) for h in heads]
        dcs = [dot(kws[h], vones[h]) for h in heads]
        cxs = [w_inters[ck][chunk - 1:chunk, h:h + 1] * cxs[h] + dcs[h] for h in heads]
        a16s = []
        for h in heads:
            dm = xs[ck][:, h:h + 1] + c_t[h:h + 1, sl]
            a16s.append((ss[h] * jnp.exp(jnp.where(causal, dm, NEG_INF))).astype(BF16))
        avs = [dot(a16s[h], vones[h]) for h in heads]
        hhs = []
        for h in heads:
            wi = jnp.broadcast_to(w_inters[ck][:, h:h + 1], (chunk, DV))
            en = jnp.broadcast_to(e_negms[ck][:, h:h + 1], (chunk, DV))
            den = avs[h][:, DV:] + wi * qcs[h][:, DV:]
            inv = 1.0 / jnp.maximum(jnp.abs(den), en)
            hhs.append((avs[h][:, :DV] + qcs[h][:, :DV] * wi) * inv)
        sq16s = [(hhs[h] * hhs[h]).astype(BF16) for h in heads]
        mss = [dot(sq16s[h], ones_sq) for h in heads]
        for h in heads:
            hn = hhs[h] * lax.rsqrt(mss[h] * (1.0 / DV) + EPS)
            hn = hn * gain_ref[:, h * DV:(h + 1) * DV]
            og = _sigmoid(o_ref[0, sl, h * DV:(h + 1) * DV])
            hg_ref[0, sl, h * DV:(h + 1) * DV] = (og * hn).astype(hg_ref.dtype)
    for h in heads:
        cx_s[h] = cxs[h]

    @pl.when(step == pl.num_programs(1) - 1)
    def _():
        cx_out_ref[0] = cx_s[...]
        m_out_ref[0] = m_s[...]


def mlstm_prompt(q, kt, v, o, gate, b_i, b_f, gain, batch, seq):
    hq, hv = N_HEADS * DQK, N_HEADS * DV
    t = PROMPT_BLOCK
    nblk = seq // t
    r3 = lambda a: a.reshape(batch, seq, a.shape[-1])
    pad8 = lambda a: jnp.pad(a.reshape(1, N_HEADS), ((0, 0), (0, LANES - N_HEADS)))
    tok = lambda w: pl.BlockSpec((1, t, w), lambda bb, s: (bb, s, 0))
    cst = lambda w: pl.BlockSpec((1, w), lambda bb, s: (0, 0))
    hg, cx, m_new = pl.pallas_call(
        functools.partial(_mlstm_prompt_kernel, chunk=PROMPT_CHUNK),
        grid=(batch, nblk),
        in_specs=[tok(hq), pl.BlockSpec((hq, t), lambda bb, s: (0, bb * nblk + s)),
                  tok(hv), tok(hv), tok(LANES), cst(LANES), cst(LANES), cst(hv)],
        out_specs=[tok(hv),
                   pl.BlockSpec((1, N_HEADS, DQK, 2 * DV), lambda bb, s: (bb, 0, 0, 0)),
                   pl.BlockSpec((1, 1, LANES), lambda bb, s: (bb, 0, 0))],
        out_shape=[jax.ShapeDtypeStruct((batch, seq, hv), BF16),
                   jax.ShapeDtypeStruct((batch, N_HEADS, DQK, 2 * DV), F32),
                   jax.ShapeDtypeStruct((batch, 1, LANES), F32)],
        scratch_shapes=[pltpu.VMEM((N_HEADS, DQK, 2 * DV), F32), pltpu.VMEM((1, LANES), F32)],
        compiler_params=_cparams("parallel", "arbitrary"),
        name="mlstm_prompt",
    )(r3(q), kt, r3(v), r3(o), r3(gate), pad8(b_i), pad8(b_f), gain.reshape(1, hv))
    return hg.reshape(batch * seq, hv), cx[..., :DV], cx[..., DV], m_new[:, 0, :N_HEADS]


def _mlstm_sample_kernel(q_ref, k_ref, v_ref, o_ref, gate_ref, m0_ref, n0_ref, c0_ref, c_carry_ref,
                         bi_ref, bf_ref, gain_ref, seg64_ref, seg128_ref, e64_ref, e128_ref,
                         hg_ref, c_out_ref, n_out_ref, m_out_ref,
                         qc_s, *, seq):
    del c_carry_ref
    spans_layers = len(c_out_ref.shape) == 5
    c_new_ref = c_out_ref.at[0] if spans_layers else c_out_ref
    if spans_layers:
        for later in range(1, c_out_ref.shape[0]):
            c_out_ref[later] = jnp.zeros(c_out_ref.shape[1:], F32)
    rows = q_ref.shape[0]
    n_b = rows // seq
    tpos = lax.broadcasted_iota(jnp.int32, (rows, 1), 0) % seq
    dot = functools.partial(jnp.dot, preferred_element_type=F32)

    def shift(x, d):
        return x if d == 0 else pltpu.roll(x, d, 0)

    def unshift(x, d):
        return x if d == 0 else pltpu.roll(x, rows - d, 0)

    def split_cat(x):
        return jnp.concatenate(_split3(x), axis=1)


    q = q_ref[...].astype(BF16).astype(F32)
    k = k_ref[...].astype(BF16).astype(F32)
    v = v_ref[...].astype(BF16).astype(F32)

    units = [(p, h) for p in range(n_b // 2) for h in range(N_HEADS)]
    prow = lambda p: slice(p * 2 * seq, (p + 1) * 2 * seq)
    q_pairs = [q[prow(p), h * DQK:(h + 1) * DQK].astype(BF16) for p, h in units]
    c_cats = [jnp.concatenate([c0_ref[2 * p, h], c0_ref[2 * p + 1, h]], axis=1).astype(BF16)
              for p, h in units]
    qcs = [dot(q_pairs[i], c_cats[i]) for i in range(len(units))]

    gates = gate_ref[...]
    li = gates + bi_ref[...]
    lf = _log_sigmoid(pltpu.roll(gates, LANES - N_HEADS, 1) + bf_ref[...])
    b = lf
    for d in range(1, seq):
        b = b + jnp.where(tpos >= d, shift(lf, d), 0.0)
    inter = b + m0_ref[...]
    dvals = []
    m_t = inter
    for d in range(seq):
        dd = jnp.where(tpos >= d, b - shift(b, d) + shift(li, d), NEG_INF)
        dvals.append(dd)
        m_t = jnp.maximum(m_t, dd)
    w_inter = jnp.exp(inter - m_t)
    w_intra = [jnp.exp(dd - m_t) for dd in dvals]
    last = lambda x: functools.reduce(
        lambda acc, d: jnp.where(tpos == seq - 1 - d, unshift(x, d), acc), range(1, seq), x)
    wk = jnp.exp(last(b) - b + li - last(m_t))

    qk_in = [split_cat(q * shift(k, d)) for d in range(seq)]
    qn_in = split_cat(q * n0_ref[...].astype(BF16).astype(F32))
    wk_in, wi_in = split_cat(wk), split_cat(w_inter)
    seg64, e64, e128 = seg64_ref[...], e64_ref[...], e128_ref[...]
    qks = [dot(x, seg64) for x in qk_in]
    qn = dot(qn_in, seg64)
    wk_exp = dot(wk_in, e64)
    decay64 = dot(wi_in, e64)
    decay_exp = dot(wi_in, e128)

    a_s = [qks[d] * w_intra[d] for d in range(seq)]
    e128_1 = e128[:LANES]
    a_exps = [dot(a.astype(BF16), e128_1) for a in a_s]
    den = w_inter * qn
    num = jnp.zeros((rows, N_HEADS * DV), F32)
    for d in range(seq):
        den = den + a_s[d]
        num = num + a_exps[d] * shift(v, d)

    kw = k * wk_exp
    ksum = kw
    for d in range(1, seq):
        ksum = ksum + shift(kw, d)
    n_rows = decay64 * n0_ref[...] + ksum
    own = ((lax.broadcasted_iota(jnp.int32, (2 * seq, 2 * DV), 0) < seq)
           == (lax.broadcasted_iota(jnp.int32, (2 * seq, 2 * DV), 1) < DV))
    kw_pairs = [kw[prow(p), h * DQK:(h + 1) * DQK].astype(BF16) for p, h in units]
    v2s = []
    for p, h in units:
        v_pair = v[prow(p), h * DV:(h + 1) * DV]
        v2s.append(jnp.where(own, jnp.concatenate([v_pair, v_pair], axis=1), 0.0).astype(BF16))
    sel3 = (lax.broadcasted_iota(jnp.int32, (n_b, 3 * rows), 1) % rows
            == lax.broadcasted_iota(jnp.int32, (n_b, 3 * rows), 0) * seq + (seq - 1))
    sel3 = jnp.where(sel3, 1.0, 0.0).astype(BF16)
    n_in = jnp.concatenate(_split3(n_rows), axis=0)
    m_in = jnp.concatenate(_split3(m_t), axis=0)
    d_cs = [lax.dot_general(kw_pairs[i], v2s[i], (((0,), (0,)), ((), ())),
                            preferred_element_type=F32) for i in range(len(units))]
    n_out_ref[...] = dot(sel3, n_in)
    m_out_ref[...] = dot(sel3, m_in)

    first = lax.broadcasted_iota(jnp.int32, (2 * seq, DV), 0) < seq
    for i, (p, h) in enumerate(units):
        r0 = p * 2 * seq
        qc_s[prow(p), h * DV:(h + 1) * DV] = jnp.where(first, qcs[i][:, :DV], qcs[i][:, DV:])
        dec_a = decay_exp[r0 + seq - 1:r0 + seq, h * DV:(h + 1) * DV]
        dec_b = decay_exp[r0 + 2 * seq - 1:r0 + 2 * seq, h * DV:(h + 1) * DV]
        c_new_ref[2 * p, h] = dec_a * c0_ref[2 * p, h] + d_cs[i][:, :DV]
        c_new_ref[2 * p + 1, h] = dec_b * c0_ref[2 * p + 1, h] + d_cs[i][:, DV:]

    num = num + qc_s[...] * decay_exp
    inv_scale = 1.0 / jnp.maximum(jnp.abs(den), jnp.exp(-m_t))
    hh = num * dot(split_cat(inv_scale), e128)
    ms = dot(split_cat(hh * hh), seg128_ref[...]) * (1.0 / DV)
    hn = hh * dot(split_cat(lax.rsqrt(ms + EPS)), e128) * gain_ref[...]
    hg_ref[...] = (_sigmoid(o_ref[...]) * hn).astype(hg_ref.dtype)


def mlstm_sample(q, k, v, o, gate, b_i, b_f, gain, state_c, c_carry, l, n0, m0, batch, seq):
    hq, hv = N_HEADS * DQK, N_HEADS * DV
    c_blk = pl.BlockSpec((None, SAMPLE_BATCH_BLOCK, N_HEADS, DQK, DV), lambda i: (l, i, 0, 0, 0))
    c_all = pl.BlockSpec((state_c.shape[0], SAMPLE_BATCH_BLOCK, N_HEADS, DQK, DV),
                         lambda i: (0, i, 0, 0, 0))
    rows = SAMPLE_BATCH_BLOCK * seq
    pad8 = lambda a: jnp.pad(a.reshape(1, N_HEADS), ((0, 0), (0, LANES - N_HEADS)))
    m0_rows = jnp.pad(jnp.repeat(m0, seq, axis=0), ((0, 0), (0, LANES - N_HEADS)))
    n0_rows = jnp.repeat(n0.reshape(batch, hq), seq, axis=0)
    lane = jnp.arange(LANES)
    seg64 = (jnp.arange(hq)[:, None] // DQK == lane[None, :]).astype(BF16)
    seg128 = (jnp.arange(hv)[:, None] // DV == lane[None, :]).astype(BF16)
    tok = lambda w: pl.BlockSpec((rows, w), lambda i: (i, 0))
    cst = lambda a: pl.BlockSpec(a.shape, lambda i: (0,) * a.ndim)
    x3 = lambda a: jnp.tile(a, (3, 1))
    consts = [pad8(b_i), pad8(b_f), gain.reshape(1, hv),
              x3(seg64), x3(seg128), x3(seg64.T), x3(seg128.T)]
    hg, c_new, n_new, m_new = pl.pallas_call(
        functools.partial(_mlstm_sample_kernel, seq=seq),
        grid=(batch // SAMPLE_BATCH_BLOCK,),
        in_specs=[tok(hq), tok(hq), tok(hv), tok(hv), tok(LANES), tok(LANES), tok(hq),
                  c_blk, pl.BlockSpec(memory_space=pl.ANY)]
                 + [cst(a) for a in consts],
        out_specs=[tok(hv), c_all if l == 0 else c_blk,
                   pl.BlockSpec((SAMPLE_BATCH_BLOCK, hq), lambda i: (i, 0)),
                   pl.BlockSpec((SAMPLE_BATCH_BLOCK, LANES), lambda i: (i, 0))],
        out_shape=[jax.ShapeDtypeStruct((batch * seq, hv), BF16),
                   jax.ShapeDtypeStruct(state_c.shape, F32),
                   jax.ShapeDtypeStruct((batch, hq), F32),
                   jax.ShapeDtypeStruct((batch, LANES), F32)],
        scratch_shapes=[pltpu.VMEM((rows, hv), F32)],
        input_output_aliases={8: 1} if l > 0 else {},
        compiler_params=_cparams("parallel"),
        name="mlstm_sample",
    )(q, k, v, o, gate, m0_rows, n0_rows, state_c, c_carry, *consts)
    return hg, c_new, n_new.reshape(batch, N_HEADS, DQK), m_new[:, :N_HEADS]


def _bias_table_kernel(rb_ref, sink_ref, o_ref, *, sink_col):
    n_layers, _, nq, ns = o_ref.shape
    col = lax.broadcasted_iota(jnp.int32, (nq, ns), 1)
    dist = lax.broadcasted_iota(jnp.int32, (nq, ns), 0) + WINDOW - col
    n = jnp.maximum(dist, 0)
    large = MAX_EXACT + (jnp.log(jnp.maximum(n, 1).astype(F32) / MAX_EXACT)
                         / math.log(MAX_DISTANCE / MAX_EXACT) * (N_BUCKETS - MAX_EXACT)).astype(jnp.int32)
    large = jnp.minimum(large, N_BUCKETS - 1)
    bucket = jnp.where(n < MAX_EXACT, n, large)
    valid = (dist >= 0) & (dist < WINDOW)
    for h in range(Q_HEADS):
        acc = jnp.zeros((nq, ns), F32)
        for bkt in range(N_BUCKETS):
            acc = jnp.where(bucket == bkt, rb_ref[bkt, h], acc)
        acc = jnp.where(valid, acc, NEG_INF)
        for layer in range(n_layers):
            o_ref[layer, h] = jnp.where(col == sink_col, sink_ref[layer, h], acc)


def bias_table(rel_bias, sinks, nq, ns, sink_col):
    return pl.pallas_call(
        functools.partial(_bias_table_kernel, sink_col=sink_col),
        in_specs=[pl.BlockSpec(memory_space=pltpu.SMEM), pl.BlockSpec(memory_space=pltpu.SMEM)],
        out_specs=pl.BlockSpec(memory_space=pltpu.VMEM),
        out_shape=jax.ShapeDtypeStruct((sinks.shape[0], Q_HEADS, nq, ns), F32),
        name="bias_table",
    )(rel_bias, sinks)


PROMPT_SINK_KEY = 0


def _attn_prompt_kernel(q_ref, kp_ref, kc_ref, vp_ref, vc_ref, bias_ref, o_ref):
    dk = KV_HEADS * HEAD_DIM
    kvh = range(KV_HEADS)
    sink_row = lax.broadcasted_iota(jnp.int32, (2 * WINDOW, dk), 0) == PROMPT_SINK_KEY
    lane_head = lax.broadcasted_iota(jnp.int32, (2 * WINDOW, dk), 1) // HEAD_DIM
    out_head = lax.broadcasted_iota(jnp.int32, (GROUP * WINDOW, dk), 1) // HEAD_DIM
    zero = jnp.zeros((), BF16)
    first_of_sequence = pl.program_id(1) == 0
    for j in range(q_ref.shape[1] // WINDOW):
        rows = slice(j * WINDOW, (j + 1) * WINDOW)
        before = slice((j - 1) * WINDOW, j * WINDOW)
        k_prev, v_prev = (kp_ref[0], vp_ref[0]) if j == 0 else (kc_ref[0, before], vc_ref[0, before])
        variant = jnp.where(first_of_sequence, 0, 1) if j == 0 else 1
        k_all = jnp.where(sink_row, zero, jnp.concatenate([k_prev, kc_ref[0, rows]], axis=0))
        v_all = jnp.where(sink_row, zero, jnp.concatenate([v_prev, vc_ref[0, rows]], axis=0))
        k_bd = jnp.concatenate([jnp.where(lane_head == kh, k_all, zero) for kh in kvh], axis=0)
        v_bd = jnp.concatenate([jnp.where(lane_head == kh, v_all, zero) for kh in kvh], axis=0)
        q_cat = jnp.concatenate([q_ref[0, rows, g * dk:(g + 1) * dk] for g in range(GROUP)],
                                axis=0)
        s = lax.dot_general(q_cat, k_bd, (((1,), (1,)), ((), ())), preferred_element_type=F32)
        es, invs = [], []
        for kh in kvh:
            sk = s[:, kh * 2 * WINDOW:(kh + 1) * 2 * WINDOW] + bias_ref[variant, kh]
            e = jnp.exp(sk - jnp.max(sk, axis=1, keepdims=True))
            invs.append(1.0 / jnp.sum(e, axis=1, keepdims=True))
            es.append(e.astype(BF16))
        pv = jnp.dot(jnp.concatenate(es, axis=1), v_bd, preferred_element_type=F32)
        inv = invs[KV_HEADS - 1]
        for kh in range(KV_HEADS - 1):
            inv = jnp.where(out_head == kh, invs[kh], inv)
        o = pv * inv
        for g in range(GROUP):
            o_ref[0, rows, g * dk:(g + 1) * dk] = o[g * WINDOW:(g + 1) * WINDOW].astype(o_ref.dtype)


def attn_prompt(q, k, v, table, batch, seq):
    d = Q_HEADS * HEAD_DIM
    dk = KV_HEADS * HEAD_DIM
    nq = ATTN_PROMPT_BLOCKS
    r3 = lambda a: a.reshape(batch, seq, a.shape[-1])
    prev = pl.BlockSpec((1, WINDOW, dk), lambda b, i: (b, jnp.maximum(i * nq - 1, 0), 0))
    cur = pl.BlockSpec((1, nq * WINDOW, dk), lambda b, i: (b, i, 0))
    table = table.reshape(KV_HEADS, GROUP * WINDOW, 2 * WINDOW)
    key = jnp.arange(2 * WINDOW)
    no_prev = jnp.where((key < WINDOW) & (key != PROMPT_SINK_KEY), NEG_INF, table)
    tables = jnp.stack([no_prev, table])
    out = pl.pallas_call(
        _attn_prompt_kernel,
        grid=(batch, seq // (nq * WINDOW)),
        in_specs=[pl.BlockSpec((1, nq * WINDOW, d), lambda b, i: (b, i, 0)),
                  prev, cur, prev, cur,
                  pl.BlockSpec(tables.shape, lambda b, i: (0, 0, 0, 0))],
        out_specs=pl.BlockSpec((1, nq * WINDOW, d), lambda b, i: (b, i, 0)),
        out_shape=jax.ShapeDtypeStruct((batch, seq, d), BF16),
        compiler_params=_cparams("parallel", "arbitrary"),
        name="attn_prompt",
    )(r3(q), r3(k), r3(k), r3(v), r3(v), tables)
    return out.reshape(batch * seq, d)


def _attn_sample_kernel(q_ref, kc_ref, kn_ref, vc_ref, vn_ref, bias_ref, o_ref,
                        *cache_out_refs, seq):
    n_b = q_ref.shape[0]
    dk = KV_HEADS * HEAD_DIM
    pad = jnp.zeros((WINDOW - seq, dk), F32)
    pairs = [(b, kh) for b in range(n_b) for kh in range(KV_HEADS)]
    kslice = lambda kh: slice(kh * HEAD_DIM, (kh + 1) * HEAD_DIM)
    head_lanes = lambda kh, g: slice((g * KV_HEADS + kh) * HEAD_DIM, (g * KV_HEADS + kh + 1) * HEAD_DIM)
    kks = [jnp.concatenate([kc_ref[b], kn_ref[b], pad], axis=0).astype(BF16) for b in range(n_b)]
    vvs = [jnp.concatenate([vc_ref[b], vn_ref[b], pad], axis=0).astype(BF16) for b in range(n_b)]
    ss = []
    for b, kh in pairs:
        qs = jnp.concatenate([q_ref[b, :, head_lanes(kh, g)] for g in range(GROUP)],
                             axis=0).astype(BF16)
        ss.append(lax.dot_general(qs, kks[b][:, kslice(kh)], (((1,), (1,)), ((), ())),
                                  preferred_element_type=F32))
    es, denoms = [], []
    for i, (b, kh) in enumerate(pairs):
        s = ss[i] + bias_ref[kh]
        e = jnp.exp(s - jnp.max(s, axis=1, keepdims=True))
        denoms.append(jnp.sum(e, axis=1, keepdims=True))
        es.append(e.astype(BF16))
    pvs = [jnp.dot(es[i], vvs[b][:, kslice(kh)], preferred_element_type=F32)
           for i, (b, kh) in enumerate(pairs)]
    for i, (b, kh) in enumerate(pairs):
        o = pvs[i] / denoms[i]
        for g in range(GROUP):
            o_ref[b, :, head_lanes(kh, g)] = o[g * seq:(g + 1) * seq].astype(o_ref.dtype)
    for out_ref, old_ref, new_ref in zip(cache_out_refs, (kc_ref, vc_ref), (kn_ref, vn_ref)):
        for b in range(n_b):
            out_ref[b, :WINDOW - seq, :] = old_ref[b, seq:, :]
            out_ref[b, WINDOW - seq:, :] = new_ref[b]


def attn_sample(q, k_new, v_new, cache_k, cache_v, table, batch, seq, write_cache):
    d = Q_HEADS * HEAD_DIM
    dk = KV_HEADS * HEAD_DIM
    nbk = ATTN_SAMPLE_BLOCK
    new = pl.BlockSpec((nbk, seq, dk), lambda i: (i, 0, 0))
    old = pl.BlockSpec((nbk, WINDOW, dk), lambda i: (i, 0, 0))
    n_cache = 2 if write_cache else 0
    out = pl.pallas_call(
        functools.partial(_attn_sample_kernel, seq=seq),
        grid=(batch // nbk,),
        in_specs=[pl.BlockSpec((nbk, seq, d), lambda i: (i, 0, 0)),
                  old, new, old, new,
                  pl.BlockSpec((KV_HEADS, GROUP * seq, 2 * WINDOW), lambda i: (0, 0, 0))],
        out_specs=[pl.BlockSpec((nbk, seq, d), lambda i: (i, 0, 0))] + [old] * n_cache,
        out_shape=[jax.ShapeDtypeStruct((batch, seq, d), F32)]
                  + [jax.ShapeDtypeStruct((batch, WINDOW, dk), F32)] * n_cache,
        compiler_params=_cparams("parallel"),
        name="attn_sample",
    )(q.reshape(batch, seq, d), cache_k.reshape(batch, WINDOW, dk),
      k_new.reshape(batch, seq, dk), cache_v.reshape(batch, WINDOW, dk),
      v_new.reshape(batch, seq, dk), table.reshape(KV_HEADS, GROUP * seq, 2 * WINDOW))
    return (out[0].reshape(batch * seq, d),) + tuple(out[1:])


def _trunk(x, state, cache, w, tm, tm_mlp):
    batch, seq, d = x.shape
    x = x.reshape(batch * seq, d)
    hq, hv = N_HEADS * DQK, N_HEADS * DV
    depth = w["norm_mix"].shape[0]
    n_a = w["w_in"].shape[0]
    prompt = state is None
    act = BF16 if prompt else F32
    cs, ns, ms = [], [], []
    c_stack = None if prompt else state[0]
    for l in range(depth):
        if l < n_a:
            gates = (w["b_igate"][l], w["b_fgate"][l], w["mlstm_norm"][l])
            q, k, v, o, gate = mlstm_inproj(x, w["norm_mix"], w["w_in"], l, tm, prompt, act)
            if prompt:
                a, c_new, n_new, m_new = mlstm_prompt(q, k, v, o, gate, *gates, batch, seq)
                cs.append(c_new)
            else:
                a, c_stack, n_new, m_new = mlstm_sample(q, k, v, o, gate, *gates, state[0], c_stack,
                                                        l, state[1][l], state[2][l], batch, seq)
            ns.append(n_new); ms.append(m_new)
            w_o, lo = w["w_mlstm_out"], l
        else:
            j = l - n_a
            q_proj = (w["norm_mix"], l, w["w_q"], j, ((0, Q_HEADS * HEAD_DIM),), (act,))
            if j == 0:
                dk = KV_HEADS * HEAD_DIM
                kv_proj = (w["kv_norm"], 0, w["w_kv"], 0, ((0, dk), (dk, dk), (0, dk), (dk, dk)),
                           (F32, F32, BF16, BF16))
                (k_new, v_new, k16, v16), (q,) = norm_matmul(x, (kv_proj, q_proj), tm)
                tables = bias_table(w["rel_bias"], w["attn_sinks"], WINDOW if prompt else seq,
                                    2 * WINDOW, PROMPT_SINK_KEY if prompt else 2 * WINDOW - 1)
            else:
                ((q,),) = norm_matmul(x, (q_proj,), tm)
            if prompt:
                a = attn_prompt(q, k16, v16, tables[j], batch, seq)
            elif j == 0:
                a, win_k, win_v = attn_sample(q, k_new, v_new, cache[0], cache[1], tables[j],
                                              batch, seq, True)
            else:
                (a,) = attn_sample(q, k_new, v_new, cache[0], cache[1], tables[j],
                                   batch, seq, False)
            w_o, lo = w["w_attn_out"], j
        x = mlp(a, w_o, lo, x, w["norm_ffn"], w["w_up"], w["w_down"], l, w["final_norm"],
                l == depth - 1, tm_mlp, MLP_FF_BLOCK)
    if prompt:
        dk = KV_HEADS * HEAD_DIM
        win_k = k_new.reshape(batch, seq, dk)[:, -WINDOW:]
        win_v = v_new.reshape(batch, seq, dk)[:, -WINDOW:]
        c_stack = jnp.stack(cs)
    shp = (batch, WINDOW, KV_HEADS, HEAD_DIM)
    return (x.reshape(batch, seq, d), c_stack, jnp.stack(ns), jnp.stack(ms),
            win_k.reshape(shp), win_v.reshape(shp))


def kernel(x_prompt, x_sample, state_C, state_n, state_m, cache_k, cache_v, norm_mix, norm_ffn,
           w_mlstm_in, b_igate, b_fgate, mlstm_norm, w_mlstm_out, kv_norm, w_kv, w_q, attn_sinks,
           w_attn_out, rel_bias, w_up, w_down, final_norm):
    n_b, d = w_q.shape[0], w_q.shape[1]
    heads = (KV_HEADS, GROUP, HEAD_DIM)
    w_q_perm = (w_q * HEAD_DIM ** -0.5).astype(BF16).reshape((n_b, d) + heads)
    w_q_perm = w_q_perm.transpose(0, 1, 3, 2, 4).reshape(n_b, d, d)
    w_ao_perm = w_attn_out.astype(BF16).reshape((n_b,) + heads + (d,))
    w_ao_perm = w_ao_perm.transpose(0, 2, 1, 3, 4).reshape(n_b, d, d)
    w = dict(norm_mix=norm_mix[:, None, :], norm_ffn=norm_ffn[:, None, :],
             w_in=w_mlstm_in.astype(BF16), b_igate=b_igate, b_fgate=b_fgate,
             mlstm_norm=mlstm_norm, w_mlstm_out=w_mlstm_out.astype(BF16),
             kv_norm=kv_norm[None, None, :], w_kv=w_kv.astype(BF16)[None],
             w_q=w_q_perm, attn_sinks=attn_sinks, w_attn_out=w_ao_perm, rel_bias=rel_bias,
             w_up=w_up.astype(BF16), w_down=w_down.astype(BF16), final_norm=final_norm)
    y_p, c_p, n_p, m_p, k_p, v_p = _trunk(x_prompt, None, None, w, 512, 1024)
    y_s, c_s, n_s, m_s, k_s, v_s = _trunk(x_sample, (state_C, state_n, state_m),
                                          (cache_k, cache_v), w, 512, 512)
    return (y_p, y_s, c_p, n_p, m_p, k_p, v_p, c_s, n_s, m_s, k_s, v_s)
```

```python
import functools
import math

import jax
import jax.numpy as jnp
from jax import lax
from jax.experimental import pallas as pl
from jax.experimental.pallas import tpu as pltpu

F32 = jnp.float32
BF16 = jnp.bfloat16
EPS = 1e-6
NEG_INF = float("-inf")

N_HEADS = 8
DQK = 64
DV = 128
Q_HEADS = 16
KV_HEADS = 4
GROUP = Q_HEADS // KV_HEADS
HEAD_DIM = 64
WINDOW = 128
N_BUCKETS = 32
MAX_EXACT = N_BUCKETS // 2
MAX_DISTANCE = 128

LANES = 128
VMEM_LIMIT = 48 * 1024 * 1024

PROMPT_CHUNK = 128
PROMPT_BLOCK = 512
SAMPLE_BATCH_BLOCK = 8
ATTN_SAMPLE_BLOCK = 8
ATTN_PROMPT_BLOCKS = 4
MLP_ROW_CHUNK = 512
MLP_FF_BLOCK = 1024


def _cparams(*sem):
    return pltpu.CompilerParams(dimension_semantics=sem, vmem_limit_bytes=VMEM_LIMIT)


def _rms(x, g):
    return x * lax.rsqrt(jnp.mean(x * x, axis=-1, keepdims=True) + EPS) * g


def _split3(x):
    hi = x.astype(BF16)
    r = x - hi.astype(F32)
    mid = r.astype(BF16)
    lo = (r - mid.astype(F32)).astype(BF16)
    return hi, mid, lo


def _dot01(x, onehot):
    hi, mid, lo = _split3(x)
    d = lambda a: jnp.dot(a, onehot, preferred_element_type=F32)
    return d(hi) + d(mid) + d(lo)


def _dot01_left(onehot, x):
    hi, mid, lo = _split3(x)
    d = lambda a: jnp.dot(onehot, a, preferred_element_type=F32)
    return d(hi) + d(mid) + d(lo)


def _log_sigmoid(x):
    return jnp.minimum(x, 0.0) - jnp.log1p(jnp.exp(-jnp.abs(x)))


def _sigmoid(x):
    return 1.0 / (1.0 + jnp.exp(-x))


def _norm_mm_kernel(x_ref, *refs, splits):
    n_proj = len(splits)
    out_refs = iter(refs[2 * n_proj:])
    x = x_ref[...]
    xhat = x * lax.rsqrt(jnp.mean(x * x, axis=-1, keepdims=True) + EPS)
    ys = [(xhat * refs[2 * p][...]).astype(BF16) for p in range(n_proj)]
    rs = [jnp.dot(ys[p], refs[2 * p + 1][...], preferred_element_type=F32) for p in range(n_proj)]
    for p in range(n_proj):
        for off, n in splits[p]:
            o_ref = next(out_refs)
            o_ref[...] = rs[p][:, off:off + n].astype(o_ref.dtype)


def _layer_spec(arr, layer):
    idx = (layer,) + (0,) * (arr.ndim - 1)
    return pl.BlockSpec((None,) + arr.shape[1:], lambda *_: idx)


def norm_matmul(x, projections, tm):
    m, d = x.shape
    assert m % tm == 0
    params, param_specs, out_specs, out_shapes = [], [], [], []
    for g, lg, w, lw, splits, dtypes in projections:
        assert all(off + n <= w.shape[-1] for off, n in splits)
        params += [g, w]
        param_specs += [_layer_spec(g, lg), _layer_spec(w, lw)]
        out_specs += [pl.BlockSpec((tm, n), lambda i: (i, 0)) for _, n in splits]
        out_shapes += [jax.ShapeDtypeStruct((m, n), dt) for (_, n), dt in zip(splits, dtypes)]
    outs = pl.pallas_call(
        functools.partial(_norm_mm_kernel, splits=tuple(p[4] for p in projections)),
        grid=(m // tm,),
        in_specs=[pl.BlockSpec((tm, d), lambda i: (i, 0))] + param_specs,
        out_specs=out_specs,
        out_shape=out_shapes,
        compiler_params=_cparams("parallel"),
        name="norm_matmul",
    )(x, *params)
    outs, grouped = list(outs), []
    for p in projections:
        grouped.append(outs[:len(p[4])])
        outs = outs[len(p[4]):]
    return grouped


def _mlp_kernel(a_ref, wo_ref, x_ref, g_ref, wu_ref, wd_ref, gf_ref, o_ref, xn_ref, *,
                final_norm, row_chunk):
    j = pl.program_id(1)
    tm = x_ref.shape[0]

    @pl.when(j == 0)
    def _():
        for r in range(0, tm, row_chunk):
            rows = slice(r, r + row_chunk)
            x = x_ref[rows, :] + jnp.dot(a_ref[rows, :].astype(BF16), wo_ref[...],
                                         preferred_element_type=F32)
            xn_ref[rows, :] = _rms(x, g_ref[...]).astype(BF16)
            o_ref[rows, :] = x

    h = jnp.dot(xn_ref[...], wu_ref[...], preferred_element_type=F32)
    h = jnp.maximum(h, 0.0)
    o_ref[...] += jnp.dot((h * h).astype(BF16), wd_ref[...], preferred_element_type=F32)

    if final_norm:
        @pl.when(j == pl.num_programs(1) - 1)
        def _():
            o_ref[...] = _rms(o_ref[...], gf_ref[...])


def mlp(a, w_o, lo, x, g, w_up, w_down, l, g_final, final_norm, tm, tf):
    m, d = x.shape
    ff = w_up.shape[-1]
    return pl.pallas_call(
        functools.partial(_mlp_kernel, final_norm=final_norm, row_chunk=min(tm, MLP_ROW_CHUNK)),
        grid=(m // tm, ff // tf),
        in_specs=[pl.BlockSpec((tm, d), lambda i, j: (i, 0)),
                  _layer_spec(w_o, lo),
                  pl.BlockSpec((tm, d), lambda i, j: (i, 0)),
                  _layer_spec(g, l),
                  pl.BlockSpec((None, d, tf), lambda i, j: (l, 0, j)),
                  pl.BlockSpec((None, tf, d), lambda i, j: (l, j, 0)),
                  pl.BlockSpec((1, d), lambda i, j: (0, 0))],
        out_specs=pl.BlockSpec((tm, d), lambda i, j: (i, 0)),
        out_shape=jax.ShapeDtypeStruct((m, d), F32),
        scratch_shapes=[pltpu.VMEM((tm, d), BF16)],
        compiler_params=_cparams("parallel", "arbitrary"),
        name="mlp",
    )(a, w_o, x, g, w_up, w_down, g_final.reshape(1, d))


def _mlstm_inproj_kernel(x_ref, g_ref, w_ref, q_ref, k_ref, v_ref, o_ref, gate_ref, *, k_transposed):
    hq, hv = N_HEADS * DQK, N_HEADS * DV
    tm = x_ref.shape[0]
    y = _rms(x_ref[...], g_ref[...]).astype(BF16)
    r = jnp.dot(y, w_ref[...], preferred_element_type=F32)
    q_ref[...] = r[:, :hq].astype(q_ref.dtype)
    k = r[:, hq:2 * hq] * DQK ** -0.5
    k_ref[...] = (k.T if k_transposed else k).astype(k_ref.dtype)
    v_ref[...] = r[:, 2 * hq:2 * hq + hv].astype(v_ref.dtype)
    o_ref[...] = r[:, 2 * hq + hv:2 * hq + 2 * hv]
    gate_ref[...] = jnp.concatenate(
        [r[:, 2 * hq + 2 * hv:], jnp.zeros((tm, LANES - 2 * N_HEADS), F32)], axis=1)


def mlstm_inproj(x, g, w, l, tm, k_transposed, act):
    m, d = x.shape
    hq, hv = N_HEADS * DQK, N_HEADS * DV
    assert w.shape[-1] == 2 * hq + 2 * hv + 2 * N_HEADS
    row = lambda width: pl.BlockSpec((tm, width), lambda i: (i, 0))
    k_spec = pl.BlockSpec((hq, tm), lambda i: (0, i)) if k_transposed else row(hq)
    k_shape = (hq, m) if k_transposed else (m, hq)
    return pl.pallas_call(
        functools.partial(_mlstm_inproj_kernel, k_transposed=k_transposed),
        grid=(m // tm,),
        in_specs=[row(d), _layer_spec(g, l), _layer_spec(w, l)],
        out_specs=[row(hq), k_spec, row(hv), row(hv), row(LANES)],
        out_shape=[jax.ShapeDtypeStruct((m, hq), act), jax.ShapeDtypeStruct(k_shape, act),
                   jax.ShapeDtypeStruct((m, hv), act), jax.ShapeDtypeStruct((m, hv), F32),
                   jax.ShapeDtypeStruct((m, LANES), F32)],
        compiler_params=_cparams("parallel"),
        name="mlstm_inproj",
    )(x, g, w)


def _chunk_scan(x, pos, op, fill, length):
    k = 1
    while k < length:
        x = op(x, jnp.where(pos >= k, pltpu.roll(x, k, 0), fill))
        k *= 2
    return x


def _mlstm_prompt_kernel(q_ref, kt_ref, v_ref, o_ref, gate_ref, bi_ref, bf_ref, gain_ref,
                         hg_ref, cx_out_ref, m_out_ref, cx_s, m_s, *, chunk):
    t_blk = q_ref.shape[1]
    n_chunks = t_blk // chunk
    step = pl.program_id(1)

    @pl.when(step == 0)
    def _():
        cx_s[...] = jnp.zeros_like(cx_s)
        m_s[...] = jnp.zeros_like(m_s)

    gates = gate_ref[0]
    li = gates + bi_ref[...]
    lf = _log_sigmoid(pltpu.roll(gates, LANES - N_HEADS, 1) + bf_ref[...])
    pos = lax.broadcasted_iota(jnp.int32, (t_blk, 1), 0) % chunk
    b = _chunk_scan(lf, pos, jnp.add, 0.0, chunk)
    c = li - b
    cm = _chunk_scan(c, pos, jnp.maximum, NEG_INF, chunk)

    m_prev = m_s[...]
    xs, w_inters, e_negms, wks = [], [], [], []
    for ck in range(n_chunks):
        sl = slice(ck * chunk, (ck + 1) * chunk)
        m_t = b[sl] + jnp.maximum(m_prev, cm[sl])
        m_new = m_t[chunk - 1:chunk]
        b_last = b[(ck + 1) * chunk - 1:(ck + 1) * chunk]
        xs.append(b[sl] - m_t)
        w_inters.append(jnp.exp(b[sl] + m_prev - m_t))
        e_negms.append(jnp.exp(-m_t))
        wks.append(jnp.exp(c[sl] + (b_last - m_new)))
        m_prev = m_new
    m_s[...] = m_prev
    c_t = c.T
    wk_t = jnp.concatenate(wks, axis=0).T

    ri = lax.broadcasted_iota(jnp.int32, (chunk, chunk), 0)
    ci = lax.broadcasted_iota(jnp.int32, (chunk, chunk), 1)
    causal = ci <= ri
    ones_blk = jnp.ones((chunk, DV), BF16)
    ones_sq = jnp.ones((DV, DV), BF16)
    heads = range(N_HEADS)
    dot = functools.partial(jnp.dot, preferred_element_type=F32)

    cxs = [cx_s[h] for h in heads]
    for ck in range(n_chunks):
        sl = slice(ck * chunk, (ck + 1) * chunk)
        qs = [q_ref[0, sl, h * DQK:(h + 1) * DQK] for h in heads]
        kts = [kt_ref[h * DQK:(h + 1) * DQK, sl] for h in heads]
        vs = [v_ref[0, sl, h * DV:(h + 1) * DV] for h in heads]
        cx16s = [cxs[h].astype(BF16) for h in heads]
        kws = [(kts[h].astype(F32) * wk_t[h:h + 1, sl]).astype(BF16) for h in heads]
        vones = [jnp.concatenate([vs[h], ones_blk], axis=1) for h in heads]
        ss = [dot(qs[h], kts[h]) for h in heads]
        qcs = [dot(qs[h], cx16s[h]) for h in heads]
        dcs = [dot(kws[h], vones[h]) for h in heads]
        cxs = [w_inters[ck][chunk - 1:chunk, h:h + 1] * cxs[h] + dcs[h] for h in heads]
        a16s = []
        for h in heads:
            dm = xs[ck][:, h:h + 1] + c_t[h:h + 1, sl]
            a16s.append((ss[h] * jnp.exp(jnp.where(causal, dm, NEG_INF))).astype(BF16))
        avs = [dot(a16s[h], vones[h]) for h in heads]
        hhs = []
        for h in heads:
            wi = jnp.broadcast_to(w_inters[ck][:, h:h + 1], (chunk, DV))
            en = jnp.broadcast_to(e_negms[ck][:, h:h + 1], (chunk, DV))
            den = avs[h][:, DV:] + wi * qcs[h][:, DV:]
            inv = 1.0 / jnp.maximum(jnp.abs(den), en)
            hhs.append((avs[h][:, :DV] + qcs[h][:, :DV] * wi) * inv)
        sq16s = [(hhs[h] * hhs[h]).astype(BF16) for h in heads]
        mss = [dot(sq16s[h], ones_sq) for h in heads]
        for h in heads:
            hn = hhs[h] * lax.rsqrt(mss[h] * (1.0 / DV) + EPS)
            hn = hn * gain_ref[:, h * DV:(h + 1) * DV]
            og = _sigmoid(o_ref[0, sl, h * DV:(h + 1) * DV])
            hg_ref[0, sl, h * DV:(h + 1) * DV] = (og * hn).astype(hg_ref.dtype)
    for h in heads:
        cx_s[h] = cxs[h]

    @pl.when(step == pl.num_programs(1) - 1)
    def _():
        cx_out_ref[0] = cx_s[...]
        m_out_ref[0] = m_s[...]


def mlstm_prompt(q, kt, v, o, gate, b_i, b_f, gain, batch, seq):
    hq, hv = N_HEADS * DQK, N_HEADS * DV
    t = PROMPT_BLOCK
    nblk = seq // t
    r3 = lambda a: a.reshape(batch, seq, a.shape[-1])
    pad8 = lambda a: jnp.pad(a.reshape(1, N_HEADS), ((0, 0), (0, LANES - N_HEADS)))
    tok = lambda w: pl.BlockSpec((1, t, w), lambda bb, s: (bb, s, 0))
    cst = lambda w: pl.BlockSpec((1, w), lambda bb, s: (0, 0))
    hg, cx, m_new = pl.pallas_call(
        functools.partial(_mlstm_prompt_kernel, chunk=PROMPT_CHUNK),
        grid=(batch, nblk),
        in_specs=[tok(hq), pl.BlockSpec((hq, t), lambda bb, s: (0, bb * nblk + s)),
                  tok(hv), tok(hv), tok(LANES), cst(LANES), cst(LANES), cst(hv)],
        out_specs=[tok(hv),
                   pl.BlockSpec((1, N_HEADS, DQK, 2 * DV), lambda bb, s: (bb, 0, 0, 0)),
                   pl.BlockSpec((1, 1, LANES), lambda bb, s: (bb, 0, 0))],
        out_shape=[jax.ShapeDtypeStruct((batch, seq, hv), BF16),
                   jax.ShapeDtypeStruct((batch, N_HEADS, DQK, 2 * DV), F32),
                   jax.ShapeDtypeStruct((batch, 1, LANES), F32)],
        scratch_shapes=[pltpu.VMEM((N_HEADS, DQK, 2 * DV), F32), pltpu.VMEM((1, LANES), F32)],
        compiler_params=_cparams("parallel", "arbitrary"),
        name="mlstm_prompt",
    )(r3(q), kt, r3(v), r3(o), r3(gate), pad8(b_i), pad8(b_f), gain.reshape(1, hv))
    return hg.reshape(batch * seq, hv), cx[..., :DV], cx[..., DV], m_new[:, 0, :N_HEADS]


def _mlstm_sample_kernel(q_ref, k_ref, v_ref, o_ref, gate_ref, m0_ref, n0_ref, c0_ref, c_carry_ref,
                         bi_ref, bf_ref, gain_ref, seg64_ref, seg128_ref, e64_ref, e128_ref,
                         hg_ref, c_out_ref, n_out_ref, m_out_ref,
                         qc_s, *, seq):
    del c_carry_ref
    spans_layers = len(c_out_ref.shape) == 5
    c_new_ref = c_out_ref.at[0] if spans_layers else c_out_ref
    if spans_layers:
        for later in range(1, c_out_ref.shape[0]):
            c_out_ref[later] = jnp.zeros(c_out_ref.shape[1:], F32)
    rows = q_ref.shape[0]
    n_b = rows // seq
    tpos = lax.broadcasted_iota(jnp.int32, (rows, 1), 0) % seq
    dot = functools.partial(jnp.dot, preferred_element_type=F32)

    def shift(x, d):
        return x if d == 0 else pltpu.roll(x, d, 0)

    def unshift(x, d):
        return x if d == 0 else pltpu.roll(x, rows - d, 0)

    def split_cat(x):
        return jnp.concatenate(_split3(x), axis=1)


    q = q_ref[...].astype(BF16).astype(F32)
    k = k_ref[...].astype(BF16).astype(F32)
    v = v_ref[...].astype(BF16).astype(F32)

    units = [(p, h) for p in range(n_b // 2) for h in range(N_HEADS)]
    prow = lambda p: slice(p * 2 * seq, (p + 1) * 2 * seq)
    q_pairs = [q[prow(p), h * DQK:(h + 1) * DQK].astype(BF16) for p, h in units]
    c_cats = [jnp.concatenate([c0_ref[2 * p, h], c0_ref[2 * p + 1, h]], axis=1).astype(BF16)
              for p, h in units]
    qcs = [dot(q_pairs[i], c_cats[i]) for i in range(len(units))]

    gates = gate_ref[...]
    li = gates + bi_ref[...]
    lf = _log_sigmoid(pltpu.roll(gates, LANES - N_HEADS, 1) + bf_ref[...])
    b = lf
    for d in range(1, seq):
        b = b + jnp.where(tpos >= d, shift(lf, d), 0.0)
    inter = b + m0_ref[...]
    dvals = []
    m_t = inter
    for d in range(seq):
        dd = jnp.where(tpos >= d, b - shift(b, d) + shift(li, d), NEG_INF)
        dvals.append(dd)
        m_t = jnp.maximum(m_t, dd)
    w_inter = jnp.exp(inter - m_t)
    w_intra = [jnp.exp(dd - m_t) for dd in dvals]
    last = lambda x: functools.reduce(
        lambda acc, d: jnp.where(tpos == seq - 1 - d, unshift(x, d), acc), range(1, seq), x)
    wk = jnp.exp(last(b) - b + li - last(m_t))

    qk_in = [split_cat(q * shift(k, d)) for d in range(seq)]
    qn_in = split_cat(q * n0_ref[...].astype(BF16).astype(F32))
    wk_in, wi_in = split_cat(wk), split_cat(w_inter)
    seg64, e64, e128 = seg64_ref[...], e64_ref[...], e128_ref[...]
    qks = [dot(x, seg64) for x in qk_in]
    qn = dot(qn_in, seg64)
    wk_exp = dot(wk_in, e64)
    decay64 = dot(wi_in, e64)
    decay_exp = dot(wi_in, e128)

    a_s = [qks[d] * w_intra[d] for d in range(seq)]
    e128_1 = e128[:LANES]
    a_exps = [dot(a.astype(BF16), e128_1) for a in a_s]
    den = w_inter * qn
    num = jnp.zeros((rows, N_HEADS * DV), F32)
    for d in range(seq):
        den = den + a_s[d]
        num = num + a_exps[d] * shift(v, d)

    kw = k * wk_exp
    ksum = kw
    for d in range(1, seq):
        ksum = ksum + shift(kw, d)
    n_rows = decay64 * n0_ref[...] + ksum
    own = ((lax.broadcasted_iota(jnp.int32, (2 * seq, 2 * DV), 0) < seq)
           == (lax.broadcasted_iota(jnp.int32, (2 * seq, 2 * DV), 1) < DV))
    kw_pairs = [kw[prow(p), h * DQK:(h + 1) * DQK].astype(BF16) for p, h in units]
    v2s = []
    for p, h in units:
        v_pair = v[prow(p), h * DV:(h + 1) * DV]
        v2s.append(jnp.where(own, jnp.concatenate([v_pair, v_pair], axis=1), 0.0).astype(BF16))
    sel3 = (lax.broadcasted_iota(jnp.int32, (n_b, 3 * rows), 1) % rows
            == lax.broadcasted_iota(jnp.int32, (n_b, 3 * rows), 0) * seq + (seq - 1))
    sel3 = jnp.where(sel3, 1.0, 0.0).astype(BF16)
    n_in = jnp.concatenate(_split3(n_rows), axis=0)
    m_in = jnp.concatenate(_split3(m_t), axis=0)
    d_cs = [lax.dot_general(kw_pairs[i], v2s[i], (((0,), (0,)), ((), ())),
                            preferred_element_type=F32) for i in range(len(units))]
    n_out_ref[...] = dot(sel3, n_in)
    m_out_ref[...] = dot(sel3, m_in)

    first = lax.broadcasted_iota(jnp.int32, (2 * seq, DV), 0) < seq
    for i, (p, h) in enumerate(units):
        r0 = p * 2 * seq
        qc_s[prow(p), h * DV:(h + 1) * DV] = jnp.where(first, qcs[i][:, :DV], qcs[i][:, DV:])
        dec_a = decay_exp[r0 + seq - 1:r0 + seq, h * DV:(h + 1) * DV]
        dec_b = decay_exp[r0 + 2 * seq - 1:r0 + 2 * seq, h * DV:(h + 1) * DV]
        c_new_ref[2 * p, h] = dec_a * c0_ref[2 * p, h] + d_cs[i][:, :DV]
        c_new_ref[2 * p + 1, h] = dec_b * c0_ref[2 * p + 1, h] + d_cs[i][:, DV:]

    num = num + qc_s[...] * decay_exp
    inv_scale = 1.0 / jnp.maximum(jnp.abs(den), jnp.exp(-m_t))
    hh = num * dot(split_cat(inv_scale), e128)
    ms = dot(split_cat(hh * hh), seg128_ref[...]) * (1.0 / DV)
    hn = hh * dot(split_cat(lax.rsqrt(ms + EPS)), e128) * gain_ref[...]
    hg_ref[...] = (_sigmoid(o_ref[...]) * hn).astype(hg_ref.dtype)


def mlstm_sample(q, k, v, o, gate, b_i, b_f, gain, state_c, c_carry, l, n0, m0, batch, seq):
    hq, hv = N_HEADS * DQK, N_HEADS * DV
    c_blk = pl.BlockSpec((None, SAMPLE_BATCH_BLOCK, N_HEADS, DQK, DV), lambda i: (l, i, 0, 0, 0))
    c_all = pl.BlockSpec((state_c.shape[0], SAMPLE_BATCH_BLOCK, N_HEADS, DQK, DV),
                         lambda i: (0, i, 0, 0, 0))
    rows = SAMPLE_BATCH_BLOCK * seq
    pad8 = lambda a: jnp.pad(a.reshape(1, N_HEADS), ((0, 0), (0, LANES - N_HEADS)))
    m0_rows = jnp.pad(jnp.repeat(m0, seq, axis=0), ((0, 0), (0, LANES - N_HEADS)))
    n0_rows = jnp.repeat(n0.reshape(batch, hq), seq, axis=0)
    lane = jnp.arange(LANES)
    seg64 = (jnp.arange(hq)[:, None] // DQK == lane[None, :]).astype(BF16)
    seg128 = (jnp.arange(hv)[:, None] // DV == lane[None, :]).astype(BF16)
    tok = lambda w: pl.BlockSpec((rows, w), lambda i: (i, 0))
    cst = lambda a: pl.BlockSpec(a.shape, lambda i: (0,) * a.ndim)
    x3 = lambda a: jnp.tile(a, (3, 1))
    consts = [pad8(b_i), pad8(b_f), gain.reshape(1, hv),
              x3(seg64), x3(seg128), x3(seg64.T), x3(seg128.T)]
    hg, c_new, n_new, m_new = pl.pallas_call(
        functools.partial(_mlstm_sample_kernel, seq=seq),
        grid=(batch // SAMPLE_BATCH_BLOCK,),
        in_specs=[tok(hq), tok(hq), tok(hv), tok(hv), tok(LANES), tok(LANES), tok(hq),
                  c_blk, pl.BlockSpec(memory_space=pl.ANY)]
                 + [cst(a) for a in consts],
        out_specs=[tok(hv), c_all if l == 0 else c_blk,
                   pl.BlockSpec((SAMPLE_BATCH_BLOCK, hq), lambda i: (i, 0)),
                   pl.BlockSpec((SAMPLE_BATCH_BLOCK, LANES), lambda i: (i, 0))],
        out_shape=[jax.ShapeDtypeStruct((batch * seq, hv), BF16),
                   jax.ShapeDtypeStruct(state_c.shape, F32),
                   jax.ShapeDtypeStruct((batch, hq), F32),
                   jax.ShapeDtypeStruct((batch, LANES), F32)],
        scratch_shapes=[pltpu.VMEM((rows, hv), F32)],
        input_output_aliases={8: 1} if l > 0 else {},
        compiler_params=_cparams("parallel"),
        name="mlstm_sample",
    )(q, k, v, o, gate, m0_rows, n0_rows, state_c, c_carry, *consts)
    return hg, c_new, n_new.reshape(batch, N_HEADS, DQK), m_new[:, :N_HEADS]


def _bias_table_kernel(rb_ref, sink_ref, o_ref, *, sink_col):
    n_layers, _, nq, ns = o_ref.shape
    col = lax.broadcasted_iota(jnp.int32, (nq, ns), 1)
    dist = lax.broadcasted_iota(jnp.int32, (nq, ns), 0) + WINDOW - col
    n = jnp.maximum(dist, 0)
    large = MAX_EXACT + (jnp.log(jnp.maximum(n, 1).astype(F32) / MAX_EXACT)
                         / math.log(MAX_DISTANCE / MAX_EXACT) * (N_BUCKETS - MAX_EXACT)).astype(jnp.int32)
    large = jnp.minimum(large, N_BUCKETS - 1)
    bucket = jnp.where(n < MAX_EXACT, n, large)
    valid = (dist >= 0) & (dist < WINDOW)
    for h in range(Q_HEADS):
        acc = jnp.zeros((nq, ns), F32)
        for bkt in range(N_BUCKETS):
            acc = jnp.where(bucket == bkt, rb_ref[bkt, h], acc)
        acc = jnp.where(valid, acc, NEG_INF)
        for layer in range(n_layers):
            o_ref[layer, h] = jnp.where(col == sink_col, sink_ref[layer, h], acc)


def bias_table(rel_bias, sinks, nq, ns, sink_col):
    return pl.pallas_call(
        functools.partial(_bias_table_kernel, sink_col=sink_col),
        in_specs=[pl.BlockSpec(memory_space=pltpu.SMEM), pl.BlockSpec(memory_space=pltpu.SMEM)],
        out_specs=pl.BlockSpec(memory_space=pltpu.VMEM),
        out_shape=jax.ShapeDtypeStruct((sinks.shape[0], Q_HEADS, nq, ns), F32),
        name="bias_table",
    )(rel_bias, sinks)


PROMPT_SINK_KEY = 0


def _attn_prompt_kernel(q_ref, kp_ref, kc_ref, vp_ref, vc_ref, bias_ref, o_ref):
    dk = KV_HEADS * HEAD_DIM
    kvh = range(KV_HEADS)
    sink_row = lax.broadcasted_iota(jnp.int32, (2 * WINDOW, dk), 0) == PROMPT_SINK_KEY
    lane_head = lax.broadcasted_iota(jnp.int32, (2 * WINDOW, dk), 1) // HEAD_DIM
    out_head = lax.broadcasted_iota(jnp.int32, (GROUP * WINDOW, dk), 1) // HEAD_DIM
    zero = jnp.zeros((), BF16)
    first_of_sequence = pl.program_id(1) == 0
    for j in range(q_ref.shape[1] // WINDOW):
        rows = slice(j * WINDOW, (j + 1) * WINDOW)
        before = slice((j - 1) * WINDOW, j * WINDOW)
        k_prev, v_prev = (kp_ref[0], vp_ref[0]) if j == 0 else (kc_ref[0, before], vc_ref[0, before])
        variant = jnp.where(first_of_sequence, 0, 1) if j == 0 else 1
        k_all = jnp.where(sink_row, zero, jnp.concatenate([k_prev, kc_ref[0, rows]], axis=0))
        v_all = jnp.where(sink_row, zero, jnp.concatenate([v_prev, vc_ref[0, rows]], axis=0))
        k_bd = jnp.concatenate([jnp.where(lane_head == kh, k_all, zero) for kh in kvh], axis=0)
        v_bd = jnp.concatenate([jnp.where(lane_head == kh, v_all, zero) for kh in kvh], axis=0)
        q_cat = jnp.concatenate([q_ref[0, rows, g * dk:(g + 1) * dk] for g in range(GROUP)],
                                axis=0)
        s = lax.dot_general(q_cat, k_bd, (((1,), (1,)), ((), ())), preferred_element_type=F32)
        es, invs = [], []
        for kh in kvh:
            sk = s[:, kh * 2 * WINDOW:(kh + 1) * 2 * WINDOW] + bias_ref[variant, kh]
            e = jnp.exp(sk - jnp.max(sk, axis=1, keepdims=True))
            invs.append(1.0 / jnp.sum(e, axis=1, keepdims=True))
            es.append(e.astype(BF16))
        pv = jnp.dot(jnp.concatenate(es, axis=1), v_bd, preferred_element_type=F32)
        inv = invs[KV_HEADS - 1]
        for kh in range(KV_HEADS - 1):
            inv = jnp.where(out_head == kh, invs[kh], inv)
        o = pv * inv
        for g in range(GROUP):
            o_ref[0, rows, g * dk:(g + 1) * dk] = o[g * WINDOW:(g + 1) * WINDOW].astype(o_ref.dtype)


def attn_prompt(q, k, v, table, batch, seq):
    d = Q_HEADS * HEAD_DIM
    dk = KV_HEADS * HEAD_DIM
    nq = ATTN_PROMPT_BLOCKS
    r3 = lambda a: a.reshape(batch, seq, a.shape[-1])
    prev = pl.BlockSpec((1, WINDOW, dk), lambda b, i: (b, jnp.maximum(i * nq - 1, 0), 0))
    cur = pl.BlockSpec((1, nq * WINDOW, dk), lambda b, i: (b, i, 0))
    table = table.reshape(KV_HEADS, GROUP * WINDOW, 2 * WINDOW)
    key = jnp.arange(2 * WINDOW)
    no_prev = jnp.where((key < WINDOW) & (key != PROMPT_SINK_KEY), NEG_INF, table)
    tables = jnp.stack([no_prev, table])
    out = pl.pallas_call(
        _attn_prompt_kernel,
        grid=(batch, seq // (nq * WINDOW)),
        in_specs=[pl.BlockSpec((1, nq * WINDOW, d), lambda b, i: (b, i, 0)),
                  prev, cur, prev, cur,
                  pl.BlockSpec(tables.shape, lambda b, i: (0, 0, 0, 0))],
        out_specs=pl.BlockSpec((1, nq * WINDOW, d), lambda b, i: (b, i, 0)),
        out_shape=jax.ShapeDtypeStruct((batch, seq, d), BF16),
        compiler_params=_cparams("parallel", "arbitrary"),
        name="attn_prompt",
    )(r3(q), r3(k), r3(k), r3(v), r3(v), tables)
    return out.reshape(batch * seq, d)


SAMPLE_NEW_KEYS = 16


def _attn_sample_kernel(q_ref, kct_ref, vct_ref, kn_ref, vn_ref, knt_ref, vnt_ref, bias_c_ref,
                        bias_n_ref, o_ref, *cache_out_refs, seq):
    n_b = q_ref.shape[0]
    dk = KV_HEADS * HEAD_DIM
    pairs = [(b, kh) for b in range(n_b) for kh in range(KV_HEADS)]
    kslice = lambda kh: slice(kh * HEAD_DIM, (kh + 1) * HEAD_DIM)
    head_lanes = lambda kh, g: slice((g * KV_HEADS + kh) * HEAD_DIM, (g * KV_HEADS + kh + 1) * HEAD_DIM)
    pad = jnp.zeros((SAMPLE_NEW_KEYS - seq, dk), F32)
    nt = (((1,), (1,)), ((), ()))
    dot = functools.partial(jnp.dot, preferred_element_type=F32)
    dot_nt = functools.partial(lax.dot_general, dimension_numbers=nt, preferred_element_type=F32)

    if cache_out_refs:
        lane = lax.broadcasted_iota(jnp.int32, (HEAD_DIM, WINDOW), 1)
        steps_per_block = knt_ref.shape[1] // (n_b * seq)
        first_token = (pl.program_id(0) % steps_per_block) * (n_b * seq)
        for out_ref, old_ref, newt_ref in zip(cache_out_refs, (kct_ref, vct_ref), (knt_ref, vnt_ref)):
            for b, kh in pairs:
                moved = pltpu.roll(old_ref[b, kh], WINDOW - seq, 1)
                shift = (2 * WINDOW - seq - first_token - b * seq) % WINDOW
                fresh = pltpu.roll(newt_ref[kslice(kh), :], shift, 1)
                out_ref[b, kh] = jnp.where(lane >= WINDOW - seq, fresh, moved)

    qss = [jnp.concatenate([q_ref[b, :, head_lanes(kh, g)] for g in range(GROUP)],
                           axis=0).astype(BF16) for b, kh in pairs]
    kcts = [kct_ref[b, kh].astype(BF16) for b, kh in pairs]
    vcts = [vct_ref[b, kh].astype(BF16) for b, kh in pairs]
    kns = [jnp.concatenate([kn_ref[b], pad], axis=0).astype(BF16) for b in range(n_b)]
    vns = [jnp.concatenate([vn_ref[b], pad], axis=0).astype(BF16) for b in range(n_b)]
    s_cs = [dot(qss[i], kcts[i]) for i in range(len(pairs))]
    s_ns = [dot_nt(qss[i], kns[b][:, kslice(kh)]) for i, (b, kh) in enumerate(pairs)]
    e_cs, e_ns, denoms = [], [], []
    for i, (b, kh) in enumerate(pairs):
        s_c = s_cs[i] + bias_c_ref[kh]
        s_n = s_ns[i] + bias_n_ref[kh]
        m = jnp.maximum(jnp.max(s_c, axis=1, keepdims=True), jnp.max(s_n, axis=1, keepdims=True))
        e_c, e_n = jnp.exp(s_c - m), jnp.exp(s_n - m)
        denoms.append(jnp.sum(e_c, axis=1, keepdims=True) + jnp.sum(e_n, axis=1, keepdims=True))
        e_cs.append(e_c.astype(BF16))
        e_ns.append(e_n.astype(BF16))
    pv_cs = [dot_nt(e_cs[i], vcts[i]) for i in range(len(pairs))]
    pv_ns = [dot(e_ns[i], vns[b][:, kslice(kh)]) for i, (b, kh) in enumerate(pairs)]
    for i, (b, kh) in enumerate(pairs):
        o = (pv_cs[i] + pv_ns[i]) / denoms[i]
        for g in range(GROUP):
            o_ref[b, :, head_lanes(kh, g)] = o[g * seq:(g + 1) * seq].astype(o_ref.dtype)


def attn_sample(q, k_new, v_new, cache_k, cache_v, table, batch, seq, write_cache):
    d = Q_HEADS * HEAD_DIM
    dk = KV_HEADS * HEAD_DIM
    nbk = ATTN_SAMPLE_BLOCK
    assert WINDOW % (nbk * seq) == 0
    new = pl.BlockSpec((nbk, seq, dk), lambda i: (i, 0, 0))
    old = pl.BlockSpec((nbk, KV_HEADS, HEAD_DIM, WINDOW), lambda i: (i, 0, 0, 0))
    newt = pl.BlockSpec((dk, WINDOW), lambda i: (0, i * nbk * seq // WINDOW))
    to_t = lambda c: jnp.transpose(c, (0, 2, 3, 1))
    table = table.reshape(KV_HEADS, GROUP * seq, 2 * WINDOW)
    n_cache = 2 if write_cache else 0
    out = pl.pallas_call(
        functools.partial(_attn_sample_kernel, seq=seq),
        grid=(batch // nbk,),
        in_specs=[pl.BlockSpec((nbk, seq, d), lambda i: (i, 0, 0)),
                  old, old, new, new, newt, newt,
                  pl.BlockSpec((KV_HEADS, GROUP * seq, WINDOW), lambda i: (0, 0, 0)),
                  pl.BlockSpec((KV_HEADS, GROUP * seq, SAMPLE_NEW_KEYS), lambda i: (0, 0, 0))],
        out_specs=[pl.BlockSpec((nbk, seq, d), lambda i: (i, 0, 0))] + [old] * n_cache,
        out_shape=[jax.ShapeDtypeStruct((batch, seq, d), F32)]
                  + [jax.ShapeDtypeStruct((batch, KV_HEADS, HEAD_DIM, WINDOW), F32)] * n_cache,
        compiler_params=_cparams("parallel"),
        name="attn_sample",
    )(q.reshape(batch, seq, d), to_t(cache_k), to_t(cache_v),
      k_new.reshape(batch, seq, dk), v_new.reshape(batch, seq, dk), k_new.T, v_new.T,
      table[:, :, :WINDOW], table[:, :, WINDOW:WINDOW + SAMPLE_NEW_KEYS])
    return (out[0].reshape(batch * seq, d),) + tuple(jnp.transpose(c, (0, 3, 1, 2)) for c in out[1:])


def _trunk(x, state, cache, w, tm, tm_mlp):
    batch, seq, d = x.shape
    x = x.reshape(batch * seq, d)
    hq, hv = N_HEADS * DQK, N_HEADS * DV
    depth = w["norm_mix"].shape[0]
    n_a = w["w_in"].shape[0]
    prompt = state is None
    act = BF16 if prompt else F32
    cs, ns, ms = [], [], []
    c_stack = None if prompt else state[0]
    for l in range(depth):
        if l < n_a:
            gates = (w["b_igate"][l], w["b_fgate"][l], w["mlstm_norm"][l])
            q, k, v, o, gate = mlstm_inproj(x, w["norm_mix"], w["w_in"], l, tm, prompt, act)
            if prompt:
                a, c_new, n_new, m_new = mlstm_prompt(q, k, v, o, gate, *gates, batch, seq)
                cs.append(c_new)
            else:
                a, c_stack, n_new, m_new = mlstm_sample(q, k, v, o, gate, *gates, state[0], c_stack,
                                                        l, state[1][l], state[2][l], batch, seq)
            ns.append(n_new); ms.append(m_new)
            w_o, lo = w["w_mlstm_out"], l
        else:
            j = l - n_a
            q_proj = (w["norm_mix"], l, w["w_q"], j, ((0, Q_HEADS * HEAD_DIM),), (act,))
            if j == 0:
                dk = KV_HEADS * HEAD_DIM
                kv_proj = (w["kv_norm"], 0, w["w_kv"], 0, ((0, dk), (dk, dk), (0, dk), (dk, dk)),
                           (F32, F32, BF16, BF16))
                (k_new, v_new, k16, v16), (q,) = norm_matmul(x, (kv_proj, q_proj), tm)
                tables = bias_table(w["rel_bias"], w["attn_sinks"], WINDOW if prompt else seq,
                                    2 * WINDOW, PROMPT_SINK_KEY if prompt else WINDOW + seq)
            else:
                ((q,),) = norm_matmul(x, (q_proj,), tm)
            if prompt:
                a = attn_prompt(q, k16, v16, tables[j], batch, seq)
            elif j == 0:
                a, win_k, win_v = attn_sample(q, k_new, v_new, cache[0], cache[1], tables[j],
                                              batch, seq, True)
            else:
                (a,) = attn_sample(q, k_new, v_new, cache[0], cache[1], tables[j],
                                   batch, seq, False)
            w_o, lo = w["w_attn_out"], j
        x = mlp(a, w_o, lo, x, w["norm_ffn"], w["w_up"], w["w_down"], l, w["final_norm"],
                l == depth - 1, tm_mlp, MLP_FF_BLOCK)
    if prompt:
        dk = KV_HEADS * HEAD_DIM
        win_k = k_new.reshape(batch, seq, dk)[:, -WINDOW:]
        win_v = v_new.reshape(batch, seq, dk)[:, -WINDOW:]
        c_stack = jnp.stack(cs)
    shp = (batch, WINDOW, KV_HEADS, HEAD_DIM)
    return (x.reshape(batch, seq, d), c_stack, jnp.stack(ns), jnp.stack(ms),
            win_k.reshape(shp), win_v.reshape(shp))


def kernel(x_prompt, x_sample, state_C, state_n, state_m, cache_k, cache_v, norm_mix, norm_ffn,
           w_mlstm_in, b_igate, b_fgate, mlstm_norm, w_mlstm_out, kv_norm, w_kv, w_q, attn_sinks,
           w_attn_out, rel_bias, w_up, w_down, final_norm):
    n_b, d = w_q.shape[0], w_q.shape[1]
    heads = (KV_HEADS, GROUP, HEAD_DIM)
    w_q_perm = (w_q * HEAD_DIM ** -0.5).astype(BF16).reshape((n_b, d) + heads)
    w_q_perm = w_q_perm.transpose(0, 1, 3, 2, 4).reshape(n_b, d, d)
    w_ao_perm = w_attn_out.astype(BF16).reshape((n_b,) + heads + (d,))
    w_ao_perm = w_ao_perm.transpose(0, 2, 1, 3, 4).reshape(n_b, d, d)
    w = dict(norm_mix=norm_mix[:, None, :], norm_ffn=norm_ffn[:, None, :],
             w_in=w_mlstm_in.astype(BF16), b_igate=b_igate, b_fgate=b_fgate,
             mlstm_norm=mlstm_norm, w_mlstm_out=w_mlstm_out.astype(BF16),
             kv_norm=kv_norm[None, None, :], w_kv=w_kv.astype(BF16)[None],
             w_q=w_q_perm, attn_sinks=attn_sinks, w_attn_out=w_ao_perm, rel_bias=rel_bias,
             w_up=w_up.astype(BF16), w_down=w_down.astype(BF16), final_norm=final_norm)
    y_p, c_p, n_p, m_p, k_p, v_p = _trunk(x_prompt, None, None, w, 512, 1024)
    y_s, c_s, n_s, m_s, k_s, v_s = _trunk(x_sample, (state_C, state_n, state_m),
                                          (cache_k, cache_v), w, 512, 512)
    return (y_p, y_s, c_p, n_p, m_p, k_p, v_p, c_s, n_s, m_s, k_s, v_s)
```

```python
import functools
import math

import jax
import jax.numpy as jnp
from jax import lax
from jax.experimental import pallas as pl
from jax.experimental.pallas import tpu as pltpu

F32 = jnp.float32
BF16 = jnp.bfloat16
EPS = 1e-6
NEG_INF = float("-inf")

N_HEADS = 8
DQK = 64
DV = 128
Q_HEADS = 16
KV_HEADS = 4
GROUP = Q_HEADS // KV_HEADS
HEAD_DIM = 64
WINDOW = 128
N_BUCKETS = 32
MAX_EXACT = N_BUCKETS // 2
MAX_DISTANCE = 128

LANES = 128
VMEM_LIMIT = 56 * 1024 * 1024

PROMPT_CHUNK = 128
PROMPT_BLOCK = 1024
SAMPLE_BATCH_BLOCK = 8
ATTN_SAMPLE_BLOCK = 8
ATTN_PROMPT_BLOCKS = 8
MLP_ROW_CHUNK = 512
MLP_FF_BLOCK = 2048
MLP_FF_CHUNK = 1024


def _cparams(*sem):
    return pltpu.CompilerParams(dimension_semantics=sem, vmem_limit_bytes=VMEM_LIMIT)


def _rms(x, g):
    return x * lax.rsqrt(jnp.mean(x * x, axis=-1, keepdims=True) + EPS) * g


def _split3(x):
    hi = x.astype(BF16)
    r = x - hi.astype(F32)
    mid = r.astype(BF16)
    lo = (r - mid.astype(F32)).astype(BF16)
    return hi, mid, lo


def _dot01(x, onehot):
    hi, mid, lo = _split3(x)
    d = lambda a: jnp.dot(a, onehot, preferred_element_type=F32)
    return d(hi) + d(mid) + d(lo)


def _dot01_left(onehot, x):
    hi, mid, lo = _split3(x)
    d = lambda a: jnp.dot(onehot, a, preferred_element_type=F32)
    return d(hi) + d(mid) + d(lo)


def _log_sigmoid(x):
    return jnp.minimum(x, 0.0) - jnp.log1p(jnp.exp(-jnp.abs(x)))


def _sigmoid(x):
    return 1.0 / (1.0 + jnp.exp(-x))


def _norm_mm_kernel(x_ref, *refs, splits):
    n_proj = len(splits)
    out_refs = iter(refs[2 * n_proj:])
    x = x_ref[...]
    xhat = x * lax.rsqrt(jnp.mean(x * x, axis=-1, keepdims=True) + EPS)
    ys = [(xhat * refs[2 * p][...]).astype(BF16) for p in range(n_proj)]
    rs = [jnp.dot(ys[p], refs[2 * p + 1][...], preferred_element_type=F32) for p in range(n_proj)]
    for p in range(n_proj):
        for off, n in splits[p]:
            o_ref = next(out_refs)
            o_ref[...] = rs[p][:, off:off + n].astype(o_ref.dtype)


def _layer_spec(arr, layer):
    idx = (layer,) + (0,) * (arr.ndim - 1)
    return pl.BlockSpec((None,) + arr.shape[1:], lambda *_: idx)


def norm_matmul(x, projections, tm):
    m, d = x.shape
    assert m % tm == 0
    params, param_specs, out_specs, out_shapes = [], [], [], []
    for g, lg, w, lw, splits, dtypes in projections:
        assert all(off + n <= w.shape[-1] for off, n in splits)
        params += [g, w]
        param_specs += [_layer_spec(g, lg), _layer_spec(w, lw)]
        out_specs += [pl.BlockSpec((tm, n), lambda i: (i, 0)) for _, n in splits]
        out_shapes += [jax.ShapeDtypeStruct((m, n), dt) for (_, n), dt in zip(splits, dtypes)]
    outs = pl.pallas_call(
        functools.partial(_norm_mm_kernel, splits=tuple(p[4] for p in projections)),
        grid=(m // tm,),
        in_specs=[pl.BlockSpec((tm, d), lambda i: (i, 0))] + param_specs,
        out_specs=out_specs,
        out_shape=out_shapes,
        compiler_params=_cparams("parallel"),
        name="norm_matmul",
    )(x, *params)
    outs, grouped = list(outs), []
    for p in projections:
        grouped.append(outs[:len(p[4])])
        outs = outs[len(p[4]):]
    return grouped


def _mlp_kernel(a_ref, wo_ref, x_ref, g_ref, wu_ref, wd_ref, gf_ref, o_ref, xn_ref, *,
                final_norm, row_chunk, ff_chunk):
    j = pl.program_id(1)
    tm = x_ref.shape[0]

    @pl.when(j == 0)
    def _():
        for r in range(0, tm, row_chunk):
            rows = slice(r, r + row_chunk)
            x = x_ref[rows, :] + jnp.dot(a_ref[rows, :].astype(BF16), wo_ref[...],
                                         preferred_element_type=F32)
            xn_ref[rows, :] = _rms(x, g_ref[...]).astype(BF16)
            o_ref[rows, :] = x

    for c in range(0, wu_ref.shape[1], ff_chunk):
        h = jnp.dot(xn_ref[...], wu_ref[:, c:c + ff_chunk], preferred_element_type=F32)
        h = jnp.maximum(h, 0.0)
        o_ref[...] += jnp.dot((h * h).astype(BF16), wd_ref[c:c + ff_chunk, :],
                              preferred_element_type=F32)

    if final_norm:
        @pl.when(j == pl.num_programs(1) - 1)
        def _():
            o_ref[...] = _rms(o_ref[...], gf_ref[...])


def mlp(a, w_o, lo, x, g, w_up, w_down, l, g_final, final_norm, tm, tf):
    m, d = x.shape
    ff = w_up.shape[-1]
    return pl.pallas_call(
        functools.partial(_mlp_kernel, final_norm=final_norm, row_chunk=min(tm, MLP_ROW_CHUNK),
                          ff_chunk=MLP_FF_CHUNK),
        grid=(m // tm, ff // tf),
        in_specs=[pl.BlockSpec((tm, d), lambda i, j: (i, 0)),
                  _layer_spec(w_o, lo),
                  pl.BlockSpec((tm, d), lambda i, j: (i, 0)),
                  _layer_spec(g, l),
                  pl.BlockSpec((None, d, tf), lambda i, j: (l, 0, j)),
                  pl.BlockSpec((None, tf, d), lambda i, j: (l, j, 0)),
                  pl.BlockSpec((1, d), lambda i, j: (0, 0))],
        out_specs=pl.BlockSpec((tm, d), lambda i, j: (i, 0)),
        out_shape=jax.ShapeDtypeStruct((m, d), F32),
        scratch_shapes=[pltpu.VMEM((tm, d), BF16)],
        compiler_params=_cparams("parallel", "arbitrary"),
        name="mlp",
    )(a, w_o, x, g, w_up, w_down, g_final.reshape(1, d))


def _mlstm_inproj_kernel(x_ref, g_ref, w_ref, q_ref, k_ref, v_ref, o_ref, gate_ref, *, k_transposed):
    hq, hv = N_HEADS * DQK, N_HEADS * DV
    tm = x_ref.shape[0]
    y = _rms(x_ref[...], g_ref[...]).astype(BF16)
    r = jnp.dot(y, w_ref[...], preferred_element_type=F32)
    q_ref[...] = r[:, :hq].astype(q_ref.dtype)
    k = r[:, hq:2 * hq] * DQK ** -0.5
    k_ref[...] = (k.T if k_transposed else k).astype(k_ref.dtype)
    v_ref[...] = r[:, 2 * hq:2 * hq + hv].astype(v_ref.dtype)
    o_ref[...] = r[:, 2 * hq + hv:2 * hq + 2 * hv]
    gate_ref[...] = jnp.concatenate(
        [r[:, 2 * hq + 2 * hv:], jnp.zeros((tm, LANES - 2 * N_HEADS), F32)], axis=1)


def mlstm_inproj(x, g, w, l, tm, k_transposed, act):
    m, d = x.shape
    hq, hv = N_HEADS * DQK, N_HEADS * DV
    assert w.shape[-1] == 2 * hq + 2 * hv + 2 * N_HEADS
    row = lambda width: pl.BlockSpec((tm, width), lambda i: (i, 0))
    k_spec = pl.BlockSpec((hq, tm), lambda i: (0, i)) if k_transposed else row(hq)
    k_shape = (hq, m) if k_transposed else (m, hq)
    return pl.pallas_call(
        functools.partial(_mlstm_inproj_kernel, k_transposed=k_transposed),
        grid=(m // tm,),
        in_specs=[row(d), _layer_spec(g, l), _layer_spec(w, l)],
        out_specs=[row(hq), k_spec, row(hv), row(hv), row(LANES)],
        out_shape=[jax.ShapeDtypeStruct((m, hq), act), jax.ShapeDtypeStruct(k_shape, act),
                   jax.ShapeDtypeStruct((m, hv), act), jax.ShapeDtypeStruct((m, hv), F32),
                   jax.ShapeDtypeStruct((m, LANES), F32)],
        compiler_params=_cparams("parallel"),
        name="mlstm_inproj",
    )(x, g, w)


def _chunk_scan(x, pos, op, fill, length):
    k = 1
    while k < length:
        x = op(x, jnp.where(pos >= k, pltpu.roll(x, k, 0), fill))
        k *= 2
    return x


def _mlstm_prompt_kernel(q_ref, kt_ref, v_ref, o_ref, gate_ref, bi_ref, bf_ref, gain_ref,
                         hg_ref, cx_out_ref, m_out_ref, cx_s, m_s, *, chunk):
    t_blk = q_ref.shape[1]
    n_chunks = t_blk // chunk
    step = pl.program_id(1)

    @pl.when(step == 0)
    def _():
        cx_s[...] = jnp.zeros_like(cx_s)
        m_s[...] = jnp.zeros_like(m_s)

    gates = gate_ref[0]
    li = gates + bi_ref[...]
    lf = _log_sigmoid(pltpu.roll(gates, LANES - N_HEADS, 1) + bf_ref[...])
    pos = lax.broadcasted_iota(jnp.int32, (t_blk, 1), 0) % chunk
    b = _chunk_scan(lf, pos, jnp.add, 0.0, chunk)
    c = li - b
    cm = _chunk_scan(c, pos, jnp.maximum, NEG_INF, chunk)

    m_prev = m_s[...]
    xs, w_inters, e_negms, wks = [], [], [], []
    for ck in range(n_chunks):
        sl = slice(ck * chunk, (ck + 1) * chunk)
        m_t = b[sl] + jnp.maximum(m_prev, cm[sl])
        m_new = m_t[chunk - 1:chunk]
        b_last = b[(ck + 1) * chunk - 1:(ck + 1) * chunk]
        xs.append(b[sl] - m_t)
        w_inters.append(jnp.exp(b[sl] + m_prev - m_t))
        e_negms.append(jnp.exp(-m_t))
        wks.append(jnp.exp(c[sl] + (b_last - m_new)))
        m_prev = m_new
    m_s[...] = m_prev
    c_t = c.T
    wk_t = jnp.concatenate(wks, axis=0).T

    ri = lax.broadcasted_iota(jnp.int32, (chunk, chunk), 0)
    ci = lax.broadcasted_iota(jnp.int32, (chunk, chunk), 1)
    causal = ci <= ri
    ones_blk = jnp.ones((chunk, DV), BF16)
    ones_sq = jnp.ones((DV, DV), BF16)
    heads = range(N_HEADS)
    dot = functools.partial(jnp.dot, preferred_element_type=F32)

    cxs = [cx_s[h] for h in heads]
    for ck in range(n_chunks):
        sl = slice(ck * chunk, (ck + 1) * chunk)
        qs = [q_ref[0, sl, h * DQK:(h + 1) * DQK] for h in heads]
        kts = [kt_ref[h * DQK:(h + 1) * DQK, sl] for h in heads]
        vs = [v_ref[0, sl, h * DV:(h + 1) * DV] for h in heads]
        cx16s = [cxs[h].astype(BF16) for h in heads]
        kws = [(kts[h].astype(F32) * wk_t[h:h + 1, sl]).astype(BF16) for h in heads]
        vones = [jnp.concatenate([vs[h], ones_blk], axis=1) for h in heads]
        ss = [dot(qs[h], kts[h]) for h in heads]
        qcs = [dot(qs[h], cx16s[h]) for h in heads]
        dcs = [dot(kws[h], vones[h]) for h in heads]
        cxs = [w_inters[ck][chunk - 1:chunk, h:h + 1] * cxs[h] + dcs[h] for h in heads]
        a16s = []
        for h in heads:
            dm = xs[ck][:, h:h + 1] + c_t[h:h + 1, sl]
            a16s.append((ss[h] * jnp.exp(jnp.where(causal, dm, NEG_INF))).astype(BF16))
        avs = [dot(a16s[h], vones[h]) for h in heads]
        hhs = []
        for h in heads:
            wi = jnp.broadcast_to(w_inters[ck][:, h:h + 1], (chunk, DV))
            en = jnp.broadcast_to(e_negms[ck][:, h:h + 1], (chunk, DV))
            den = avs[h][:, DV:] + wi * qcs[h][:, DV:]
            inv = 1.0 / jnp.maximum(jnp.abs(den), en)
            hhs.append((avs[h][:, :DV] + qcs[h][:, :DV] * wi) * inv)
        sq16s = [(hhs[h] * hhs[h]).astype(BF16) for h in heads]
        mss = [dot(sq16s[h], ones_sq) for h in heads]
        for h in heads:
            hn = hhs[h] * lax.rsqrt(mss[h] * (1.0 / DV) + EPS)
            hn = hn * gain_ref[:, h * DV:(h + 1) * DV]
            og = _sigmoid(o_ref[0, sl, h * DV:(h + 1) * DV])
            hg_ref[0, sl, h * DV:(h + 1) * DV] = (og * hn).astype(hg_ref.dtype)
    for h in heads:
        cx_s[h] = cxs[h]

    @pl.when(step == pl.num_programs(1) - 1)
    def _():
        cx_out_ref[0] = cx_s[...]
        m_out_ref[0] = m_s[...]


def mlstm_prompt(q, kt, v, o, gate, b_i, b_f, gain, batch, seq):
    hq, hv = N_HEADS * DQK, N_HEADS * DV
    t = PROMPT_BLOCK
    nblk = seq // t
    r3 = lambda a: a.reshape(batch, seq, a.shape[-1])
    pad8 = lambda a: jnp.pad(a.reshape(1, N_HEADS), ((0, 0), (0, LANES - N_HEADS)))
    tok = lambda w: pl.BlockSpec((1, t, w), lambda bb, s: (bb, s, 0))
    cst = lambda w: pl.BlockSpec((1, w), lambda bb, s: (0, 0))
    hg, cx, m_new = pl.pallas_call(
        functools.partial(_mlstm_prompt_kernel, chunk=PROMPT_CHUNK),
        grid=(batch, nblk),
        in_specs=[tok(hq), pl.BlockSpec((hq, t), lambda bb, s: (0, bb * nblk + s)),
                  tok(hv), tok(hv), tok(LANES), cst(LANES), cst(LANES), cst(hv)],
        out_specs=[tok(hv),
                   pl.BlockSpec((1, N_HEADS, DQK, 2 * DV), lambda bb, s: (bb, 0, 0, 0)),
                   pl.BlockSpec((1, 1, LANES), lambda bb, s: (bb, 0, 0))],
        out_shape=[jax.ShapeDtypeStruct((batch, seq, hv), BF16),
                   jax.ShapeDtypeStruct((batch, N_HEADS, DQK, 2 * DV), F32),
                   jax.ShapeDtypeStruct((batch, 1, LANES), F32)],
        scratch_shapes=[pltpu.VMEM((N_HEADS, DQK, 2 * DV), F32), pltpu.VMEM((1, LANES), F32)],
        compiler_params=_cparams("parallel", "arbitrary"),
        name="mlstm_prompt",
    )(r3(q), kt, r3(v), r3(o), r3(gate), pad8(b_i), pad8(b_f), gain.reshape(1, hv))
    return hg.reshape(batch * seq, hv), cx[..., :DV], cx[..., DV], m_new[:, 0, :N_HEADS]


def _mlstm_sample_kernel(q_ref, k_ref, v_ref, o_ref, gate_ref, m0_ref, n0_ref, c0_ref, c_carry_ref,
                         bi_ref, bf_ref, gain_ref, seg64_ref, seg128_ref, e64_ref, e128_ref,
                         hg_ref, c_out_ref, n_out_ref, m_out_ref,
                         qc_s, *, seq):
    del c_carry_ref
    spans_layers = len(c_out_ref.shape) == 5
    c_new_ref = c_out_ref.at[0] if spans_layers else c_out_ref
    if spans_layers:
        for later in range(1, c_out_ref.shape[0]):
            c_out_ref[later] = jnp.zeros(c_out_ref.shape[1:], F32)
    rows = q_ref.shape[0]
    n_b = rows // seq
    tpos = lax.broadcasted_iota(jnp.int32, (rows, 1), 0) % seq
    dot = functools.partial(jnp.dot, preferred_element_type=F32)

    def shift(x, d):
        return x if d == 0 else pltpu.roll(x, d, 0)

    def unshift(x, d):
        return x if d == 0 else pltpu.roll(x, rows - d, 0)

    def split_cat(x):
        return jnp.concatenate(_split3(x), axis=1)


    q = q_ref[...].astype(BF16).astype(F32)
    k = k_ref[...].astype(BF16).astype(F32)
    v = v_ref[...].astype(BF16).astype(F32)

    units = [(p, h) for p in range(n_b // 2) for h in range(N_HEADS)]
    prow = lambda p: slice(p * 2 * seq, (p + 1) * 2 * seq)
    q_pairs = [q[prow(p), h * DQK:(h + 1) * DQK].astype(BF16) for p, h in units]
    c_cats = [jnp.concatenate([c0_ref[2 * p, h], c0_ref[2 * p + 1, h]], axis=1).astype(BF16)
              for p, h in units]
    qcs = [dot(q_pairs[i], c_cats[i]) for i in range(len(units))]

    gates = gate_ref[...]
    li = gates + bi_ref[...]
    lf = _log_sigmoid(pltpu.roll(gates, LANES - N_HEADS, 1) + bf_ref[...])
    b = lf
    for d in range(1, seq):
        b = b + jnp.where(tpos >= d, shift(lf, d), 0.0)
    inter = b + m0_ref[...]
    dvals = []
    m_t = inter
    for d in range(seq):
        dd = jnp.where(tpos >= d, b - shift(b, d) + shift(li, d), NEG_INF)
        dvals.append(dd)
        m_t = jnp.maximum(m_t, dd)
    w_inter = jnp.exp(inter - m_t)
    w_intra = [jnp.exp(dd - m_t) for dd in dvals]
    last = lambda x: functools.reduce(
        lambda acc, d: jnp.where(tpos == seq - 1 - d, unshift(x, d), acc), range(1, seq), x)
    wk = jnp.exp(last(b) - b + li - last(m_t))

    qk_in = [split_cat(q * shift(k, d)) for d in range(seq)]
    qn_in = split_cat(q * n0_ref[...].astype(BF16).astype(F32))
    wk_in, wi_in = split_cat(wk), split_cat(w_inter)
    seg64, e64, e128 = seg64_ref[...], e64_ref[...], e128_ref[...]
    qks = [dot(x, seg64) for x in qk_in]
    qn = dot(qn_in, seg64)
    wk_exp = dot(wk_in, e64)
    decay64 = dot(wi_in, e64)
    decay_exp = dot(wi_in, e128)

    a_s = [qks[d] * w_intra[d] for d in range(seq)]
    e128_1 = e128[:LANES]
    a_exps = [dot(a.astype(BF16), e128_1) for a in a_s]
    den = w_inter * qn
    num = jnp.zeros((rows, N_HEADS * DV), F32)
    for d in range(seq):
        den = den + a_s[d]
        num = num + a_exps[d] * shift(v, d)

    kw = k * wk_exp
    ksum = kw
    for d in range(1, seq):
        ksum = ksum + shift(kw, d)
    n_rows = decay64 * n0_ref[...] + ksum
    own = ((lax.broadcasted_iota(jnp.int32, (2 * seq, 2 * DV), 0) < seq)
           == (lax.broadcasted_iota(jnp.int32, (2 * seq, 2 * DV), 1) < DV))
    kw_pairs = [kw[prow(p), h * DQK:(h + 1) * DQK].astype(BF16) for p, h in units]
    v2s = []
    for p, h in units:
        v_pair = v[prow(p), h * DV:(h + 1) * DV]
        v2s.append(jnp.where(own, jnp.concatenate([v_pair, v_pair], axis=1), 0.0).astype(BF16))
    sel3 = (lax.broadcasted_iota(jnp.int32, (n_b, 3 * rows), 1) % rows
            == lax.broadcasted_iota(jnp.int32, (n_b, 3 * rows), 0) * seq + (seq - 1))
    sel3 = jnp.where(sel3, 1.0, 0.0).astype(BF16)
    n_in = jnp.concatenate(_split3(n_rows), axis=0)
    m_in = jnp.concatenate(_split3(m_t), axis=0)
    d_cs = [lax.dot_general(kw_pairs[i], v2s[i], (((0,), (0,)), ((), ())),
                            preferred_element_type=F32) for i in range(len(units))]
    n_out_ref[...] = dot(sel3, n_in)
    m_out_ref[...] = dot(sel3, m_in)

    first = lax.broadcasted_iota(jnp.int32, (2 * seq, DV), 0) < seq
    for i, (p, h) in enumerate(units):
        r0 = p * 2 * seq
        qc_s[prow(p), h * DV:(h + 1) * DV] = jnp.where(first, qcs[i][:, :DV], qcs[i][:, DV:])
        dec_a = decay_exp[r0 + seq - 1:r0 + seq, h * DV:(h + 1) * DV]
        dec_b = decay_exp[r0 + 2 * seq - 1:r0 + 2 * seq, h * DV:(h + 1) * DV]
        c_new_ref[2 * p, h] = dec_a * c0_ref[2 * p, h] + d_cs[i][:, :DV]
        c_new_ref[2 * p + 1, h] = dec_b * c0_ref[2 * p + 1, h] + d_cs[i][:, DV:]

    num = num + qc_s[...] * decay_exp
    inv_scale = 1.0 / jnp.maximum(jnp.abs(den), jnp.exp(-m_t))
    hh = num * dot(split_cat(inv_scale), e128)
    ms = dot(split_cat(hh * hh), seg128_ref[...]) * (1.0 / DV)
    hn = hh * dot(split_cat(lax.rsqrt(ms + EPS)), e128) * gain_ref[...]
    hg_ref[...] = (_sigmoid(o_ref[...]) * hn).astype(hg_ref.dtype)


def mlstm_sample(q, k, v, o, gate, b_i, b_f, gain, state_c, c_carry, l, n0, m0, batch, seq):
    hq, hv = N_HEADS * DQK, N_HEADS * DV
    c_blk = pl.BlockSpec((None, SAMPLE_BATCH_BLOCK, N_HEADS, DQK, DV), lambda i: (l, i, 0, 0, 0))
    c_all = pl.BlockSpec((state_c.shape[0], SAMPLE_BATCH_BLOCK, N_HEADS, DQK, DV),
                         lambda i: (0, i, 0, 0, 0))
    rows = SAMPLE_BATCH_BLOCK * seq
    pad8 = lambda a: jnp.pad(a.reshape(1, N_HEADS), ((0, 0), (0, LANES - N_HEADS)))
    m0_rows = jnp.pad(jnp.repeat(m0, seq, axis=0), ((0, 0), (0, LANES - N_HEADS)))
    n0_rows = jnp.repeat(n0.reshape(batch, hq), seq, axis=0)
    lane = jnp.arange(LANES)
    seg64 = (jnp.arange(hq)[:, None] // DQK == lane[None, :]).astype(BF16)
    seg128 = (jnp.arange(hv)[:, None] // DV == lane[None, :]).astype(BF16)
    tok = lambda w: pl.BlockSpec((rows, w), lambda i: (i, 0))
    cst = lambda a: pl.BlockSpec(a.shape, lambda i: (0,) * a.ndim)
    x3 = lambda a: jnp.tile(a, (3, 1))
    consts = [pad8(b_i), pad8(b_f), gain.reshape(1, hv),
              x3(seg64), x3(seg128), x3(seg64.T), x3(seg128.T)]
    hg, c_new, n_new, m_new = pl.pallas_call(
        functools.partial(_mlstm_sample_kernel, seq=seq),
        grid=(batch // SAMPLE_BATCH_BLOCK,),
        in_specs=[tok(hq), tok(hq), tok(hv), tok(hv), tok(LANES), tok(LANES), tok(hq),
                  c_blk, pl.BlockSpec(memory_space=pl.ANY)]
                 + [cst(a) for a in consts],
        out_specs=[tok(hv), c_all if l == 0 else c_blk,
                   pl.BlockSpec((SAMPLE_BATCH_BLOCK, hq), lambda i: (i, 0)),
                   pl.BlockSpec((SAMPLE_BATCH_BLOCK, LANES), lambda i: (i, 0))],
        out_shape=[jax.ShapeDtypeStruct((batch * seq, hv), BF16),
                   jax.ShapeDtypeStruct(state_c.shape, F32),
                   jax.ShapeDtypeStruct((batch, hq), F32),
                   jax.ShapeDtypeStruct((batch, LANES), F32)],
        scratch_shapes=[pltpu.VMEM((rows, hv), F32)],
        input_output_aliases={8: 1} if l > 0 else {},
        compiler_params=_cparams("parallel"),
        name="mlstm_sample",
    )(q, k, v, o, gate, m0_rows, n0_rows, state_c, c_carry, *consts)
    return hg, c_new, n_new.reshape(batch, N_HEADS, DQK), m_new[:, :N_HEADS]


def _bias_table_kernel(rb_ref, sink_ref, o_ref, *, sink_col):
    n_layers, _, nq, ns = o_ref.shape
    col = lax.broadcasted_iota(jnp.int32, (nq, ns), 1)
    dist = lax.broadcasted_iota(jnp.int32, (nq, ns), 0) + WINDOW - col
    n = jnp.maximum(dist, 0)
    large = MAX_EXACT + (jnp.log(jnp.maximum(n, 1).astype(F32) / MAX_EXACT)
                         / math.log(MAX_DISTANCE / MAX_EXACT) * (N_BUCKETS - MAX_EXACT)).astype(jnp.int32)
    large = jnp.minimum(large, N_BUCKETS - 1)
    bucket = jnp.where(n < MAX_EXACT, n, large)
    valid = (dist >= 0) & (dist < WINDOW)
    for h in range(Q_HEADS):
        acc = jnp.zeros((nq, ns), F32)
        for bkt in range(N_BUCKETS):
            acc = jnp.where(bucket == bkt, rb_ref[bkt, h], acc)
        acc = jnp.where(valid, acc, NEG_INF)
        for layer in range(n_layers):
            o_ref[layer, h] = jnp.where(col == sink_col, sink_ref[layer, h], acc)


def bias_table(rel_bias, sinks, nq, ns, sink_col):
    return pl.pallas_call(
        functools.partial(_bias_table_kernel, sink_col=sink_col),
        in_specs=[pl.BlockSpec(memory_space=pltpu.SMEM), pl.BlockSpec(memory_space=pltpu.SMEM)],
        out_specs=pl.BlockSpec(memory_space=pltpu.VMEM),
        out_shape=jax.ShapeDtypeStruct((sinks.shape[0], Q_HEADS, nq, ns), F32),
        name="bias_table",
    )(rel_bias, sinks)


PROMPT_SINK_KEY = 0


def _attn_prompt_kernel(q_ref, kp_ref, kc_ref, vp_ref, vc_ref, bias_ref, o_ref):
    dk = KV_HEADS * HEAD_DIM
    kvh = range(KV_HEADS)
    sink_row = lax.broadcasted_iota(jnp.int32, (2 * WINDOW, dk), 0) == PROMPT_SINK_KEY
    lane_head = lax.broadcasted_iota(jnp.int32, (2 * WINDOW, dk), 1) // HEAD_DIM
    out_head = lax.broadcasted_iota(jnp.int32, (GROUP * WINDOW, dk), 1) // HEAD_DIM
    zero = jnp.zeros((), BF16)
    first_of_sequence = pl.program_id(1) == 0
    for j in range(q_ref.shape[1] // WINDOW):
        rows = slice(j * WINDOW, (j + 1) * WINDOW)
        before = slice((j - 1) * WINDOW, j * WINDOW)
        k_prev, v_prev = (kp_ref[0], vp_ref[0]) if j == 0 else (kc_ref[0, before], vc_ref[0, before])
        variant = jnp.where(first_of_sequence, 0, 1) if j == 0 else 1
        k_all = jnp.where(sink_row, zero, jnp.concatenate([k_prev, kc_ref[0, rows]], axis=0))
        v_all = jnp.where(sink_row, zero, jnp.concatenate([v_prev, vc_ref[0, rows]], axis=0))
        k_bd = jnp.concatenate([jnp.where(lane_head == kh, k_all, zero) for kh in kvh], axis=0)
        v_bd = jnp.concatenate([jnp.where(lane_head == kh, v_all, zero) for kh in kvh], axis=0)
        q_cat = jnp.concatenate([q_ref[0, rows, g * dk:(g + 1) * dk] for g in range(GROUP)],
                                axis=0)
        s = lax.dot_general(q_cat, k_bd, (((1,), (1,)), ((), ())), preferred_element_type=F32)
        es, invs = [], []
        for kh in kvh:
            sk = s[:, kh * 2 * WINDOW:(kh + 1) * 2 * WINDOW] + bias_ref[variant, kh]
            e = jnp.exp(sk - jnp.max(sk, axis=1, keepdims=True))
            invs.append(1.0 / jnp.sum(e, axis=1, keepdims=True))
            es.append(e.astype(BF16))
        pv = jnp.dot(jnp.concatenate(es, axis=1), v_bd, preferred_element_type=F32)
        inv = invs[KV_HEADS - 1]
        for kh in range(KV_HEADS - 1):
            inv = jnp.where(out_head == kh, invs[kh], inv)
        o = pv * inv
        for g in range(GROUP):
            o_ref[0, rows, g * dk:(g + 1) * dk] = o[g * WINDOW:(g + 1) * WINDOW].astype(o_ref.dtype)


def attn_prompt(q, k, v, table, batch, seq):
    d = Q_HEADS * HEAD_DIM
    dk = KV_HEADS * HEAD_DIM
    nq = ATTN_PROMPT_BLOCKS
    r3 = lambda a: a.reshape(batch, seq, a.shape[-1])
    prev = pl.BlockSpec((1, WINDOW, dk), lambda b, i: (b, jnp.maximum(i * nq - 1, 0), 0))
    cur = pl.BlockSpec((1, nq * WINDOW, dk), lambda b, i: (b, i, 0))
    table = table.reshape(KV_HEADS, GROUP * WINDOW, 2 * WINDOW)
    key = jnp.arange(2 * WINDOW)
    no_prev = jnp.where((key < WINDOW) & (key != PROMPT_SINK_KEY), NEG_INF, table)
    tables = jnp.stack([no_prev, table])
    out = pl.pallas_call(
        _attn_prompt_kernel,
        grid=(batch, seq // (nq * WINDOW)),
        in_specs=[pl.BlockSpec((1, nq * WINDOW, d), lambda b, i: (b, i, 0)),
                  prev, cur, prev, cur,
                  pl.BlockSpec(tables.shape, lambda b, i: (0, 0, 0, 0))],
        out_specs=pl.BlockSpec((1, nq * WINDOW, d), lambda b, i: (b, i, 0)),
        out_shape=jax.ShapeDtypeStruct((batch, seq, d), BF16),
        compiler_params=_cparams("parallel", "arbitrary"),
        name="attn_prompt",
    )(r3(q), r3(k), r3(k), r3(v), r3(v), tables)
    return out.reshape(batch * seq, d)


SAMPLE_NEW_KEYS = 16


def _attn_sample_kernel(q_ref, kct_ref, vct_ref, kn_ref, vn_ref, knt_ref, vnt_ref, bias_c_ref,
                        bias_n_ref, o_ref, *cache_out_refs, seq):
    n_b = q_ref.shape[0]
    dk = KV_HEADS * HEAD_DIM
    pairs = [(b, kh) for b in range(n_b) for kh in range(KV_HEADS)]
    kslice = lambda kh: slice(kh * HEAD_DIM, (kh + 1) * HEAD_DIM)
    head_lanes = lambda kh, g: slice((g * KV_HEADS + kh) * HEAD_DIM, (g * KV_HEADS + kh + 1) * HEAD_DIM)
    pad = jnp.zeros((SAMPLE_NEW_KEYS - seq, dk), F32)
    nt = (((1,), (1,)), ((), ()))
    dot = functools.partial(jnp.dot, preferred_element_type=F32)
    dot_nt = functools.partial(lax.dot_general, dimension_numbers=nt, preferred_element_type=F32)

    if cache_out_refs:
        lane = lax.broadcasted_iota(jnp.int32, (HEAD_DIM, WINDOW), 1)
        steps_per_block = knt_ref.shape[1] // (n_b * seq)
        first_token = (pl.program_id(0) % steps_per_block) * (n_b * seq)
        for out_ref, old_ref, newt_ref in zip(cache_out_refs, (kct_ref, vct_ref), (knt_ref, vnt_ref)):
            for b, kh in pairs:
                moved = pltpu.roll(old_ref[b, kh], WINDOW - seq, 1)
                shift = (2 * WINDOW - seq - first_token - b * seq) % WINDOW
                fresh = pltpu.roll(newt_ref[kslice(kh), :], shift, 1)
                out_ref[b, kh] = jnp.where(lane >= WINDOW - seq, fresh, moved)

    qss = [jnp.concatenate([q_ref[b, :, head_lanes(kh, g)] for g in range(GROUP)],
                           axis=0).astype(BF16) for b, kh in pairs]
    kcts = [kct_ref[b, kh].astype(BF16) for b, kh in pairs]
    vcts = [vct_ref[b, kh].astype(BF16) for b, kh in pairs]
    kns = [jnp.concatenate([kn_ref[b], pad], axis=0).astype(BF16) for b in range(n_b)]
    vns = [jnp.concatenate([vn_ref[b], pad], axis=0).astype(BF16) for b in range(n_b)]
    s_cs = [dot(qss[i], kcts[i]) for i in range(len(pairs))]
    s_ns = [dot_nt(qss[i], kns[b][:, kslice(kh)]) for i, (b, kh) in enumerate(pairs)]
    e_cs, e_ns, denoms = [], [], []
    for i, (b, kh) in enumerate(pairs):
        s_c = s_cs[i] + bias_c_ref[kh]
        s_n = s_ns[i] + bias_n_ref[kh]
        m = jnp.maximum(jnp.max(s_c, axis=1, keepdims=True), jnp.max(s_n, axis=1, keepdims=True))
        e_c, e_n = jnp.exp(s_c - m), jnp.exp(s_n - m)
        denoms.append(jnp.sum(e_c, axis=1, keepdims=True) + jnp.sum(e_n, axis=1, keepdims=True))
        e_cs.append(e_c.astype(BF16))
        e_ns.append(e_n.astype(BF16))
    pv_cs = [dot_nt(e_cs[i], vcts[i]) for i in range(len(pairs))]
    pv_ns = [dot(e_ns[i], vns[b][:, kslice(kh)]) for i, (b, kh) in enumerate(pairs)]
    for i, (b, kh) in enumerate(pairs):
        o = (pv_cs[i] + pv_ns[i]) / denoms[i]
        for g in range(GROUP):
            o_ref[b, :, head_lanes(kh, g)] = o[g * seq:(g + 1) * seq].astype(o_ref.dtype)


def attn_sample(q, k_new, v_new, cache_k, cache_v, table, batch, seq, write_cache):
    d = Q_HEADS * HEAD_DIM
    dk = KV_HEADS * HEAD_DIM
    nbk = ATTN_SAMPLE_BLOCK
    assert WINDOW % (nbk * seq) == 0
    new = pl.BlockSpec((nbk, seq, dk), lambda i: (i, 0, 0))
    old = pl.BlockSpec((nbk, KV_HEADS, HEAD_DIM, WINDOW), lambda i: (i, 0, 0, 0))
    newt = pl.BlockSpec((dk, WINDOW), lambda i: (0, i * nbk * seq // WINDOW))
    to_t = lambda c: jnp.transpose(c, (0, 2, 3, 1))
    table = table.reshape(KV_HEADS, GROUP * seq, 2 * WINDOW)
    n_cache = 2 if write_cache else 0
    out = pl.pallas_call(
        functools.partial(_attn_sample_kernel, seq=seq),
        grid=(batch // nbk,),
        in_specs=[pl.BlockSpec((nbk, seq, d), lambda i: (i, 0, 0)),
                  old, old, new, new, newt, newt,
                  pl.BlockSpec((KV_HEADS, GROUP * seq, WINDOW), lambda i: (0, 0, 0)),
                  pl.BlockSpec((KV_HEADS, GROUP * seq, SAMPLE_NEW_KEYS), lambda i: (0, 0, 0))],
        out_specs=[pl.BlockSpec((nbk, seq, d), lambda i: (i, 0, 0))] + [old] * n_cache,
        out_shape=[jax.ShapeDtypeStruct((batch, seq, d), F32)]
                  + [jax.ShapeDtypeStruct((batch, KV_HEADS, HEAD_DIM, WINDOW), F32)] * n_cache,
        compiler_params=_cparams("parallel"),
        name="attn_sample",
    )(q.reshape(batch, seq, d), to_t(cache_k), to_t(cache_v),
      k_new.reshape(batch, seq, dk), v_new.reshape(batch, seq, dk), k_new.T, v_new.T,
      table[:, :, :WINDOW], table[:, :, WINDOW:WINDOW + SAMPLE_NEW_KEYS])
    return (out[0].reshape(batch * seq, d),) + tuple(jnp.transpose(c, (0, 3, 1, 2)) for c in out[1:])


def _trunk(x, state, cache, w, tm, tm_mlp):
    batch, seq, d = x.shape
    x = x.reshape(batch * seq, d)
    hq, hv = N_HEADS * DQK, N_HEADS * DV
    depth = w["norm_mix"].shape[0]
    n_a = w["w_in"].shape[0]
    prompt = state is None
    act = BF16 if prompt else F32
    cs, ns, ms = [], [], []
    c_stack = None if prompt else state[0]
    for l in range(depth):
        if l < n_a:
            gates = (w["b_igate"][l], w["b_fgate"][l], w["mlstm_norm"][l])
            q, k, v, o, gate = mlstm_inproj(x, w["norm_mix"], w["w_in"], l, tm, prompt, act)
            if prompt:
                a, c_new, n_new, m_new = mlstm_prompt(q, k, v, o, gate, *gates, batch, seq)
                cs.append(c_new)
            else:
                a, c_stack, n_new, m_new = mlstm_sample(q, k, v, o, gate, *gates, state[0], c_stack,
                                                        l, state[1][l], state[2][l], batch, seq)
            ns.append(n_new); ms.append(m_new)
            w_o, lo = w["w_mlstm_out"], l
        else:
            j = l - n_a
            q_proj = (w["norm_mix"], l, w["w_q"], j, ((0, Q_HEADS * HEAD_DIM),), (act,))
            if j == 0:
                dk = KV_HEADS * HEAD_DIM
                kv_proj = (w["kv_norm"], 0, w["w_kv"], 0, ((0, dk), (dk, dk), (0, dk), (dk, dk)),
                           (F32, F32, BF16, BF16))
                (k_new, v_new, k16, v16), (q,) = norm_matmul(x, (kv_proj, q_proj), tm_mlp)
                tables = bias_table(w["rel_bias"], w["attn_sinks"], WINDOW if prompt else seq,
                                    2 * WINDOW, PROMPT_SINK_KEY if prompt else WINDOW + seq)
            else:
                ((q,),) = norm_matmul(x, (q_proj,), tm_mlp)
            if prompt:
                a = attn_prompt(q, k16, v16, tables[j], batch, seq)
            elif j == 0:
                a, win_k, win_v = attn_sample(q, k_new, v_new, cache[0], cache[1], tables[j],
                                              batch, seq, True)
            else:
                (a,) = attn_sample(q, k_new, v_new, cache[0], cache[1], tables[j],
                                   batch, seq, False)
            w_o, lo = w["w_attn_out"], j
        x = mlp(a, w_o, lo, x, w["norm_ffn"], w["w_up"], w["w_down"], l, w["final_norm"],
                l == depth - 1, tm_mlp, MLP_FF_BLOCK)
    if prompt:
        dk = KV_HEADS * HEAD_DIM
        win_k = k_new.reshape(batch, seq, dk)[:, -WINDOW:]
        win_v = v_new.reshape(batch, seq, dk)[:, -WINDOW:]
        c_stack = jnp.stack(cs)
    shp = (batch, WINDOW, KV_HEADS, HEAD_DIM)
    return (x.reshape(batch, seq, d), c_stack, jnp.stack(ns), jnp.stack(ms),
            win_k.reshape(shp), win_v.reshape(shp))


def kernel(x_prompt, x_sample, state_C, state_n, state_m, cache_k, cache_v, norm_mix, norm_ffn,
           w_mlstm_in, b_igate, b_fgate, mlstm_norm, w_mlstm_out, kv_norm, w_kv, w_q, attn_sinks,
           w_attn_out, rel_bias, w_up, w_down, final_norm):
    n_b, d = w_q.shape[0], w_q.shape[1]
    heads = (KV_HEADS, GROUP, HEAD_DIM)
    w_q_perm = (w_q * HEAD_DIM ** -0.5).astype(BF16).reshape((n_b, d) + heads)
    w_q_perm = w_q_perm.transpose(0, 1, 3, 2, 4).reshape(n_b, d, d)
    w_ao_perm = w_attn_out.astype(BF16).reshape((n_b,) + heads + (d,))
    w_ao_perm = w_ao_perm.transpose(0, 2, 1, 3, 4).reshape(n_b, d, d)
    w = dict(norm_mix=norm_mix[:, None, :], norm_ffn=norm_ffn[:, None, :],
             w_in=w_mlstm_in.astype(BF16), b_igate=b_igate, b_fgate=b_fgate,
             mlstm_norm=mlstm_norm, w_mlstm_out=w_mlstm_out.astype(BF16),
             kv_norm=kv_norm[None, None, :], w_kv=w_kv.astype(BF16)[None],
             w_q=w_q_perm, attn_sinks=attn_sinks, w_attn_out=w_ao_perm, rel_bias=rel_bias,
             w_up=w_up.astype(BF16), w_down=w_down.astype(BF16), final_norm=final_norm)
    y_p, c_p, n_p, m_p, k_p, v_p = _trunk(x_prompt, None, None, w, 512, 1024)
    y_s, c_s, n_s, m_s, k_s, v_s = _trunk(x_sample, (state_C, state_n, state_m),
                                          (cache_k, cache_v), w, 512, 512)
    return (y_p, y_s, c_p, n_p, m_p, k_p, v_p, c_s, n_s, m_s, k_s, v_s)
```

```python
import functools
import math

import jax
import jax.numpy as jnp
from jax import lax
from jax.experimental import pallas as pl
from jax.experimental.pallas import tpu as pltpu

F32 = jnp.float32
BF16 = jnp.bfloat16
EPS = 1e-6
NEG_INF = float("-inf")

N_HEADS = 8
DQK = 64
DV = 128
Q_HEADS = 16
KV_HEADS = 4
GROUP = Q_HEADS // KV_HEADS
HEAD_DIM = 64
WINDOW = 128
N_BUCKETS = 32
MAX_EXACT = N_BUCKETS // 2
MAX_DISTANCE = 128

LANES = 128
VMEM_LIMIT = 56 * 1024 * 1024

PROMPT_CHUNK = 128
PROMPT_BLOCK = 512
SAMPLE_BATCH_BLOCK = 8
ATTN_SAMPLE_BLOCK = 8
ATTN_PROMPT_BLOCKS = 8
MLP_ROW_CHUNK = 512
MLP_FF_BLOCK = 2048
MLP_FF_CHUNK = 1024


def _cparams(*sem):
    return pltpu.CompilerParams(dimension_semantics=sem, vmem_limit_bytes=VMEM_LIMIT)


def _rms(x, g):
    return x * lax.rsqrt(jnp.mean(x * x, axis=-1, keepdims=True) + EPS) * g


def _split3(x):
    hi = x.astype(BF16)
    r = x - hi.astype(F32)
    mid = r.astype(BF16)
    lo = (r - mid.astype(F32)).astype(BF16)
    return hi, mid, lo


def _dot01(x, onehot):
    hi, mid, lo = _split3(x)
    d = lambda a: jnp.dot(a, onehot, preferred_element_type=F32)
    return d(hi) + d(mid) + d(lo)


def _dot01_left(onehot, x):
    hi, mid, lo = _split3(x)
    d = lambda a: jnp.dot(onehot, a, preferred_element_type=F32)
    return d(hi) + d(mid) + d(lo)


def _log_sigmoid(x):
    return jnp.minimum(x, 0.0) - jnp.log1p(jnp.exp(-jnp.abs(x)))


def _sigmoid(x):
    return 1.0 / (1.0 + jnp.exp(-x))


def _norm_mm_kernel(x_ref, *refs, splits):
    n_proj = len(splits)
    out_refs = iter(refs[2 * n_proj:])
    x = x_ref[...]
    xhat = x * lax.rsqrt(jnp.mean(x * x, axis=-1, keepdims=True) + EPS)
    ys = [(xhat * refs[2 * p][...]).astype(BF16) for p in range(n_proj)]
    rs = [jnp.dot(ys[p], refs[2 * p + 1][...], preferred_element_type=F32) for p in range(n_proj)]
    for p in range(n_proj):
        for off, n in splits[p]:
            o_ref = next(out_refs)
            o_ref[...] = rs[p][:, off:off + n].astype(o_ref.dtype)


def _layer_spec(arr, layer):
    idx = (layer,) + (0,) * (arr.ndim - 1)
    return pl.BlockSpec((None,) + arr.shape[1:], lambda *_: idx)


def norm_matmul(x, projections, tm):
    m, d = x.shape
    assert m % tm == 0
    params, param_specs, out_specs, out_shapes = [], [], [], []
    for g, lg, w, lw, splits, dtypes in projections:
        assert all(off + n <= w.shape[-1] for off, n in splits)
        params += [g, w]
        param_specs += [_layer_spec(g, lg), _layer_spec(w, lw)]
        out_specs += [pl.BlockSpec((tm, n), lambda i: (i, 0)) for _, n in splits]
        out_shapes += [jax.ShapeDtypeStruct((m, n), dt) for (_, n), dt in zip(splits, dtypes)]
    outs = pl.pallas_call(
        functools.partial(_norm_mm_kernel, splits=tuple(p[4] for p in projections)),
        grid=(m // tm,),
        in_specs=[pl.BlockSpec((tm, d), lambda i: (i, 0))] + param_specs,
        out_specs=out_specs,
        out_shape=out_shapes,
        compiler_params=_cparams("parallel"),
        name="norm_matmul",
    )(x, *params)
    outs, grouped = list(outs), []
    for p in projections:
        grouped.append(outs[:len(p[4])])
        outs = outs[len(p[4]):]
    return grouped


def _mlp_kernel(a_ref, wo_ref, x_ref, g_ref, wu_ref, wd_ref, gf_ref, o_ref, xn_ref, *,
                final_norm, row_chunk, ff_chunk):
    j = pl.program_id(1)
    tm = x_ref.shape[0]

    @pl.when(j == 0)
    def _():
        for r in range(0, tm, row_chunk):
            rows = slice(r, r + row_chunk)
            x = x_ref[rows, :] + jnp.dot(a_ref[rows, :].astype(BF16), wo_ref[...],
                                         preferred_element_type=F32)
            xn_ref[rows, :] = _rms(x, g_ref[...]).astype(BF16)
            o_ref[rows, :] = x

    for c in range(0, wu_ref.shape[1], ff_chunk):
        h = jnp.dot(xn_ref[...], wu_ref[:, c:c + ff_chunk], preferred_element_type=F32)
        h = jnp.maximum(h, 0.0)
        o_ref[...] += jnp.dot((h * h).astype(BF16), wd_ref[c:c + ff_chunk, :],
                              preferred_element_type=F32)

    if final_norm:
        @pl.when(j == pl.num_programs(1) - 1)
        def _():
            o_ref[...] = _rms(o_ref[...], gf_ref[...])


def mlp(a, w_o, lo, x, g, w_up, w_down, l, g_final, final_norm, tm, tf):
    m, d = x.shape
    ff = w_up.shape[-1]
    return pl.pallas_call(
        functools.partial(_mlp_kernel, final_norm=final_norm, row_chunk=min(tm, MLP_ROW_CHUNK),
                          ff_chunk=MLP_FF_CHUNK),
        grid=(m // tm, ff // tf),
        in_specs=[pl.BlockSpec((tm, d), lambda i, j: (i, 0)),
                  _layer_spec(w_o, lo),
                  pl.BlockSpec((tm, d), lambda i, j: (i, 0)),
                  _layer_spec(g, l),
                  pl.BlockSpec((None, d, tf), lambda i, j: (l, 0, j)),
                  pl.BlockSpec((None, tf, d), lambda i, j: (l, j, 0)),
                  pl.BlockSpec((1, d), lambda i, j: (0, 0))],
        out_specs=pl.BlockSpec((tm, d), lambda i, j: (i, 0)),
        out_shape=jax.ShapeDtypeStruct((m, d), F32),
        scratch_shapes=[pltpu.VMEM((tm, d), BF16)],
        compiler_params=_cparams("parallel", "arbitrary"),
        name="mlp",
    )(a, w_o, x, g, w_up, w_down, g_final.reshape(1, d))


def _mlstm_inproj_kernel(x_ref, g_ref, w_ref, q_ref, k_ref, v_ref, o_ref, gate_ref, *, k_transposed):
    hq, hv = N_HEADS * DQK, N_HEADS * DV
    tm = x_ref.shape[0]
    y = _rms(x_ref[...], g_ref[...]).astype(BF16)
    r = jnp.dot(y, w_ref[...], preferred_element_type=F32)
    q_ref[...] = r[:, :hq].astype(q_ref.dtype)
    k = r[:, hq:2 * hq] * DQK ** -0.5
    k_ref[...] = (k.T if k_transposed else k).astype(k_ref.dtype)
    v_ref[...] = r[:, 2 * hq:2 * hq + hv].astype(v_ref.dtype)
    o_ref[...] = r[:, 2 * hq + hv:2 * hq + 2 * hv]
    gate_ref[...] = jnp.concatenate(
        [r[:, 2 * hq + 2 * hv:], jnp.zeros((tm, LANES - 2 * N_HEADS), F32)], axis=1)


def mlstm_inproj(x, g, w, l, tm, k_transposed, act):
    m, d = x.shape
    hq, hv = N_HEADS * DQK, N_HEADS * DV
    assert w.shape[-1] == 2 * hq + 2 * hv + 2 * N_HEADS
    row = lambda width: pl.BlockSpec((tm, width), lambda i: (i, 0))
    k_spec = pl.BlockSpec((hq, tm), lambda i: (0, i)) if k_transposed else row(hq)
    k_shape = (hq, m) if k_transposed else (m, hq)
    return pl.pallas_call(
        functools.partial(_mlstm_inproj_kernel, k_transposed=k_transposed),
        grid=(m // tm,),
        in_specs=[row(d), _layer_spec(g, l), _layer_spec(w, l)],
        out_specs=[row(hq), k_spec, row(hv), row(hv), row(LANES)],
        out_shape=[jax.ShapeDtypeStruct((m, hq), act), jax.ShapeDtypeStruct(k_shape, act),
                   jax.ShapeDtypeStruct((m, hv), act), jax.ShapeDtypeStruct((m, hv), F32),
                   jax.ShapeDtypeStruct((m, LANES), F32)],
        compiler_params=_cparams("parallel"),
        name="mlstm_inproj",
    )(x, g, w)


def _chunk_scan(x, pos, op, fill, length):
    k = 1
    while k < length:
        x = op(x, jnp.where(pos >= k, pltpu.roll(x, k, 0), fill))
        k *= 2
    return x


def _mlstm_prompt_kernel(q_ref, kt_ref, v_ref, o_ref, gate_ref, bi_ref, bf_ref, gain_ref,
                         hg_ref, cx_out_ref, m_out_ref, cx_s, m_s, *, chunk):
    t_blk = q_ref.shape[1]
    n_chunks = t_blk // chunk
    step = pl.program_id(1)

    @pl.when(step == 0)
    def _():
        cx_s[...] = jnp.zeros_like(cx_s)
        m_s[...] = jnp.zeros_like(m_s)

    gates = gate_ref[0]
    li = gates + bi_ref[...]
    lf = _log_sigmoid(pltpu.roll(gates, LANES - N_HEADS, 1) + bf_ref[...])
    pos = lax.broadcasted_iota(jnp.int32, (t_blk, 1), 0) % chunk
    b = _chunk_scan(lf, pos, jnp.add, 0.0, chunk)
    c = li - b
    cm = _chunk_scan(c, pos, jnp.maximum, NEG_INF, chunk)

    m_prev = m_s[...]
    xs, w_inters, e_negms, wks = [], [], [], []
    for ck in range(n_chunks):
        sl = slice(ck * chunk, (ck + 1) * chunk)
        m_t = b[sl] + jnp.maximum(m_prev, cm[sl])
        m_new = m_t[chunk - 1:chunk]
        b_last = b[(ck + 1) * chunk - 1:(ck + 1) * chunk]
        xs.append(b[sl] - m_t)
        w_inters.append(jnp.exp(b[sl] + m_prev - m_t))
        e_negms.append(jnp.exp(-m_t))
        wks.append(jnp.exp(c[sl] + (b_last - m_new)))
        m_prev = m_new
    m_s[...] = m_prev
    c_t = c.T
    wk_t = jnp.concatenate(wks, axis=0).T

    ri = lax.broadcasted_iota(jnp.int32, (chunk, chunk), 0)
    ci = lax.broadcasted_iota(jnp.int32, (chunk, chunk), 1)
    causal = ci <= ri
    ones_blk = jnp.ones((chunk, DV), BF16)
    ones_sq = jnp.ones((DV, DV), BF16)
    heads = range(N_HEADS)
    dot = functools.partial(jnp.dot, preferred_element_type=F32)

    cxs = [cx_s[h] for h in heads]
    for ck in range(n_chunks):
        sl = slice(ck * chunk, (ck + 1) * chunk)
        qs = [q_ref[0, sl, h * DQK:(h + 1) * DQK] for h in heads]
        kts = [kt_ref[h * DQK:(h + 1) * DQK, sl] for h in heads]
        vs = [v_ref[0, sl, h * DV:(h + 1) * DV] for h in heads]
        cx16s = [cxs[h].astype(BF16) for h in heads]
        kws = [(kts[h].astype(F32) * wk_t[h:h + 1, sl]).astype(BF16) for h in heads]
        vones = [jnp.concatenate([vs[h], ones_blk], axis=1) for h in heads]
        ss = [dot(qs[h], kts[h]) for h in heads]
        qcs = [dot(qs[h], cx16s[h]) for h in heads]
        dcs = [dot(kws[h], vones[h]) for h in heads]
        cxs = [w_inters[ck][chunk - 1:chunk, h:h + 1] * cxs[h] + dcs[h] for h in heads]
        a16s = []
        for h in heads:
            dm = xs[ck][:, h:h + 1] + c_t[h:h + 1, sl]
            a16s.append((ss[h] * jnp.exp(jnp.where(causal, dm, NEG_INF))).astype(BF16))
        avs = [dot(a16s[h], vones[h]) for h in heads]
        hhs = []
        for h in heads:
            wi = jnp.broadcast_to(w_inters[ck][:, h:h + 1], (chunk, DV))
            en = jnp.broadcast_to(e_negms[ck][:, h:h + 1], (chunk, DV))
            den = avs[h][:, DV:] + wi * qcs[h][:, DV:]
            inv = 1.0 / jnp.maximum(jnp.abs(den), en)
            hhs.append((avs[h][:, :DV] + qcs[h][:, :DV] * wi) * inv)
        sq16s = [(hhs[h] * hhs[h]).astype(BF16) for h in heads]
        mss = [dot(sq16s[h], ones_sq) for h in heads]
        for h in heads:
            hn = hhs[h] * lax.rsqrt(mss[h] * (1.0 / DV) + EPS)
            hn = hn * gain_ref[:, h * DV:(h + 1) * DV]
            og = _sigmoid(o_ref[0, sl, h * DV:(h + 1) * DV])
            hg_ref[0, sl, h * DV:(h + 1) * DV] = (og * hn).astype(hg_ref.dtype)
    for h in heads:
        cx_s[h] = cxs[h]

    @pl.when(step == pl.num_programs(1) - 1)
    def _():
        cx_out_ref[0] = cx_s[...]
        m_out_ref[0] = m_s[...]


def mlstm_prompt(q, kt, v, o, gate, b_i, b_f, gain, batch, seq):
    hq, hv = N_HEADS * DQK, N_HEADS * DV
    t = PROMPT_BLOCK
    nblk = seq // t
    r3 = lambda a: a.reshape(batch, seq, a.shape[-1])
    pad8 = lambda a: jnp.pad(a.reshape(1, N_HEADS), ((0, 0), (0, LANES - N_HEADS)))
    tok = lambda w: pl.BlockSpec((1, t, w), lambda bb, s: (bb, s, 0))
    cst = lambda w: pl.BlockSpec((1, w), lambda bb, s: (0, 0))
    hg, cx, m_new = pl.pallas_call(
        functools.partial(_mlstm_prompt_kernel, chunk=PROMPT_CHUNK),
        grid=(batch, nblk),
        in_specs=[tok(hq), pl.BlockSpec((hq, t), lambda bb, s: (0, bb * nblk + s)),
                  tok(hv), tok(hv), tok(LANES), cst(LANES), cst(LANES), cst(hv)],
        out_specs=[tok(hv),
                   pl.BlockSpec((1, N_HEADS, DQK, 2 * DV), lambda bb, s: (bb, 0, 0, 0)),
                   pl.BlockSpec((1, 1, LANES), lambda bb, s: (bb, 0, 0))],
        out_shape=[jax.ShapeDtypeStruct((batch, seq, hv), BF16),
                   jax.ShapeDtypeStruct((batch, N_HEADS, DQK, 2 * DV), F32),
                   jax.ShapeDtypeStruct((batch, 1, LANES), F32)],
        scratch_shapes=[pltpu.VMEM((N_HEADS, DQK, 2 * DV), F32), pltpu.VMEM((1, LANES), F32)],
        compiler_params=_cparams("parallel", "arbitrary"),
        name="mlstm_prompt",
    )(r3(q), kt, r3(v), r3(o), r3(gate), pad8(b_i), pad8(b_f), gain.reshape(1, hv))
    return hg.reshape(batch * seq, hv), cx[..., :DV], cx[..., DV], m_new[:, 0, :N_HEADS]


def _mlstm_sample_kernel(q_ref, k_ref, v_ref, o_ref, gate_ref, m0_ref, n0_ref, c0_ref, c_carry_ref,
                         bi_ref, bf_ref, gain_ref, seg64_ref, seg128_ref, e64_ref, e128_ref,
                         hg_ref, c_out_ref, n_out_ref, m_out_ref,
                         qc_s, *, seq):
    del c_carry_ref
    spans_layers = len(c_out_ref.shape) == 5
    c_new_ref = c_out_ref.at[0] if spans_layers else c_out_ref
    if spans_layers:
        for later in range(1, c_out_ref.shape[0]):
            c_out_ref[later] = jnp.zeros(c_out_ref.shape[1:], F32)
    rows = q_ref.shape[0]
    n_b = rows // seq
    tpos = lax.broadcasted_iota(jnp.int32, (rows, 1), 0) % seq
    dot = functools.partial(jnp.dot, preferred_element_type=F32)

    def shift(x, d):
        return x if d == 0 else pltpu.roll(x, d, 0)

    def unshift(x, d):
        return x if d == 0 else pltpu.roll(x, rows - d, 0)

    def split_cat(x):
        return jnp.concatenate(_split3(x), axis=1)


    q = q_ref[...].astype(BF16).astype(F32)
    k = k_ref[...].astype(BF16).astype(F32)
    v = v_ref[...].astype(BF16).astype(F32)

    units = [(p, h) for p in range(n_b // 2) for h in range(N_HEADS)]
    prow = lambda p: slice(p * 2 * seq, (p + 1) * 2 * seq)
    q_pairs = [q[prow(p), h * DQK:(h + 1) * DQK].astype(BF16) for p, h in units]
    c_cats = [jnp.concatenate([c0_ref[2 * p, h], c0_ref[2 * p + 1, h]], axis=1).astype(BF16)
              for p, h in units]
    qcs = [dot(q_pairs[i], c_cats[i]) for i in range(len(units))]

    gates = gate_ref[...]
    li = gates + bi_ref[...]
    lf = _log_sigmoid(pltpu.roll(gates, LANES - N_HEADS, 1) + bf_ref[...])
    b = lf
    for d in range(1, seq):
        b = b + jnp.where(tpos >= d, shift(lf, d), 0.0)
    inter = b + m0_ref[...]
    dvals = []
    m_t = inter
    for d in range(seq):
        dd = jnp.where(tpos >= d, b - shift(b, d) + shift(li, d), NEG_INF)
        dvals.append(dd)
        m_t = jnp.maximum(m_t, dd)
    w_inter = jnp.exp(inter - m_t)
    w_intra = [jnp.exp(dd - m_t) for dd in dvals]
    last = lambda x: functools.reduce(
        lambda acc, d: jnp.where(tpos == seq - 1 - d, unshift(x, d), acc), range(1, seq), x)
    wk = jnp.exp(last(b) - b + li - last(m_t))

    qk_in = [split_cat(q * shift(k, d)) for d in range(seq)]
    qn_in = split_cat(q * n0_ref[...].astype(BF16).astype(F32))
    wk_in, wi_in = split_cat(wk), split_cat(w_inter)
    seg64, e64, e128 = seg64_ref[...], e64_ref[...], e128_ref[...]
    qks = [dot(x, seg64) for x in qk_in]
    qn = dot(qn_in, seg64)
    wk_exp = dot(wk_in, e64)
    decay64 = dot(wi_in, e64)
    decay_exp = dot(wi_in, e128)

    a_s = [qks[d] * w_intra[d] for d in range(seq)]
    e128_1 = e128[:LANES]
    a_exps = [dot(a.astype(BF16), e128_1) for a in a_s]
    den = w_inter * qn
    num = jnp.zeros((rows, N_HEADS * DV), F32)
    for d in range(seq):
        den = den + a_s[d]
        num = num + a_exps[d] * shift(v, d)

    kw = k * wk_exp
    ksum = kw
    for d in range(1, seq):
        ksum = ksum + shift(kw, d)
    n_rows = decay64 * n0_ref[...] + ksum
    own = ((lax.broadcasted_iota(jnp.int32, (2 * seq, 2 * DV), 0) < seq)
           == (lax.broadcasted_iota(jnp.int32, (2 * seq, 2 * DV), 1) < DV))
    kw_pairs = [kw[prow(p), h * DQK:(h + 1) * DQK].astype(BF16) for p, h in units]
    v2s = []
    for p, h in units:
        v_pair = v[prow(p), h * DV:(h + 1) * DV]
        v2s.append(jnp.where(own, jnp.concatenate([v_pair, v_pair], axis=1), 0.0).astype(BF16))
    sel3 = (lax.broadcasted_iota(jnp.int32, (n_b, 3 * rows), 1) % rows
            == lax.broadcasted_iota(jnp.int32, (n_b, 3 * rows), 0) * seq + (seq - 1))
    sel3 = jnp.where(sel3, 1.0, 0.0).astype(BF16)
    n_in = jnp.concatenate(_split3(n_rows), axis=0)
    m_in = jnp.concatenate(_split3(m_t), axis=0)
    d_cs = [lax.dot_general(kw_pairs[i], v2s[i], (((0,), (0,)), ((), ())),
                            preferred_element_type=F32) for i in range(len(units))]
    n_out_ref[...] = dot(sel3, n_in)
    m_out_ref[...] = dot(sel3, m_in)

    first = lax.broadcasted_iota(jnp.int32, (2 * seq, DV), 0) < seq
    for i, (p, h) in enumerate(units):
        r0 = p * 2 * seq
        qc_s[prow(p), h * DV:(h + 1) * DV] = jnp.where(first, qcs[i][:, :DV], qcs[i][:, DV:])
        dec_a = decay_exp[r0 + seq - 1:r0 + seq, h * DV:(h + 1) * DV]
        dec_b = decay_exp[r0 + 2 * seq - 1:r0 + 2 * seq, h * DV:(h + 1) * DV]
        c_new_ref[2 * p, h] = dec_a * c0_ref[2 * p, h] + d_cs[i][:, :DV]
        c_new_ref[2 * p + 1, h] = dec_b * c0_ref[2 * p + 1, h] + d_cs[i][:, DV:]

    num = num + qc_s[...] * decay_exp
    inv_scale = 1.0 / jnp.maximum(jnp.abs(den), jnp.exp(-m_t))
    hh = num * dot(split_cat(inv_scale), e128)
    ms = dot(split_cat(hh * hh), seg128_ref[...]) * (1.0 / DV)
    hn = hh * dot(split_cat(lax.rsqrt(ms + EPS)), e128) * gain_ref[...]
    hg_ref[...] = (_sigmoid(o_ref[...]) * hn).astype(hg_ref.dtype)


def mlstm_sample(q, k, v, o, gate, b_i, b_f, gain, state_c, c_carry, l, n0, m0, batch, seq):
    hq, hv = N_HEADS * DQK, N_HEADS * DV
    c_blk = pl.BlockSpec((None, SAMPLE_BATCH_BLOCK, N_HEADS, DQK, DV), lambda i: (l, i, 0, 0, 0))
    c_all = pl.BlockSpec((state_c.shape[0], SAMPLE_BATCH_BLOCK, N_HEADS, DQK, DV),
                         lambda i: (0, i, 0, 0, 0))
    rows = SAMPLE_BATCH_BLOCK * seq
    pad8 = lambda a: jnp.pad(a.reshape(1, N_HEADS), ((0, 0), (0, LANES - N_HEADS)))
    m0_rows = jnp.pad(jnp.repeat(m0, seq, axis=0), ((0, 0), (0, LANES - N_HEADS)))
    n0_rows = jnp.repeat(n0.reshape(batch, hq), seq, axis=0)
    lane = jnp.arange(LANES)
    seg64 = (jnp.arange(hq)[:, None] // DQK == lane[None, :]).astype(BF16)
    seg128 = (jnp.arange(hv)[:, None] // DV == lane[None, :]).astype(BF16)
    tok = lambda w: pl.BlockSpec((rows, w), lambda i: (i, 0))
    cst = lambda a: pl.BlockSpec(a.shape, lambda i: (0,) * a.ndim)
    x3 = lambda a: jnp.tile(a, (3, 1))
    consts = [pad8(b_i), pad8(b_f), gain.reshape(1, hv),
              x3(seg64), x3(seg128), x3(seg64.T), x3(seg128.T)]
    hg, c_new, n_new, m_new = pl.pallas_call(
        functools.partial(_mlstm_sample_kernel, seq=seq),
        grid=(batch // SAMPLE_BATCH_BLOCK,),
        in_specs=[tok(hq), tok(hq), tok(hv), tok(hv), tok(LANES), tok(LANES), tok(hq),
                  c_blk, pl.BlockSpec(memory_space=pl.ANY)]
                 + [cst(a) for a in consts],
        out_specs=[tok(hv), c_all if l == 0 else c_blk,
                   pl.BlockSpec((SAMPLE_BATCH_BLOCK, hq), lambda i: (i, 0)),
                   pl.BlockSpec((SAMPLE_BATCH_BLOCK, LANES), lambda i: (i, 0))],
        out_shape=[jax.ShapeDtypeStruct((batch * seq, hv), BF16),
                   jax.ShapeDtypeStruct(state_c.shape, F32),
                   jax.ShapeDtypeStruct((batch, hq), F32),
                   jax.ShapeDtypeStruct((batch, LANES), F32)],
        scratch_shapes=[pltpu.VMEM((rows, hv), F32)],
        input_output_aliases={8: 1} if l > 0 else {},
        compiler_params=_cparams("parallel"),
        name="mlstm_sample",
    )(q, k, v, o, gate, m0_rows, n0_rows, state_c, c_carry, *consts)
    return hg, c_new, n_new.reshape(batch, N_HEADS, DQK), m_new[:, :N_HEADS]


def _bias_table_kernel(rb_ref, sink_ref, o_ref, *, sink_col):
    n_layers, _, nq, ns = o_ref.shape
    col = lax.broadcasted_iota(jnp.int32, (nq, ns), 1)
    dist = lax.broadcasted_iota(jnp.int32, (nq, ns), 0) + WINDOW - col
    n = jnp.maximum(dist, 0)
    large = MAX_EXACT + (jnp.log(jnp.maximum(n, 1).astype(F32) / MAX_EXACT)
                         / math.log(MAX_DISTANCE / MAX_EXACT) * (N_BUCKETS - MAX_EXACT)).astype(jnp.int32)
    large = jnp.minimum(large, N_BUCKETS - 1)
    bucket = jnp.where(n < MAX_EXACT, n, large)
    valid = (dist >= 0) & (dist < WINDOW)
    for h in range(Q_HEADS):
        acc = jnp.zeros((nq, ns), F32)
        for bkt in range(N_BUCKETS):
            acc = jnp.where(bucket == bkt, rb_ref[bkt, h], acc)
        acc = jnp.where(valid, acc, NEG_INF)
        for layer in range(n_layers):
            o_ref[layer, h] = jnp.where(col == sink_col, sink_ref[layer, h], acc)


def bias_table(rel_bias, sinks, nq, ns, sink_col):
    return pl.pallas_call(
        functools.partial(_bias_table_kernel, sink_col=sink_col),
        in_specs=[pl.BlockSpec(memory_space=pltpu.SMEM), pl.BlockSpec(memory_space=pltpu.SMEM)],
        out_specs=pl.BlockSpec(memory_space=pltpu.VMEM),
        out_shape=jax.ShapeDtypeStruct((sinks.shape[0], Q_HEADS, nq, ns), F32),
        name="bias_table",
    )(rel_bias, sinks)


PROMPT_SINK_KEY = 0


def _attn_prompt_kernel(q_ref, kp_ref, kc_ref, vp_ref, vc_ref, bias_ref, o_ref):
    dk = KV_HEADS * HEAD_DIM
    kvh = range(KV_HEADS)
    sink_row = lax.broadcasted_iota(jnp.int32, (2 * WINDOW, dk), 0) == PROMPT_SINK_KEY
    lane_head = lax.broadcasted_iota(jnp.int32, (2 * WINDOW, dk), 1) // HEAD_DIM
    out_head = lax.broadcasted_iota(jnp.int32, (GROUP * WINDOW, dk), 1) // HEAD_DIM
    zero = jnp.zeros((), BF16)
    first_of_sequence = pl.program_id(1) == 0
    for j in range(q_ref.shape[1] // WINDOW):
        rows = slice(j * WINDOW, (j + 1) * WINDOW)
        before = slice((j - 1) * WINDOW, j * WINDOW)
        k_prev, v_prev = (kp_ref[0], vp_ref[0]) if j == 0 else (kc_ref[0, before], vc_ref[0, before])
        variant = jnp.where(first_of_sequence, 0, 1) if j == 0 else 1
        k_all = jnp.where(sink_row, zero, jnp.concatenate([k_prev, kc_ref[0, rows]], axis=0))
        v_all = jnp.where(sink_row, zero, jnp.concatenate([v_prev, vc_ref[0, rows]], axis=0))
        k_bd = jnp.concatenate([jnp.where(lane_head == kh, k_all, zero) for kh in kvh], axis=0)
        v_bd = jnp.concatenate([jnp.where(lane_head == kh, v_all, zero) for kh in kvh], axis=0)
        q_cat = jnp.concatenate([q_ref[0, rows, g * dk:(g + 1) * dk] for g in range(GROUP)],
                                axis=0)
        s = lax.dot_general(q_cat, k_bd, (((1,), (1,)), ((), ())), preferred_element_type=F32)
        es, invs = [], []
        for kh in kvh:
            sk = s[:, kh * 2 * WINDOW:(kh + 1) * 2 * WINDOW] + bias_ref[variant, kh]
            e = jnp.exp(sk - jnp.max(sk, axis=1, keepdims=True))
            invs.append(1.0 / jnp.sum(e, axis=1, keepdims=True))
            es.append(e.astype(BF16))
        pv = jnp.dot(jnp.concatenate(es, axis=1), v_bd, preferred_element_type=F32)
        inv = invs[KV_HEADS - 1]
        for kh in range(KV_HEADS - 1):
            inv = jnp.where(out_head == kh, invs[kh], inv)
        o = pv * inv
        for g in range(GROUP):
            o_ref[0, rows, g * dk:(g + 1) * dk] = o[g * WINDOW:(g + 1) * WINDOW].astype(o_ref.dtype)


def attn_prompt(q, k, v, table, batch, seq):
    d = Q_HEADS * HEAD_DIM
    dk = KV_HEADS * HEAD_DIM
    nq = ATTN_PROMPT_BLOCKS
    r3 = lambda a: a.reshape(batch, seq, a.shape[-1])
    prev = pl.BlockSpec((1, WINDOW, dk), lambda b, i: (b, jnp.maximum(i * nq - 1, 0), 0))
    cur = pl.BlockSpec((1, nq * WINDOW, dk), lambda b, i: (b, i, 0))
    table = table.reshape(KV_HEADS, GROUP * WINDOW, 2 * WINDOW)
    key = jnp.arange(2 * WINDOW)
    no_prev = jnp.where((key < WINDOW) & (key != PROMPT_SINK_KEY), NEG_INF, table)
    tables = jnp.stack([no_prev, table])
    out = pl.pallas_call(
        _attn_prompt_kernel,
        grid=(batch, seq // (nq * WINDOW)),
        in_specs=[pl.BlockSpec((1, nq * WINDOW, d), lambda b, i: (b, i, 0)),
                  prev, cur, prev, cur,
                  pl.BlockSpec(tables.shape, lambda b, i: (0, 0, 0, 0))],
        out_specs=pl.BlockSpec((1, nq * WINDOW, d), lambda b, i: (b, i, 0)),
        out_shape=jax.ShapeDtypeStruct((batch, seq, d), BF16),
        compiler_params=_cparams("parallel", "arbitrary"),
        name="attn_prompt",
    )(r3(q), r3(k), r3(k), r3(v), r3(v), tables)
    return out.reshape(batch * seq, d)


SAMPLE_NEW_KEYS = 16


def _attn_sample_kernel(q_ref, kct_ref, vct_ref, kn_ref, vn_ref, knt_ref, vnt_ref, bias_c_ref,
                        bias_n_ref, o_ref, *cache_out_refs, seq):
    n_b = q_ref.shape[0]
    dk = KV_HEADS * HEAD_DIM
    pairs = [(b, kh) for b in range(n_b) for kh in range(KV_HEADS)]
    kslice = lambda kh: slice(kh * HEAD_DIM, (kh + 1) * HEAD_DIM)
    head_lanes = lambda kh, g: slice((g * KV_HEADS + kh) * HEAD_DIM, (g * KV_HEADS + kh + 1) * HEAD_DIM)
    pad = jnp.zeros((SAMPLE_NEW_KEYS - seq, dk), F32)
    nt = (((1,), (1,)), ((), ()))
    dot = functools.partial(jnp.dot, preferred_element_type=F32)
    dot_nt = functools.partial(lax.dot_general, dimension_numbers=nt, preferred_element_type=F32)

    if cache_out_refs:
        lane = lax.broadcasted_iota(jnp.int32, (HEAD_DIM, WINDOW), 1)
        steps_per_block = knt_ref.shape[1] // (n_b * seq)
        first_token = (pl.program_id(0) % steps_per_block) * (n_b * seq)
        for out_ref, old_ref, newt_ref in zip(cache_out_refs, (kct_ref, vct_ref), (knt_ref, vnt_ref)):
            for b, kh in pairs:
                moved = pltpu.roll(old_ref[b, kh], WINDOW - seq, 1)
                shift = (2 * WINDOW - seq - first_token - b * seq) % WINDOW
                fresh = pltpu.roll(newt_ref[kslice(kh), :], shift, 1)
                out_ref[b, kh] = jnp.where(lane >= WINDOW - seq, fresh, moved)

    qss = [jnp.concatenate([q_ref[b, :, head_lanes(kh, g)] for g in range(GROUP)],
                           axis=0).astype(BF16) for b, kh in pairs]
    kcts = [kct_ref[b, kh].astype(BF16) for b, kh in pairs]
    vcts = [vct_ref[b, kh].astype(BF16) for b, kh in pairs]
    kns = [jnp.concatenate([kn_ref[b], pad], axis=0).astype(BF16) for b in range(n_b)]
    vns = [jnp.concatenate([vn_ref[b], pad], axis=0).astype(BF16) for b in range(n_b)]
    s_cs = [dot(qss[i], kcts[i]) for i in range(len(pairs))]
    s_ns = [dot_nt(qss[i], kns[b][:, kslice(kh)]) for i, (b, kh) in enumerate(pairs)]
    e_cs, e_ns, denoms = [], [], []
    for i, (b, kh) in enumerate(pairs):
        s_c = s_cs[i] + bias_c_ref[kh]
        s_n = s_ns[i] + bias_n_ref[kh]
        m = jnp.maximum(jnp.max(s_c, axis=1, keepdims=True), jnp.max(s_n, axis=1, keepdims=True))
        e_c, e_n = jnp.exp(s_c - m), jnp.exp(s_n - m)
        denoms.append(jnp.sum(e_c, axis=1, keepdims=True) + jnp.sum(e_n, axis=1, keepdims=True))
        e_cs.append(e_c.astype(BF16))
        e_ns.append(e_n.astype(BF16))
    pv_cs = [dot_nt(e_cs[i], vcts[i]) for i in range(len(pairs))]
    pv_ns = [dot(e_ns[i], vns[b][:, kslice(kh)]) for i, (b, kh) in enumerate(pairs)]
    for i, (b, kh) in enumerate(pairs):
        o = (pv_cs[i] + pv_ns[i]) / denoms[i]
        for g in range(GROUP):
            o_ref[b, :, head_lanes(kh, g)] = o[g * seq:(g + 1) * seq].astype(o_ref.dtype)


def attn_sample(q, k_new, v_new, cache_k, cache_v, table, batch, seq, write_cache):
    d = Q_HEADS * HEAD_DIM
    dk = KV_HEADS * HEAD_DIM
    nbk = ATTN_SAMPLE_BLOCK
    assert WINDOW % (nbk * seq) == 0
    new = pl.BlockSpec((nbk, seq, dk), lambda i: (i, 0, 0))
    old = pl.BlockSpec((nbk, KV_HEADS, HEAD_DIM, WINDOW), lambda i: (i, 0, 0, 0))
    newt = pl.BlockSpec((dk, WINDOW), lambda i: (0, i * nbk * seq // WINDOW))
    to_t = lambda c: jnp.transpose(c, (0, 2, 3, 1))
    table = table.reshape(KV_HEADS, GROUP * seq, 2 * WINDOW)
    n_cache = 2 if write_cache else 0
    out = pl.pallas_call(
        functools.partial(_attn_sample_kernel, seq=seq),
        grid=(batch // nbk,),
        in_specs=[pl.BlockSpec((nbk, seq, d), lambda i: (i, 0, 0)),
                  old, old, new, new, newt, newt,
                  pl.BlockSpec((KV_HEADS, GROUP * seq, WINDOW), lambda i: (0, 0, 0)),
                  pl.BlockSpec((KV_HEADS, GROUP * seq, SAMPLE_NEW_KEYS), lambda i: (0, 0, 0))],
        out_specs=[pl.BlockSpec((nbk, seq, d), lambda i: (i, 0, 0))] + [old] * n_cache,
        out_shape=[jax.ShapeDtypeStruct((batch, seq, d), F32)]
                  + [jax.ShapeDtypeStruct((batch, KV_HEADS, HEAD_DIM, WINDOW), F32)] * n_cache,
        compiler_params=_cparams("parallel"),
        name="attn_sample",
    )(q.reshape(batch, seq, d), to_t(cache_k), to_t(cache_v),
      k_new.reshape(batch, seq, dk), v_new.reshape(batch, seq, dk), k_new.T, v_new.T,
      table[:, :, :WINDOW], table[:, :, WINDOW:WINDOW + SAMPLE_NEW_KEYS])
    return (out[0].reshape(batch * seq, d),) + tuple(jnp.transpose(c, (0, 3, 1, 2)) for c in out[1:])


def _trunk(x, state, cache, w, tm, tm_mlp):
    batch, seq, d = x.shape
    x = x.reshape(batch * seq, d)
    hq, hv = N_HEADS * DQK, N_HEADS * DV
    depth = w["norm_mix"].shape[0]
    n_a = w["w_in"].shape[0]
    prompt = state is None
    act = BF16 if prompt else F32
    cs, ns, ms = [], [], []
    c_stack = None if prompt else state[0]
    for l in range(depth):
        if l < n_a:
            gates = (w["b_igate"][l], w["b_fgate"][l], w["mlstm_norm"][l])
            q, k, v, o, gate = mlstm_inproj(x, w["norm_mix"], w["w_in"], l, tm, prompt, act)
            if prompt:
                a, c_new, n_new, m_new = mlstm_prompt(q, k, v, o, gate, *gates, batch, seq)
                cs.append(c_new)
            else:
                a, c_stack, n_new, m_new = mlstm_sample(q, k, v, o, gate, *gates, state[0], c_stack,
                                                        l, state[1][l], state[2][l], batch, seq)
            ns.append(n_new); ms.append(m_new)
            w_o, lo = w["w_mlstm_out"], l
        else:
            j = l - n_a
            q_proj = (w["norm_mix"], l, w["w_q"], j, ((0, Q_HEADS * HEAD_DIM),), (act,))
            if j == 0:
                dk = KV_HEADS * HEAD_DIM
                kv_proj = (w["kv_norm"], 0, w["w_kv"], 0, ((0, dk), (dk, dk), (0, dk), (dk, dk)),
                           (F32, F32, BF16, BF16))
                (k_new, v_new, k16, v16), (q,) = norm_matmul(x, (kv_proj, q_proj), tm_mlp)
                tables = bias_table(w["rel_bias"], w["attn_sinks"], WINDOW if prompt else seq,
                                    2 * WINDOW, PROMPT_SINK_KEY if prompt else WINDOW + seq)
            else:
                ((q,),) = norm_matmul(x, (q_proj,), tm_mlp)
            if prompt:
                a = attn_prompt(q, k16, v16, tables[j], batch, seq)
            elif j == 0:
                a, win_k, win_v = attn_sample(q, k_new, v_new, cache[0], cache[1], tables[j],
                                              batch, seq, True)
            else:
                (a,) = attn_sample(q, k_new, v_new, cache[0], cache[1], tables[j],
                                   batch, seq, False)
            w_o, lo = w["w_attn_out"], j
        x = mlp(a, w_o, lo, x, w["norm_ffn"], w["w_up"], w["w_down"], l, w["final_norm"],
                l == depth - 1, tm_mlp, MLP_FF_BLOCK if prompt else MLP_FF_CHUNK)
    if prompt:
        dk = KV_HEADS * HEAD_DIM
        win_k = k_new.reshape(batch, seq, dk)[:, -WINDOW:]
        win_v = v_new.reshape(batch, seq, dk)[:, -WINDOW:]
        c_stack = jnp.stack(cs)
    shp = (batch, WINDOW, KV_HEADS, HEAD_DIM)
    return (x.reshape(batch, seq, d), c_stack, jnp.stack(ns), jnp.stack(ms),
            win_k.reshape(shp), win_v.reshape(shp))


def kernel(x_prompt, x_sample, state_C, state_n, state_m, cache_k, cache_v, norm_mix, norm_ffn,
           w_mlstm_in, b_igate, b_fgate, mlstm_norm, w_mlstm_out, kv_norm, w_kv, w_q, attn_sinks,
           w_attn_out, rel_bias, w_up, w_down, final_norm):
    n_b, d = w_q.shape[0], w_q.shape[1]
    heads = (KV_HEADS, GROUP, HEAD_DIM)
    w_q_perm = (w_q * HEAD_DIM ** -0.5).astype(BF16).reshape((n_b, d) + heads)
    w_q_perm = w_q_perm.transpose(0, 1, 3, 2, 4).reshape(n_b, d, d)
    w_ao_perm = w_attn_out.astype(BF16).reshape((n_b,) + heads + (d,))
    w_ao_perm = w_ao_perm.transpose(0, 2, 1, 3, 4).reshape(n_b, d, d)
    w = dict(norm_mix=norm_mix[:, None, :], norm_ffn=norm_ffn[:, None, :],
             w_in=w_mlstm_in.astype(BF16), b_igate=b_igate, b_fgate=b_fgate,
             mlstm_norm=mlstm_norm, w_mlstm_out=w_mlstm_out.astype(BF16),
             kv_norm=kv_norm[None, None, :], w_kv=w_kv.astype(BF16)[None],
             w_q=w_q_perm, attn_sinks=attn_sinks, w_attn_out=w_ao_perm, rel_bias=rel_bias,
             w_up=w_up.astype(BF16), w_down=w_down.astype(BF16), final_norm=final_norm)
    y_p, c_p, n_p, m_p, k_p, v_p = _trunk(x_prompt, None, None, w, 512, 1024)
    y_s, c_s, n_s, m_s, k_s, v_s = _trunk(x_sample, (state_C, state_n, state_m),
                                          (cache_k, cache_v), w, 512, 512)
    return (y_p, y_s, c_p, n_p, m_p, k_p, v_p, c_s, n_s, m_s, k_s, v_s)
```

```python
import functools
import math

import jax
import jax.numpy as jnp
from jax import lax
from jax.experimental import pallas as pl
from jax.experimental.pallas import tpu as pltpu

F32 = jnp.float32
BF16 = jnp.bfloat16
EPS = 1e-6
LOG2_E = 1.4426950408889634
NEG_INF = float("-inf")

N_HEADS = 8
DQK = 64
DV = 128
Q_HEADS = 16
KV_HEADS = 4
GROUP = Q_HEADS // KV_HEADS
HEAD_DIM = 64
WINDOW = 128
N_BUCKETS = 32
MAX_EXACT = N_BUCKETS // 2
MAX_DISTANCE = 128

LANES = 128
VMEM_LIMIT = 48 * 1024 * 1024
MLP_VMEM_LIMIT = 56 * 1024 * 1024

PROMPT_CHUNK = 128
PROMPT_HEAD_GROUP = 4
PROMPT_BLOCK = 512
SAMPLE_BATCH_BLOCK = 8
ATTN_SAMPLE_BLOCK = 8
ATTN_PROMPT_BLOCKS = 8
MLP_ROW_CHUNK = 512
MLP_FF_BLOCK = 2048
MLP_FF_CHUNK = 1024


def _cparams(*sem, vmem_limit=VMEM_LIMIT):
    return pltpu.CompilerParams(dimension_semantics=sem, vmem_limit_bytes=vmem_limit)


def _rms(x, g):
    return x * lax.rsqrt(jnp.mean(x * x, axis=-1, keepdims=True) + EPS) * g


def _split3(x):
    hi = x.astype(BF16)
    r = x - hi.astype(F32)
    mid = r.astype(BF16)
    lo = (r - mid.astype(F32)).astype(BF16)
    return hi, mid, lo


def _dot01(x, onehot):
    hi, mid, lo = _split3(x)
    d = lambda a: jnp.dot(a, onehot, preferred_element_type=F32)
    return d(hi) + d(mid) + d(lo)


def _dot01_left(onehot, x):
    hi, mid, lo = _split3(x)
    d = lambda a: jnp.dot(onehot, a, preferred_element_type=F32)
    return d(hi) + d(mid) + d(lo)


def _log_sigmoid(x):
    return jnp.minimum(x, 0.0) - jnp.log1p(jnp.exp(-jnp.abs(x)))


def _sigmoid(x):
    return 1.0 / (1.0 + jnp.exp(-x))


def _norm_mm_kernel(x_ref, *refs, splits):
    n_proj = len(splits)
    out_refs = iter(refs[2 * n_proj:])
    x = x_ref[...]
    xhat = x * lax.rsqrt(jnp.mean(x * x, axis=-1, keepdims=True) + EPS)
    ys = [(xhat * refs[2 * p][...]).astype(BF16) for p in range(n_proj)]
    rs = [jnp.dot(ys[p], refs[2 * p + 1][...], preferred_element_type=F32) for p in range(n_proj)]
    for p in range(n_proj):
        for off, n in splits[p]:
            o_ref = next(out_refs)
            o_ref[...] = rs[p][:, off:off + n].astype(o_ref.dtype)


def _layer_spec(arr, layer):
    idx = (layer,) + (0,) * (arr.ndim - 1)
    return pl.BlockSpec((None,) + arr.shape[1:], lambda *_: idx)


def norm_matmul(x, projections, tm):
    m, d = x.shape
    assert m % tm == 0
    params, param_specs, out_specs, out_shapes = [], [], [], []
    for g, lg, w, lw, splits, dtypes in projections:
        assert all(off + n <= w.shape[-1] for off, n in splits)
        params += [g, w]
        param_specs += [_layer_spec(g, lg), _layer_spec(w, lw)]
        out_specs += [pl.BlockSpec((tm, n), lambda i: (i, 0)) for _, n in splits]
        out_shapes += [jax.ShapeDtypeStruct((m, n), dt) for (_, n), dt in zip(splits, dtypes)]
    outs = pl.pallas_call(
        functools.partial(_norm_mm_kernel, splits=tuple(p[4] for p in projections)),
        grid=(m // tm,),
        in_specs=[pl.BlockSpec((tm, d), lambda i: (i, 0))] + param_specs,
        out_specs=out_specs,
        out_shape=out_shapes,
        compiler_params=_cparams("parallel"),
        name="norm_matmul",
    )(x, *params)
    outs, grouped = list(outs), []
    for p in projections:
        grouped.append(outs[:len(p[4])])
        outs = outs[len(p[4]):]
    return grouped


def _mlp_kernel(a_ref, wo_ref, x_ref, g_ref, wu_ref, wd_ref, gf_ref, o_ref, xn_ref, *,
                final_norm, row_chunk, ff_chunk):
    j = pl.program_id(1)
    tm = x_ref.shape[0]

    @pl.when(j == 0)
    def _():
        for r in range(0, tm, row_chunk):
            rows = slice(r, r + row_chunk)
            x = x_ref[rows, :] + jnp.dot(a_ref[rows, :].astype(BF16), wo_ref[...],
                                         preferred_element_type=F32)
            xn_ref[rows, :] = _rms(x, g_ref[...]).astype(BF16)
            o_ref[rows, :] = x

    for c in range(0, wu_ref.shape[1], ff_chunk):
        h = jnp.dot(xn_ref[...], wu_ref[:, c:c + ff_chunk], preferred_element_type=F32)
        h = jnp.maximum(h, 0.0)
        o_ref[...] += jnp.dot((h * h).astype(BF16), wd_ref[c:c + ff_chunk, :],
                              preferred_element_type=F32)

    if final_norm:
        @pl.when(j == pl.num_programs(1) - 1)
        def _():
            o_ref[...] = _rms(o_ref[...], gf_ref[...])


def mlp(a, w_o, lo, x, g, w_up, w_down, l, g_final, final_norm, tm, tf):
    m, d = x.shape
    ff = w_up.shape[-1]
    return pl.pallas_call(
        functools.partial(_mlp_kernel, final_norm=final_norm, row_chunk=min(tm, MLP_ROW_CHUNK),
                          ff_chunk=MLP_FF_CHUNK),
        grid=(m // tm, ff // tf),
        in_specs=[pl.BlockSpec((tm, d), lambda i, j: (i, 0)),
                  _layer_spec(w_o, lo),
                  pl.BlockSpec((tm, d), lambda i, j: (i, 0)),
                  _layer_spec(g, l),
                  pl.BlockSpec((None, d, tf), lambda i, j: (l, 0, j)),
                  pl.BlockSpec((None, tf, d), lambda i, j: (l, j, 0)),
                  pl.BlockSpec((1, d), lambda i, j: (0, 0))],
        out_specs=pl.BlockSpec((tm, d), lambda i, j: (i, 0)),
        out_shape=jax.ShapeDtypeStruct((m, d), F32),
        scratch_shapes=[pltpu.VMEM((tm, d), BF16)],
        compiler_params=_cparams("parallel", "arbitrary",
                                 vmem_limit=MLP_VMEM_LIMIT if tf > MLP_FF_CHUNK else VMEM_LIMIT),
        name="mlp",
    )(a, w_o, x, g, w_up, w_down, g_final.reshape(1, d))


def _mlstm_inproj_kernel(x_ref, g_ref, w_ref, q_ref, k_ref, v_ref, o_ref, gate_ref, *, k_transposed):
    hq, hv = N_HEADS * DQK, N_HEADS * DV
    tm = x_ref.shape[0]
    y = _rms(x_ref[...], g_ref[...]).astype(BF16)
    r = jnp.dot(y, w_ref[...], preferred_element_type=F32)
    q_ref[...] = r[:, :hq].astype(q_ref.dtype)
    k = r[:, hq:2 * hq] * DQK ** -0.5
    k_ref[...] = (k.T if k_transposed else k).astype(k_ref.dtype)
    v_ref[...] = r[:, 2 * hq:2 * hq + hv].astype(v_ref.dtype)
    o_ref[...] = r[:, 2 * hq + hv:2 * hq + 2 * hv]
    gate_ref[...] = jnp.concatenate(
        [r[:, 2 * hq + 2 * hv:], jnp.zeros((tm, LANES - 2 * N_HEADS), F32)], axis=1)


def mlstm_inproj(x, g, w, l, tm, k_transposed, act):
    m, d = x.shape
    hq, hv = N_HEADS * DQK, N_HEADS * DV
    assert w.shape[-1] == 2 * hq + 2 * hv + 2 * N_HEADS
    row = lambda width: pl.BlockSpec((tm, width), lambda i: (i, 0))
    k_spec = pl.BlockSpec((hq, tm), lambda i: (0, i)) if k_transposed else row(hq)
    k_shape = (hq, m) if k_transposed else (m, hq)
    return pl.pallas_call(
        functools.partial(_mlstm_inproj_kernel, k_transposed=k_transposed),
        grid=(m // tm,),
        in_specs=[row(d), _layer_spec(g, l), _layer_spec(w, l)],
        out_specs=[row(hq), k_spec, row(hv), row(hv), row(LANES)],
        out_shape=[jax.ShapeDtypeStruct((m, hq), act), jax.ShapeDtypeStruct(k_shape, act),
                   jax.ShapeDtypeStruct((m, hv), act), jax.ShapeDtypeStruct((m, hv), F32),
                   jax.ShapeDtypeStruct((m, LANES), F32)],
        compiler_params=_cparams("parallel"),
        name="mlstm_inproj",
    )(x, g, w)


def _chunk_scan(x, pos, op, fill, length):
    k = 1
    while k < length:
        x = op(x, jnp.where(pos >= k, pltpu.roll(x, k, 0), fill))
        k *= 2
    return x


def _mlstm_prompt_kernel(q_ref, kt_ref, v_ref, o_ref, gate_ref, bi_ref, bf_ref, gain_ref,
                         hg_ref, cx_out_ref, m_out_ref, cx_s, m_s, *, chunk):
    t_blk = q_ref.shape[1]
    n_chunks = t_blk // chunk
    step = pl.program_id(1)

    @pl.when(step == 0)
    def _():
        cx_s[...] = jnp.zeros_like(cx_s)
        m_s[...] = jnp.zeros_like(m_s)

    gates = gate_ref[0]
    li = gates + bi_ref[...]
    lf = _log_sigmoid(pltpu.roll(gates, LANES - N_HEADS, 1) + bf_ref[...])
    pos = lax.broadcasted_iota(jnp.int32, (t_blk, 1), 0) % chunk
    b = _chunk_scan(lf, pos, jnp.add, 0.0, chunk)
    c = li - b
    cm = _chunk_scan(c, pos, jnp.maximum, NEG_INF, chunk)

    m_prev = m_s[...]
    xs, w_inters, e_negms, wks = [], [], [], []
    for ck in range(n_chunks):
        sl = slice(ck * chunk, (ck + 1) * chunk)
        m_t = b[sl] + jnp.maximum(m_prev, cm[sl])
        m_new = m_t[chunk - 1:chunk]
        b_last = b[(ck + 1) * chunk - 1:(ck + 1) * chunk]
        xs.append((b[sl] - m_t) * LOG2_E)
        w_inters.append(jnp.exp(b[sl] + m_prev - m_t))
        e_negms.append(jnp.exp(-m_t))
        wks.append(jnp.exp(c[sl] + (b_last - m_new)))
        m_prev = m_new
    m_s[...] = m_prev
    c_t = (c * LOG2_E).T
    wk_t = jnp.concatenate(wks, axis=0).T

    ri = lax.broadcasted_iota(jnp.int32, (chunk, chunk), 0)
    ci = lax.broadcasted_iota(jnp.int32, (chunk, chunk), 1)
    causal = ci <= ri
    ones_blk = jnp.ones((chunk, DV), BF16)
    mean_sq = jnp.full((DV, DV), 1.0 / DV, BF16)
    heads = range(N_HEADS)
    dot = functools.partial(jnp.dot, preferred_element_type=F32)

    cxs = {h: cx_s[h] for h in heads}
    head_groups = [heads[g:g + PROMPT_HEAD_GROUP] for g in range(0, N_HEADS, PROMPT_HEAD_GROUP)]
    for ck in range(n_chunks):
        sl = slice(ck * chunk, (ck + 1) * chunk)
        for hs in head_groups:
            qs = {h: q_ref[0, sl, h * DQK:(h + 1) * DQK] for h in hs}
            kts = {h: kt_ref[h * DQK:(h + 1) * DQK, sl] for h in hs}
            vs = {h: v_ref[0, sl, h * DV:(h + 1) * DV] for h in hs}
            cx16s = {h: cxs[h].astype(BF16) for h in hs}
            kws = {h: (kts[h].astype(F32) * wk_t[h:h + 1, sl]).astype(BF16) for h in hs}
            vones = {h: jnp.concatenate([vs[h], ones_blk], axis=1) for h in hs}
            ss = {h: dot(qs[h], kts[h]) for h in hs}
            qcs = {h: dot(qs[h], cx16s[h]) for h in hs}
            dcs = {h: dot(kws[h], vones[h]) for h in hs}
            for h in hs:
                cxs[h] = w_inters[ck][chunk - 1:chunk, h:h + 1] * cxs[h] + dcs[h]
            a16s = {}
            for h in hs:
                dm = xs[ck][:, h:h + 1] + c_t[h:h + 1, sl]
                a16s[h] = (ss[h] * jnp.exp2(jnp.where(causal, dm, NEG_INF))).astype(BF16)
            avs = {h: dot(a16s[h], vones[h]) for h in hs}
            hhs = {}
            for h in hs:
                wi = jnp.broadcast_to(w_inters[ck][:, h:h + 1], (chunk, DV))
                en = jnp.broadcast_to(e_negms[ck][:, h:h + 1], (chunk, DV))
                den = avs[h][:, DV:] + wi * qcs[h][:, DV:]
                inv = 1.0 / jnp.maximum(jnp.abs(den), en)
                hhs[h] = (avs[h][:, :DV] + qcs[h][:, :DV] * wi) * inv
            sq16s = {h: (hhs[h] * hhs[h]).astype(BF16) for h in hs}
            mss = {h: dot(sq16s[h], mean_sq) for h in hs}
            for h in hs:
                hn = hhs[h] * lax.rsqrt(mss[h] + EPS)
                hn = hn * gain_ref[:, h * DV:(h + 1) * DV]
                og = _sigmoid(o_ref[0, sl, h * DV:(h + 1) * DV])
                hg_ref[0, sl, h * DV:(h + 1) * DV] = (og * hn).astype(hg_ref.dtype)
    for h in heads:
        cx_s[h] = cxs[h]

    @pl.when(step == pl.num_programs(1) - 1)
    def _():
        cx_out_ref[0] = cx_s[...]
        m_out_ref[0] = m_s[...]


def mlstm_prompt(q, kt, v, o, gate, b_i, b_f, gain, batch, seq):
    hq, hv = N_HEADS * DQK, N_HEADS * DV
    t = PROMPT_BLOCK
    nblk = seq // t
    r3 = lambda a: a.reshape(batch, seq, a.shape[-1])
    pad8 = lambda a: jnp.pad(a.reshape(1, N_HEADS), ((0, 0), (0, LANES - N_HEADS)))
    tok = lambda w: pl.BlockSpec((1, t, w), lambda bb, s: (bb, s, 0))
    cst = lambda w: pl.BlockSpec((1, w), lambda bb, s: (0, 0))
    hg, cx, m_new = pl.pallas_call(
        functools.partial(_mlstm_prompt_kernel, chunk=PROMPT_CHUNK),
        grid=(batch, nblk),
        in_specs=[tok(hq), pl.BlockSpec((hq, t), lambda bb, s: (0, bb * nblk + s)),
                  tok(hv), tok(hv), tok(LANES), cst(LANES), cst(LANES), cst(hv)],
        out_specs=[tok(hv),
                   pl.BlockSpec((1, N_HEADS, DQK, 2 * DV), lambda bb, s: (bb, 0, 0, 0)),
                   pl.BlockSpec((1, 1, LANES), lambda bb, s: (bb, 0, 0))],
        out_shape=[jax.ShapeDtypeStruct((batch, seq, hv), BF16),
                   jax.ShapeDtypeStruct((batch, N_HEADS, DQK, 2 * DV), F32),
                   jax.ShapeDtypeStruct((batch, 1, LANES), F32)],
        scratch_shapes=[pltpu.VMEM((N_HEADS, DQK, 2 * DV), F32), pltpu.VMEM((1, LANES), F32)],
        compiler_params=_cparams("parallel", "arbitrary"),
        name="mlstm_prompt",
    )(r3(q), kt, r3(v), r3(o), r3(gate), pad8(b_i), pad8(b_f), gain.reshape(1, hv))
    return hg.reshape(batch * seq, hv), cx[..., :DV], cx[..., DV], m_new[:, 0, :N_HEADS]


def _mlstm_sample_kernel(q_ref, k_ref, v_ref, o_ref, gate_ref, m0_ref, n0_ref, c0_ref, c_carry_ref,
                         bi_ref, bf_ref, gain_ref, seg64_ref, seg128_ref, e64_ref, e128_ref,
                         hg_ref, c_out_ref, n_out_ref, m_out_ref,
                         qc_s, *, seq):
    del c_carry_ref
    spans_layers = len(c_out_ref.shape) == 5
    c_new_ref = c_out_ref.at[0] if spans_layers else c_out_ref
    if spans_layers:
        for later in range(1, c_out_ref.shape[0]):
            c_out_ref[later] = jnp.zeros(c_out_ref.shape[1:], F32)
    rows = q_ref.shape[0]
    n_b = rows // seq
    tpos = lax.broadcasted_iota(jnp.int32, (rows, 1), 0) % seq
    dot = functools.partial(jnp.dot, preferred_element_type=F32)

    def shift(x, d):
        return x if d == 0 else pltpu.roll(x, d, 0)

    def unshift(x, d):
        return x if d == 0 else pltpu.roll(x, rows - d, 0)

    def split_cat(x):
        return jnp.concatenate(_split3(x), axis=1)


    q = q_ref[...].astype(BF16).astype(F32)
    k = k_ref[...].astype(BF16).astype(F32)
    v = v_ref[...].astype(BF16).astype(F32)

    units = [(p, h) for p in range(n_b // 2) for h in range(N_HEADS)]
    prow = lambda p: slice(p * 2 * seq, (p + 1) * 2 * seq)
    q_pairs = [q[prow(p), h * DQK:(h + 1) * DQK].astype(BF16) for p, h in units]
    c_cats = [jnp.concatenate([c0_ref[2 * p, h], c0_ref[2 * p + 1, h]], axis=1).astype(BF16)
              for p, h in units]
    qcs = [dot(q_pairs[i], c_cats[i]) for i in range(len(units))]

    gates = gate_ref[...]
    li = gates + bi_ref[...]
    lf = _log_sigmoid(pltpu.roll(gates, LANES - N_HEADS, 1) + bf_ref[...])
    b = lf
    for d in range(1, seq):
        b = b + jnp.where(tpos >= d, shift(lf, d), 0.0)
    inter = b + m0_ref[...]
    dvals = []
    m_t = inter
    for d in range(seq):
        dd = jnp.where(tpos >= d, b - shift(b, d) + shift(li, d), NEG_INF)
        dvals.append(dd)
        m_t = jnp.maximum(m_t, dd)
    w_inter = jnp.exp(inter - m_t)
    w_intra = [jnp.exp(dd - m_t) for dd in dvals]
    last = lambda x: functools.reduce(
        lambda acc, d: jnp.where(tpos == seq - 1 - d, unshift(x, d), acc), range(1, seq), x)
    wk = jnp.exp(last(b) - b + li - last(m_t))

    qk_in = [split_cat(q * shift(k, d)) for d in range(seq)]
    qn_in = split_cat(q * n0_ref[...].astype(BF16).astype(F32))
    wk_in, wi_in = split_cat(wk), split_cat(w_inter)
    seg64, e64, e128 = seg64_ref[...], e64_ref[...], e128_ref[...]
    qks = [dot(x, seg64) for x in qk_in]
    qn = dot(qn_in, seg64)
    wk_exp = dot(wk_in, e64)
    decay64 = dot(wi_in, e64)
    decay_exp = dot(wi_in, e128)

    a_s = [qks[d] * w_intra[d] for d in range(seq)]
    e128_1 = e128[:LANES]
    a_exps = [dot(a.astype(BF16), e128_1) for a in a_s]
    den = w_inter * qn
    num = jnp.zeros((rows, N_HEADS * DV), F32)
    for d in range(seq):
        den = den + a_s[d]
        num = num + a_exps[d] * shift(v, d)

    kw = k * wk_exp
    ksum = kw
    for d in range(1, seq):
        ksum = ksum + shift(kw, d)
    n_rows = decay64 * n0_ref[...] + ksum
    own = ((lax.broadcasted_iota(jnp.int32, (2 * seq, 2 * DV), 0) < seq)
           == (lax.broadcasted_iota(jnp.int32, (2 * seq, 2 * DV), 1) < DV))
    kw_pairs = [kw[prow(p), h * DQK:(h + 1) * DQK].astype(BF16) for p, h in units]
    v2s = []
    for p, h in units:
        v_pair = v[prow(p), h * DV:(h + 1) * DV]
        v2s.append(jnp.where(own, jnp.concatenate([v_pair, v_pair], axis=1), 0.0).astype(BF16))
    sel3 = (lax.broadcasted_iota(jnp.int32, (n_b, 3 * rows), 1) % rows
            == lax.broadcasted_iota(jnp.int32, (n_b, 3 * rows), 0) * seq + (seq - 1))
    sel3 = jnp.where(sel3, 1.0, 0.0).astype(BF16)
    n_in = jnp.concatenate(_split3(n_rows), axis=0)
    m_in = jnp.concatenate(_split3(m_t), axis=0)
    d_cs = [lax.dot_general(kw_pairs[i], v2s[i], (((0,), (0,)), ((), ())),
                            preferred_element_type=F32) for i in range(len(units))]
    n_out_ref[...] = dot(sel3, n_in)
    m_out_ref[...] = dot(sel3, m_in)

    first = lax.broadcasted_iota(jnp.int32, (2 * seq, DV), 0) < seq
    for i, (p, h) in enumerate(units):
        r0 = p * 2 * seq
        qc_s[prow(p), h * DV:(h + 1) * DV] = jnp.where(first, qcs[i][:, :DV], qcs[i][:, DV:])
        dec_a = decay_exp[r0 + seq - 1:r0 + seq, h * DV:(h + 1) * DV]
        dec_b = decay_exp[r0 + 2 * seq - 1:r0 + 2 * seq, h * DV:(h + 1) * DV]
        c_new_ref[2 * p, h] = dec_a * c0_ref[2 * p, h] + d_cs[i][:, :DV]
        c_new_ref[2 * p + 1, h] = dec_b * c0_ref[2 * p + 1, h] + d_cs[i][:, DV:]

    num = num + qc_s[...] * decay_exp
    inv_scale = 1.0 / jnp.maximum(jnp.abs(den), jnp.exp(-m_t))
    hh = num * dot(split_cat(inv_scale), e128)
    ms = dot(split_cat(hh * hh), seg128_ref[...]) * (1.0 / DV)
    hn = hh * dot(split_cat(lax.rsqrt(ms + EPS)), e128) * gain_ref[...]
    hg_ref[...] = (_sigmoid(o_ref[...]) * hn).astype(hg_ref.dtype)


def mlstm_sample(q, k, v, o, gate, b_i, b_f, gain, state_c, c_carry, l, n0, m0, batch, seq):
    hq, hv = N_HEADS * DQK, N_HEADS * DV
    c_blk = pl.BlockSpec((None, SAMPLE_BATCH_BLOCK, N_HEADS, DQK, DV), lambda i: (l, i, 0, 0, 0))
    c_all = pl.BlockSpec((state_c.shape[0], SAMPLE_BATCH_BLOCK, N_HEADS, DQK, DV),
                         lambda i: (0, i, 0, 0, 0))
    rows = SAMPLE_BATCH_BLOCK * seq
    pad8 = lambda a: jnp.pad(a.reshape(1, N_HEADS), ((0, 0), (0, LANES - N_HEADS)))
    m0_rows = jnp.pad(jnp.repeat(m0, seq, axis=0), ((0, 0), (0, LANES - N_HEADS)))
    n0_rows = jnp.repeat(n0.reshape(batch, hq), seq, axis=0)
    lane = jnp.arange(LANES)
    seg64 = (jnp.arange(hq)[:, None] // DQK == lane[None, :]).astype(BF16)
    seg128 = (jnp.arange(hv)[:, None] // DV == lane[None, :]).astype(BF16)
    tok = lambda w: pl.BlockSpec((rows, w), lambda i: (i, 0))
    cst = lambda a: pl.BlockSpec(a.shape, lambda i: (0,) * a.ndim)
    x3 = lambda a: jnp.tile(a, (3, 1))
    consts = [pad8(b_i), pad8(b_f), gain.reshape(1, hv),
              x3(seg64), x3(seg128), x3(seg64.T), x3(seg128.T)]
    hg, c_new, n_new, m_new = pl.pallas_call(
        functools.partial(_mlstm_sample_kernel, seq=seq),
        grid=(batch // SAMPLE_BATCH_BLOCK,),
        in_specs=[tok(hq), tok(hq), tok(hv), tok(hv), tok(LANES), tok(LANES), tok(hq),
                  c_blk, pl.BlockSpec(memory_space=pl.ANY)]
                 + [cst(a) for a in consts],
        out_specs=[tok(hv), c_all if l == 0 else c_blk,
                   pl.BlockSpec((SAMPLE_BATCH_BLOCK, hq), lambda i: (i, 0)),
                   pl.BlockSpec((SAMPLE_BATCH_BLOCK, LANES), lambda i: (i, 0))],
        out_shape=[jax.ShapeDtypeStruct((batch * seq, hv), BF16),
                   jax.ShapeDtypeStruct(state_c.shape, F32),
                   jax.ShapeDtypeStruct((batch, hq), F32),
                   jax.ShapeDtypeStruct((batch, LANES), F32)],
        scratch_shapes=[pltpu.VMEM((rows, hv), F32)],
        input_output_aliases={8: 1} if l > 0 else {},
        compiler_params=_cparams("parallel"),
        name="mlstm_sample",
    )(q, k, v, o, gate, m0_rows, n0_rows, state_c, c_carry, *consts)
    return hg, c_new, n_new.reshape(batch, N_HEADS, DQK), m_new[:, :N_HEADS]


def _bias_table_kernel(rb_ref, sink_ref, o_ref, *, sink_col):
    n_layers, _, nq, ns = o_ref.shape
    col = lax.broadcasted_iota(jnp.int32, (nq, ns), 1)
    dist = lax.broadcasted_iota(jnp.int32, (nq, ns), 0) + WINDOW - col
    n = jnp.maximum(dist, 0)
    large = MAX_EXACT + (jnp.log(jnp.maximum(n, 1).astype(F32) / MAX_EXACT)
                         / math.log(MAX_DISTANCE / MAX_EXACT) * (N_BUCKETS - MAX_EXACT)).astype(jnp.int32)
    large = jnp.minimum(large, N_BUCKETS - 1)
    bucket = jnp.where(n < MAX_EXACT, n, large)
    valid = (dist >= 0) & (dist < WINDOW)
    for h in range(Q_HEADS):
        acc = jnp.zeros((nq, ns), F32)
        for bkt in range(N_BUCKETS):
            acc = jnp.where(bucket == bkt, rb_ref[bkt, h], acc)
        acc = jnp.where(valid, acc, NEG_INF)
        for layer in range(n_layers):
            o_ref[layer, h] = jnp.where(col == sink_col, sink_ref[layer, h], acc)


def bias_table(rel_bias, sinks, nq, ns, sink_col):
    return pl.pallas_call(
        functools.partial(_bias_table_kernel, sink_col=sink_col),
        in_specs=[pl.BlockSpec(memory_space=pltpu.SMEM), pl.BlockSpec(memory_space=pltpu.SMEM)],
        out_specs=pl.BlockSpec(memory_space=pltpu.VMEM),
        out_shape=jax.ShapeDtypeStruct((sinks.shape[0], Q_HEADS, nq, ns), F32),
        name="bias_table",
    )(rel_bias, sinks)


PROMPT_SINK_KEY = 0


def _attn_prompt_kernel(q_ref, kp_ref, kc_ref, vp_ref, vc_ref, bias_ref, o_ref):
    dk = KV_HEADS * HEAD_DIM
    kvh = range(KV_HEADS)
    sink_row = lax.broadcasted_iota(jnp.int32, (2 * WINDOW, dk), 0) == PROMPT_SINK_KEY
    lane_head = lax.broadcasted_iota(jnp.int32, (2 * WINDOW, dk), 1) // HEAD_DIM
    out_head = lax.broadcasted_iota(jnp.int32, (GROUP * WINDOW, dk), 1) // HEAD_DIM
    zero = jnp.zeros((), BF16)
    first_of_sequence = pl.program_id(1) == 0
    for j in range(q_ref.shape[1] // WINDOW):
        rows = slice(j * WINDOW, (j + 1) * WINDOW)
        before = slice((j - 1) * WINDOW, j * WINDOW)
        k_prev, v_prev = (kp_ref[0], vp_ref[0]) if j == 0 else (kc_ref[0, before], vc_ref[0, before])
        variant = jnp.where(first_of_sequence, 0, 1) if j == 0 else 1
        k_all = jnp.where(sink_row, zero, jnp.concatenate([k_prev, kc_ref[0, rows]], axis=0))
        v_all = jnp.where(sink_row, zero, jnp.concatenate([v_prev, vc_ref[0, rows]], axis=0))
        k_bd = jnp.concatenate([jnp.where(lane_head == kh, k_all, zero) for kh in kvh], axis=0)
        v_bd = jnp.concatenate([jnp.where(lane_head == kh, v_all, zero) for kh in kvh], axis=0)
        q_cat = jnp.concatenate([q_ref[0, rows, g * dk:(g + 1) * dk] for g in range(GROUP)],
                                axis=0)
        s = lax.dot_general(q_cat, k_bd, (((1,), (1,)), ((), ())), preferred_element_type=F32)
        es, invs = [], []
        for kh in kvh:
            sk = s[:, kh * 2 * WINDOW:(kh + 1) * 2 * WINDOW] + bias_ref[variant, kh]
            e = jnp.exp(sk - jnp.max(sk, axis=1, keepdims=True))
            invs.append(1.0 / jnp.sum(e, axis=1, keepdims=True))
            es.append(e.astype(BF16))
        pv = jnp.dot(jnp.concatenate(es, axis=1), v_bd, preferred_element_type=F32)
        inv = invs[KV_HEADS - 1]
        for kh in range(KV_HEADS - 1):
            inv = jnp.where(out_head == kh, invs[kh], inv)
        o = pv * inv
        for g in range(GROUP):
            o_ref[0, rows, g * dk:(g + 1) * dk] = o[g * WINDOW:(g + 1) * WINDOW].astype(o_ref.dtype)


def attn_prompt(q, k, v, table, batch, seq):
    d = Q_HEADS * HEAD_DIM
    dk = KV_HEADS * HEAD_DIM
    nq = ATTN_PROMPT_BLOCKS
    r3 = lambda a: a.reshape(batch, seq, a.shape[-1])
    prev = pl.BlockSpec((1, WINDOW, dk), lambda b, i: (b, jnp.maximum(i * nq - 1, 0), 0))
    cur = pl.BlockSpec((1, nq * WINDOW, dk), lambda b, i: (b, i, 0))
    table = table.reshape(KV_HEADS, GROUP * WINDOW, 2 * WINDOW)
    key = jnp.arange(2 * WINDOW)
    no_prev = jnp.where((key < WINDOW) & (key != PROMPT_SINK_KEY), NEG_INF, table)
    tables = jnp.stack([no_prev, table])
    out = pl.pallas_call(
        _attn_prompt_kernel,
        grid=(batch, seq // (nq * WINDOW)),
        in_specs=[pl.BlockSpec((1, nq * WINDOW, d), lambda b, i: (b, i, 0)),
                  prev, cur, prev, cur,
                  pl.BlockSpec(tables.shape, lambda b, i: (0, 0, 0, 0))],
        out_specs=pl.BlockSpec((1, nq * WINDOW, d), lambda b, i: (b, i, 0)),
        out_shape=jax.ShapeDtypeStruct((batch, seq, d), BF16),
        compiler_params=_cparams("parallel", "arbitrary"),
        name="attn_prompt",
    )(r3(q), r3(k), r3(k), r3(v), r3(v), tables)
    return out.reshape(batch * seq, d)


SAMPLE_NEW_KEYS = 16


def _attn_sample_kernel(q_ref, kct_ref, vct_ref, kn_ref, vn_ref, knt_ref, vnt_ref, bias_c_ref,
                        bias_n_ref, o_ref, *cache_out_refs, seq):
    n_b = q_ref.shape[0]
    dk = KV_HEADS * HEAD_DIM
    pairs = [(b, kh) for b in range(n_b) for kh in range(KV_HEADS)]
    kslice = lambda kh: slice(kh * HEAD_DIM, (kh + 1) * HEAD_DIM)
    head_lanes = lambda kh, g: slice((g * KV_HEADS + kh) * HEAD_DIM, (g * KV_HEADS + kh + 1) * HEAD_DIM)
    pad = jnp.zeros((SAMPLE_NEW_KEYS - seq, dk), F32)
    nt = (((1,), (1,)), ((), ()))
    dot = functools.partial(jnp.dot, preferred_element_type=F32)
    dot_nt = functools.partial(lax.dot_general, dimension_numbers=nt, preferred_element_type=F32)

    if cache_out_refs:
        lane = lax.broadcasted_iota(jnp.int32, (HEAD_DIM, WINDOW), 1)
        steps_per_block = knt_ref.shape[1] // (n_b * seq)
        first_token = (pl.program_id(0) % steps_per_block) * (n_b * seq)
        for out_ref, old_ref, newt_ref in zip(cache_out_refs, (kct_ref, vct_ref), (knt_ref, vnt_ref)):
            for b, kh in pairs:
                moved = pltpu.roll(old_ref[b, kh], WINDOW - seq, 1)
                shift = (2 * WINDOW - seq - first_token - b * seq) % WINDOW
                fresh = pltpu.roll(newt_ref[kslice(kh), :], shift, 1)
                out_ref[b, kh] = jnp.where(lane >= WINDOW - seq, fresh, moved)

    qss = [jnp.concatenate([q_ref[b, :, head_lanes(kh, g)] for g in range(GROUP)],
                           axis=0).astype(BF16) for b, kh in pairs]
    kcts = [kct_ref[b, kh].astype(BF16) for b, kh in pairs]
    vcts = [vct_ref[b, kh].astype(BF16) for b, kh in pairs]
    kns = [jnp.concatenate([kn_ref[b], pad], axis=0).astype(BF16) for b in range(n_b)]
    vns = [jnp.concatenate([vn_ref[b], pad], axis=0).astype(BF16) for b in range(n_b)]
    s_cs = [dot(qss[i], kcts[i]) for i in range(len(pairs))]
    s_ns = [dot_nt(qss[i], kns[b][:, kslice(kh)]) for i, (b, kh) in enumerate(pairs)]
    e_cs, e_ns, denoms = [], [], []
    for i, (b, kh) in enumerate(pairs):
        s_c = s_cs[i] + bias_c_ref[kh]
        s_n = s_ns[i] + bias_n_ref[kh]
        m = jnp.maximum(jnp.max(s_c, axis=1, keepdims=True), jnp.max(s_n, axis=1, keepdims=True))
        e_c, e_n = jnp.exp(s_c - m), jnp.exp(s_n - m)
        denoms.append(jnp.sum(e_c, axis=1, keepdims=True) + jnp.sum(e_n, axis=1, keepdims=True))
        e_cs.append(e_c.astype(BF16))
        e_ns.append(e_n.astype(BF16))
    pv_cs = [dot_nt(e_cs[i], vcts[i]) for i in range(len(pairs))]
    pv_ns = [dot(e_ns[i], vns[b][:, kslice(kh)]) for i, (b, kh) in enumerate(pairs)]
    for i, (b, kh) in enumerate(pairs):
        o = (pv_cs[i] + pv_ns[i]) / denoms[i]
        for g in range(GROUP):
            o_ref[b, :, head_lanes(kh, g)] = o[g * seq:(g + 1) * seq].astype(o_ref.dtype)


def attn_sample(q, k_new, v_new, cache_k, cache_v, table, batch, seq, write_cache):
    d = Q_HEADS * HEAD_DIM
    dk = KV_HEADS * HEAD_DIM
    nbk = ATTN_SAMPLE_BLOCK
    assert WINDOW % (nbk * seq) == 0
    new = pl.BlockSpec((nbk, seq, dk), lambda i: (i, 0, 0))
    old = pl.BlockSpec((nbk, KV_HEADS, HEAD_DIM, WINDOW), lambda i: (i, 0, 0, 0))
    newt = pl.BlockSpec((dk, WINDOW), lambda i: (0, i * nbk * seq // WINDOW))
    to_t = lambda c: jnp.transpose(c, (0, 2, 3, 1))
    table = table.reshape(KV_HEADS, GROUP * seq, 2 * WINDOW)
    n_cache = 2 if write_cache else 0
    out = pl.pallas_call(
        functools.partial(_attn_sample_kernel, seq=seq),
        grid=(batch // nbk,),
        in_specs=[pl.BlockSpec((nbk, seq, d), lambda i: (i, 0, 0)),
                  old, old, new, new, newt, newt,
                  pl.BlockSpec((KV_HEADS, GROUP * seq, WINDOW), lambda i: (0, 0, 0)),
                  pl.BlockSpec((KV_HEADS, GROUP * seq, SAMPLE_NEW_KEYS), lambda i: (0, 0, 0))],
        out_specs=[pl.BlockSpec((nbk, seq, d), lambda i: (i, 0, 0))] + [old] * n_cache,
        out_shape=[jax.ShapeDtypeStruct((batch, seq, d), F32)]
                  + [jax.ShapeDtypeStruct((batch, KV_HEADS, HEAD_DIM, WINDOW), F32)] * n_cache,
        compiler_params=_cparams("parallel"),
        name="attn_sample",
    )(q.reshape(batch, seq, d), to_t(cache_k), to_t(cache_v),
      k_new.reshape(batch, seq, dk), v_new.reshape(batch, seq, dk), k_new.T, v_new.T,
      table[:, :, :WINDOW], table[:, :, WINDOW:WINDOW + SAMPLE_NEW_KEYS])
    return (out[0].reshape(batch * seq, d),) + tuple(jnp.transpose(c, (0, 3, 1, 2)) for c in out[1:])


def _trunk(x, state, cache, w, tm, tm_mlp):
    batch, seq, d = x.shape
    x = x.reshape(batch * seq, d)
    hq, hv = N_HEADS * DQK, N_HEADS * DV
    depth = w["norm_mix"].shape[0]
    n_a = w["w_in"].shape[0]
    prompt = state is None
    act = BF16 if prompt else F32
    cs, ns, ms = [], [], []
    c_stack = None if prompt else state[0]
    for l in range(depth):
        if l < n_a:
            gates = (w["b_igate"][l], w["b_fgate"][l], w["mlstm_norm"][l])
            q, k, v, o, gate = mlstm_inproj(x, w["norm_mix"], w["w_in"], l, tm, prompt, act)
            if prompt:
                a, c_new, n_new, m_new = mlstm_prompt(q, k, v, o, gate, *gates, batch, seq)
                cs.append(c_new)
            else:
                a, c_stack, n_new, m_new = mlstm_sample(q, k, v, o, gate, *gates, state[0], c_stack,
                                                        l, state[1][l], state[2][l], batch, seq)
            ns.append(n_new); ms.append(m_new)
            w_o, lo = w["w_mlstm_out"], l
        else:
            j = l - n_a
            q_proj = (w["norm_mix"], l, w["w_q"], j, ((0, Q_HEADS * HEAD_DIM),), (act,))
            if j == 0:
                dk = KV_HEADS * HEAD_DIM
                kv_proj = (w["kv_norm"], 0, w["w_kv"], 0, ((0, dk), (dk, dk), (0, dk), (dk, dk)),
                           (F32, F32, BF16, BF16))
                (k_new, v_new, k16, v16), (q,) = norm_matmul(x, (kv_proj, q_proj), tm_mlp)
                tables = bias_table(w["rel_bias"], w["attn_sinks"], WINDOW if prompt else seq,
                                    2 * WINDOW, PROMPT_SINK_KEY if prompt else WINDOW + seq)
            else:
                ((q,),) = norm_matmul(x, (q_proj,), tm_mlp)
            if prompt:
                a = attn_prompt(q, k16, v16, tables[j], batch, seq)
            elif j == 0:
                a, win_k, win_v = attn_sample(q, k_new, v_new, cache[0], cache[1], tables[j],
                                              batch, seq, True)
            else:
                (a,) = attn_sample(q, k_new, v_new, cache[0], cache[1], tables[j],
                                   batch, seq, False)
            w_o, lo = w["w_attn_out"], j
        x = mlp(a, w_o, lo, x, w["norm_ffn"], w["w_up"], w["w_down"], l, w["final_norm"],
                l == depth - 1, tm_mlp, MLP_FF_BLOCK if prompt else MLP_FF_CHUNK)
    if prompt:
        dk = KV_HEADS * HEAD_DIM
        win_k = k_new.reshape(batch, seq, dk)[:, -WINDOW:]
        win_v = v_new.reshape(batch, seq, dk)[:, -WINDOW:]
        c_stack = jnp.stack(cs)
    shp = (batch, WINDOW, KV_HEADS, HEAD_DIM)
    return (x.reshape(batch, seq, d), c_stack, jnp.stack(ns), jnp.stack(ms),
            win_k.reshape(shp), win_v.reshape(shp))


def kernel(x_prompt, x_sample, state_C, state_n, state_m, cache_k, cache_v, norm_mix, norm_ffn,
           w_mlstm_in, b_igate, b_fgate, mlstm_norm, w_mlstm_out, kv_norm, w_kv, w_q, attn_sinks,
           w_attn_out, rel_bias, w_up, w_down, final_norm):
    n_b, d = w_q.shape[0], w_q.shape[1]
    heads = (KV_HEADS, GROUP, HEAD_DIM)
    w_q_perm = (w_q * HEAD_DIM ** -0.5).astype(BF16).reshape((n_b, d) + heads)
    w_q_perm = w_q_perm.transpose(0, 1, 3, 2, 4).reshape(n_b, d, d)
    w_ao_perm = w_attn_out.astype(BF16).reshape((n_b,) + heads + (d,))
    w_ao_perm = w_ao_perm.transpose(0, 2, 1, 3, 4).reshape(n_b, d, d)
    w = dict(norm_mix=norm_mix[:, None, :], norm_ffn=norm_ffn[:, None, :],
             w_in=w_mlstm_in.astype(BF16), b_igate=b_igate, b_fgate=b_fgate,
             mlstm_norm=mlstm_norm, w_mlstm_out=w_mlstm_out.astype(BF16),
             kv_norm=kv_norm[None, None, :], w_kv=w_kv.astype(BF16)[None],
             w_q=w_q_perm, attn_sinks=attn_sinks, w_attn_out=w_ao_perm, rel_bias=rel_bias,
             w_up=w_up.astype(BF16), w_down=w_down.astype(BF16), final_norm=final_norm)
    y_p, c_p, n_p, m_p, k_p, v_p = _trunk(x_prompt, None, None, w, 512, 1024)
    y_s, c_s, n_s, m_s, k_s, v_s = _trunk(x_sample, (state_C, state_n, state_m),
                                          (cache_k, cache_v), w, 512, 512)
    return (y_p, y_s, c_p, n_p, m_p, k_p, v_p, c_s, n_s, m_s, k_s, v_s)
```

```python
import functools
import math

import jax
import jax.numpy as jnp
from jax import lax
from jax.experimental import pallas as pl
from jax.experimental.pallas import tpu as pltpu

F32 = jnp.float32
BF16 = jnp.bfloat16
EPS = 1e-6
LOG2_E = 1.4426950408889634
NEG_INF = float("-inf")

N_HEADS = 8
DQK = 64
DV = 128
Q_HEADS = 16
KV_HEADS = 4
GROUP = Q_HEADS // KV_HEADS
HEAD_DIM = 64
WINDOW = 128
N_BUCKETS = 32
MAX_EXACT = N_BUCKETS // 2
MAX_DISTANCE = 128

LANES = 128
VMEM_LIMIT = 48 * 1024 * 1024
MLP_VMEM_LIMIT = 56 * 1024 * 1024

PROMPT_CHUNK = 128
PROMPT_HEAD_GROUP = 4
PROMPT_BLOCK = 512
SAMPLE_BATCH_BLOCK = 8
ATTN_SAMPLE_BLOCK = 8
ATTN_PROMPT_BLOCKS = 8
MLP_ROW_CHUNK = 512
MLP_FF_BLOCK = 2048
MLP_FF_CHUNK = 1024


def _cparams(*sem, vmem_limit=VMEM_LIMIT):
    return pltpu.CompilerParams(dimension_semantics=sem, vmem_limit_bytes=vmem_limit)


def _rms(x, g):
    return x * lax.rsqrt(jnp.mean(x * x, axis=-1, keepdims=True) + EPS) * g


def _split3(x):
    hi = x.astype(BF16)
    r = x - hi.astype(F32)
    mid = r.astype(BF16)
    lo = (r - mid.astype(F32)).astype(BF16)
    return hi, mid, lo


def _dot01(x, onehot):
    hi, mid, lo = _split3(x)
    d = lambda a: jnp.dot(a, onehot, preferred_element_type=F32)
    return d(hi) + d(mid) + d(lo)


def _dot01_left(onehot, x):
    hi, mid, lo = _split3(x)
    d = lambda a: jnp.dot(onehot, a, preferred_element_type=F32)
    return d(hi) + d(mid) + d(lo)


def _log_sigmoid(x):
    return jnp.minimum(x, 0.0) - jnp.log1p(jnp.exp(-jnp.abs(x)))


def _sigmoid(x):
    return 1.0 / (1.0 + jnp.exp(-x))


def _norm_mm_kernel(x_ref, *refs, splits):
    n_proj = len(splits)
    out_refs = iter(refs[2 * n_proj:])
    x = x_ref[...]
    xhat = x * lax.rsqrt(jnp.mean(x * x, axis=-1, keepdims=True) + EPS)
    ys = [(xhat * refs[2 * p][...]).astype(BF16) for p in range(n_proj)]
    rs = [jnp.dot(ys[p], refs[2 * p + 1][...], preferred_element_type=F32) for p in range(n_proj)]
    for p in range(n_proj):
        for off, n in splits[p]:
            o_ref = next(out_refs)
            o_ref[...] = rs[p][:, off:off + n].astype(o_ref.dtype)


def _layer_spec(arr, layer):
    idx = (layer,) + (0,) * (arr.ndim - 1)
    return pl.BlockSpec((None,) + arr.shape[1:], lambda *_: idx)


def norm_matmul(x, projections, tm):
    m, d = x.shape
    assert m % tm == 0
    params, param_specs, out_specs, out_shapes = [], [], [], []
    for g, lg, w, lw, splits, dtypes in projections:
        assert all(off + n <= w.shape[-1] for off, n in splits)
        params += [g, w]
        param_specs += [_layer_spec(g, lg), _layer_spec(w, lw)]
        out_specs += [pl.BlockSpec((tm, n), lambda i: (i, 0)) for _, n in splits]
        out_shapes += [jax.ShapeDtypeStruct((m, n), dt) for (_, n), dt in zip(splits, dtypes)]
    outs = pl.pallas_call(
        functools.partial(_norm_mm_kernel, splits=tuple(p[4] for p in projections)),
        grid=(m // tm,),
        in_specs=[pl.BlockSpec((tm, d), lambda i: (i, 0))] + param_specs,
        out_specs=out_specs,
        out_shape=out_shapes,
        compiler_params=_cparams("parallel"),
        name="norm_matmul",
    )(x, *params)
    outs, grouped = list(outs), []
    for p in projections:
        grouped.append(outs[:len(p[4])])
        outs = outs[len(p[4]):]
    return grouped


def _mlp_kernel(a_ref, wo_ref, x_ref, g_ref, wu_ref, wd_ref, gf_ref, o_ref, xn_ref, *,
                final_norm, row_chunk, ff_chunk):
    j = pl.program_id(1)
    tm = x_ref.shape[0]

    @pl.when(j == 0)
    def _():
        for r in range(0, tm, row_chunk):
            rows = slice(r, r + row_chunk)
            x = x_ref[rows, :] + jnp.dot(a_ref[rows, :].astype(BF16), wo_ref[...],
                                         preferred_element_type=F32)
            xn_ref[rows, :] = _rms(x, g_ref[...]).astype(BF16)
            o_ref[rows, :] = x

    for c in range(0, wu_ref.shape[1], ff_chunk):
        h = jnp.dot(xn_ref[...], wu_ref[:, c:c + ff_chunk], preferred_element_type=F32)
        h = jnp.maximum(h, 0.0)
        o_ref[...] += jnp.dot((h * h).astype(BF16), wd_ref[c:c + ff_chunk, :],
                              preferred_element_type=F32)

    if final_norm:
        @pl.when(j == pl.num_programs(1) - 1)
        def _():
            o_ref[...] = _rms(o_ref[...], gf_ref[...])


def mlp(a, w_o, lo, x, g, w_up, w_down, l, g_final, final_norm, tm, tf):
    m, d = x.shape
    ff = w_up.shape[-1]
    return pl.pallas_call(
        functools.partial(_mlp_kernel, final_norm=final_norm, row_chunk=min(tm, MLP_ROW_CHUNK),
                          ff_chunk=MLP_FF_CHUNK),
        grid=(m // tm, ff // tf),
        in_specs=[pl.BlockSpec((tm, d), lambda i, j: (i, 0)),
                  _layer_spec(w_o, lo),
                  pl.BlockSpec((tm, d), lambda i, j: (i, 0)),
                  _layer_spec(g, l),
                  pl.BlockSpec((None, d, tf), lambda i, j: (l, 0, j)),
                  pl.BlockSpec((None, tf, d), lambda i, j: (l, j, 0)),
                  pl.BlockSpec((1, d), lambda i, j: (0, 0))],
        out_specs=pl.BlockSpec((tm, d), lambda i, j: (i, 0)),
        out_shape=jax.ShapeDtypeStruct((m, d), F32),
        scratch_shapes=[pltpu.VMEM((tm, d), BF16)],
        compiler_params=_cparams("parallel", "arbitrary",
                                 vmem_limit=MLP_VMEM_LIMIT if tf > MLP_FF_CHUNK else VMEM_LIMIT),
        name="mlp",
    )(a, w_o, x, g, w_up, w_down, g_final.reshape(1, d))


def _mlstm_inproj_kernel(x_ref, g_ref, w_ref, q_ref, k_ref, v_ref, o_ref, gate_ref, *, k_transposed):
    hq, hv = N_HEADS * DQK, N_HEADS * DV
    tm = x_ref.shape[0]
    y = _rms(x_ref[...], g_ref[...]).astype(BF16)
    r = jnp.dot(y, w_ref[...], preferred_element_type=F32)
    q_ref[...] = r[:, :hq].astype(q_ref.dtype)
    k = r[:, hq:2 * hq] * DQK ** -0.5
    k_ref[...] = (k.T if k_transposed else k).astype(k_ref.dtype)
    v_ref[...] = r[:, 2 * hq:2 * hq + hv].astype(v_ref.dtype)
    o_ref[...] = r[:, 2 * hq + hv:2 * hq + 2 * hv]
    gate_ref[...] = jnp.concatenate(
        [r[:, 2 * hq + 2 * hv:], jnp.zeros((tm, LANES - 2 * N_HEADS), F32)], axis=1)


def mlstm_inproj(x, g, w, l, tm, k_transposed, act):
    m, d = x.shape
    hq, hv = N_HEADS * DQK, N_HEADS * DV
    assert w.shape[-1] == 2 * hq + 2 * hv + 2 * N_HEADS
    row = lambda width: pl.BlockSpec((tm, width), lambda i: (i, 0))
    k_spec = pl.BlockSpec((hq, tm), lambda i: (0, i)) if k_transposed else row(hq)
    k_shape = (hq, m) if k_transposed else (m, hq)
    return pl.pallas_call(
        functools.partial(_mlstm_inproj_kernel, k_transposed=k_transposed),
        grid=(m // tm,),
        in_specs=[row(d), _layer_spec(g, l), _layer_spec(w, l)],
        out_specs=[row(hq), k_spec, row(hv), row(hv), row(LANES)],
        out_shape=[jax.ShapeDtypeStruct((m, hq), act), jax.ShapeDtypeStruct(k_shape, act),
                   jax.ShapeDtypeStruct((m, hv), act), jax.ShapeDtypeStruct((m, hv), F32),
                   jax.ShapeDtypeStruct((m, LANES), F32)],
        compiler_params=_cparams("parallel"),
        name="mlstm_inproj",
    )(x, g, w)


def _chunk_scan(x, pos, op, fill, length):
    k = 1
    while k < length:
        x = op(x, jnp.where(pos >= k, pltpu.roll(x, k, 0), fill))
        k *= 2
    return x


def _mlstm_prompt_kernel(q_ref, kt_ref, v_ref, o_ref, gate_ref, bi_ref, bf_ref, gain_ref,
                         hg_ref, cx_out_ref, m_out_ref, cx_s, m_s, *, chunk):
    t_blk = q_ref.shape[1]
    n_chunks = t_blk // chunk
    step = pl.program_id(1)

    @pl.when(step == 0)
    def _():
        cx_s[...] = jnp.zeros_like(cx_s)
        m_s[...] = jnp.zeros_like(m_s)

    gates = gate_ref[0]
    li = gates + bi_ref[...]
    lf = _log_sigmoid(pltpu.roll(gates, LANES - N_HEADS, 1) + bf_ref[...])
    pos = lax.broadcasted_iota(jnp.int32, (t_blk, 1), 0) % chunk
    b = _chunk_scan(lf, pos, jnp.add, 0.0, chunk)
    c = li - b
    cm = _chunk_scan(c, pos, jnp.maximum, NEG_INF, chunk)

    m_prev = m_s[...]
    xs, w_inters, e_negms, wks = [], [], [], []
    for ck in range(n_chunks):
        sl = slice(ck * chunk, (ck + 1) * chunk)
        m_t = b[sl] + jnp.maximum(m_prev, cm[sl])
        m_new = m_t[chunk - 1:chunk]
        b_last = b[(ck + 1) * chunk - 1:(ck + 1) * chunk]
        xs.append((b[sl] - m_t) * LOG2_E)
        w_inters.append(jnp.exp(b[sl] + m_prev - m_t))
        e_negms.append(jnp.exp(-m_t))
        wks.append(jnp.exp(c[sl] + (b_last - m_new)))
        m_prev = m_new
    m_s[...] = m_prev
    c_t = (c * LOG2_E).T
    wk_t = jnp.concatenate(wks, axis=0).T

    ri = lax.broadcasted_iota(jnp.int32, (chunk, chunk), 0)
    ci = lax.broadcasted_iota(jnp.int32, (chunk, chunk), 1)
    causal = ci <= ri
    ones_blk = jnp.ones((chunk, DV), BF16)
    mean_sq = jnp.full((DV, DV), 1.0 / DV, BF16)
    heads = range(N_HEADS)
    dot = functools.partial(jnp.dot, preferred_element_type=F32)

    cxs = {h: cx_s[h] for h in heads}
    head_groups = [heads[g:g + PROMPT_HEAD_GROUP] for g in range(0, N_HEADS, PROMPT_HEAD_GROUP)]
    for ck in range(n_chunks):
        sl = slice(ck * chunk, (ck + 1) * chunk)
        for hs in head_groups:
            qs = {h: q_ref[0, sl, h * DQK:(h + 1) * DQK] for h in hs}
            kts = {h: kt_ref[h * DQK:(h + 1) * DQK, sl] for h in hs}
            vs = {h: v_ref[0, sl, h * DV:(h + 1) * DV] for h in hs}
            cx16s = {h: cxs[h].astype(BF16) for h in hs}
            kws = {h: (kts[h].astype(F32) * wk_t[h:h + 1, sl]).astype(BF16) for h in hs}
            vones = {h: jnp.concatenate([vs[h], ones_blk], axis=1) for h in hs}
            ss = {h: dot(qs[h], kts[h]) for h in hs}
            qcs = {h: dot(qs[h], cx16s[h]) for h in hs}
            dcs = {h: dot(kws[h], vones[h]) for h in hs}
            for h in hs:
                cxs[h] = w_inters[ck][chunk - 1:chunk, h:h + 1] * cxs[h] + dcs[h]
            a16s = {}
            for h in hs:
                dm = xs[ck][:, h:h + 1] + c_t[h:h + 1, sl]
                a16s[h] = (ss[h] * jnp.exp2(jnp.where(causal, dm, NEG_INF))).astype(BF16)
            avs = {h: dot(a16s[h], vones[h]) for h in hs}
            hhs = {}
            for h in hs:
                wi = jnp.broadcast_to(w_inters[ck][:, h:h + 1], (chunk, DV))
                en = jnp.broadcast_to(e_negms[ck][:, h:h + 1], (chunk, DV))
                den = avs[h][:, DV:] + wi * qcs[h][:, DV:]
                inv = 1.0 / jnp.maximum(jnp.abs(den), en)
                hhs[h] = (avs[h][:, :DV] + qcs[h][:, :DV] * wi) * inv
            sq16s = {h: (hhs[h] * hhs[h]).astype(BF16) for h in hs}
            mss = {h: dot(sq16s[h], mean_sq) for h in hs}
            for h in hs:
                hn = hhs[h] * lax.rsqrt(mss[h] + EPS)
                hn = hn * gain_ref[:, h * DV:(h + 1) * DV]
                og = _sigmoid(o_ref[0, sl, h * DV:(h + 1) * DV])
                hg_ref[0, sl, h * DV:(h + 1) * DV] = (og * hn).astype(hg_ref.dtype)
    for h in heads:
        cx_s[h] = cxs[h]

    @pl.when(step == pl.num_programs(1) - 1)
    def _():
        cx_out_ref[0] = cx_s[...]
        m_out_ref[0] = m_s[...]


def mlstm_prompt(q, kt, v, o, gate, b_i, b_f, gain, batch, seq):
    hq, hv = N_HEADS * DQK, N_HEADS * DV
    t = PROMPT_BLOCK
    nblk = seq // t
    r3 = lambda a: a.reshape(batch, seq, a.shape[-1])
    pad8 = lambda a: jnp.pad(a.reshape(1, N_HEADS), ((0, 0), (0, LANES - N_HEADS)))
    tok = lambda w: pl.BlockSpec((1, t, w), lambda bb, s: (bb, s, 0))
    cst = lambda w: pl.BlockSpec((1, w), lambda bb, s: (0, 0))
    hg, cx, m_new = pl.pallas_call(
        functools.partial(_mlstm_prompt_kernel, chunk=PROMPT_CHUNK),
        grid=(batch, nblk),
        in_specs=[tok(hq), pl.BlockSpec((hq, t), lambda bb, s: (0, bb * nblk + s)),
                  tok(hv), tok(hv), tok(LANES), cst(LANES), cst(LANES), cst(hv)],
        out_specs=[tok(hv),
                   pl.BlockSpec((1, N_HEADS, DQK, 2 * DV), lambda bb, s: (bb, 0, 0, 0)),
                   pl.BlockSpec((1, 1, LANES), lambda bb, s: (bb, 0, 0))],
        out_shape=[jax.ShapeDtypeStruct((batch, seq, hv), BF16),
                   jax.ShapeDtypeStruct((batch, N_HEADS, DQK, 2 * DV), F32),
                   jax.ShapeDtypeStruct((batch, 1, LANES), F32)],
        scratch_shapes=[pltpu.VMEM((N_HEADS, DQK, 2 * DV), F32), pltpu.VMEM((1, LANES), F32)],
        compiler_params=_cparams("parallel", "arbitrary"),
        name="mlstm_prompt",
    )(r3(q), kt, r3(v), r3(o), r3(gate), pad8(b_i), pad8(b_f), gain.reshape(1, hv))
    return hg.reshape(batch * seq, hv), cx[..., :DV], cx[..., DV], m_new[:, 0, :N_HEADS]


def _mlstm_sample_kernel(q_ref, k_ref, v_ref, o_ref, gate_ref, m0_ref, n0_ref, c0_ref, c_carry_ref,
                         bi_ref, bf_ref, gain_ref, seg64_ref, seg128_ref, e64_ref, e128_ref,
                         hg_ref, c_out_ref, n_out_ref, m_out_ref,
                         qc_s, *, seq):
    del c_carry_ref
    spans_layers = len(c_out_ref.shape) == 5
    c_new_ref = c_out_ref.at[0] if spans_layers else c_out_ref
    if spans_layers:
        for later in range(1, c_out_ref.shape[0]):
            c_out_ref[later] = jnp.zeros(c_out_ref.shape[1:], F32)
    rows = q_ref.shape[0]
    n_b = rows // seq
    tpos = lax.broadcasted_iota(jnp.int32, (rows, 1), 0) % seq
    dot = functools.partial(jnp.dot, preferred_element_type=F32)

    def shift(x, d):
        return x if d == 0 else pltpu.roll(x, d, 0)

    def unshift(x, d):
        return x if d == 0 else pltpu.roll(x, rows - d, 0)

    def split_cat(x):
        return jnp.concatenate(_split3(x), axis=1)


    q = q_ref[...].astype(BF16).astype(F32)
    k = k_ref[...].astype(BF16).astype(F32)
    v = v_ref[...].astype(BF16).astype(F32)

    units = [(p, h) for p in range(n_b // 2) for h in range(N_HEADS)]
    prow = lambda p: slice(p * 2 * seq, (p + 1) * 2 * seq)
    q_pairs = [q[prow(p), h * DQK:(h + 1) * DQK].astype(BF16) for p, h in units]
    c_cats = [jnp.concatenate([c0_ref[2 * p, h], c0_ref[2 * p + 1, h]], axis=1).astype(BF16)
              for p, h in units]
    qcs = [dot(q_pairs[i], c_cats[i]) for i in range(len(units))]

    gates = gate_ref[...]
    li = gates + bi_ref[...]
    lf = _log_sigmoid(pltpu.roll(gates, LANES - N_HEADS, 1) + bf_ref[...])
    b = lf
    for d in range(1, seq):
        b = b + jnp.where(tpos >= d, shift(lf, d), 0.0)
    inter = b + m0_ref[...]
    dvals = []
    m_t = inter
    for d in range(seq):
        dd = jnp.where(tpos >= d, b - shift(b, d) + shift(li, d), NEG_INF)
        dvals.append(dd)
        m_t = jnp.maximum(m_t, dd)
    w_inter = jnp.exp(inter - m_t)
    w_intra = [jnp.exp(dd - m_t) for dd in dvals]
    last = lambda x: functools.reduce(
        lambda acc, d: jnp.where(tpos == seq - 1 - d, unshift(x, d), acc), range(1, seq), x)
    wk = jnp.exp(last(b) - b + li - last(m_t))

    qk_in = [split_cat(q * shift(k, d)) for d in range(seq)]
    qn_in = split_cat(q * n0_ref[...].astype(BF16).astype(F32))
    wk_in, wi_in = split_cat(wk), split_cat(w_inter)
    seg64, e64, e128 = seg64_ref[...], e64_ref[...], e128_ref[...]
    qks = [dot(x, seg64) for x in qk_in]
    qn = dot(qn_in, seg64)
    wk_exp = dot(wk_in, e64)
    decay64 = dot(wi_in, e64)
    decay_exp = dot(wi_in, e128)

    a_s = [qks[d] * w_intra[d] for d in range(seq)]
    e128_1 = e128[:LANES]
    a_exps = [dot(a.astype(BF16), e128_1) for a in a_s]
    den = w_inter * qn
    num = jnp.zeros((rows, N_HEADS * DV), F32)
    for d in range(seq):
        den = den + a_s[d]
        num = num + a_exps[d] * shift(v, d)

    kw = k * wk_exp
    ksum = kw
    for d in range(1, seq):
        ksum = ksum + shift(kw, d)
    n_rows = decay64 * n0_ref[...] + ksum
    own = ((lax.broadcasted_iota(jnp.int32, (2 * seq, 2 * DV), 0) < seq)
           == (lax.broadcasted_iota(jnp.int32, (2 * seq, 2 * DV), 1) < DV))
    kw_pairs = [kw[prow(p), h * DQK:(h + 1) * DQK].astype(BF16) for p, h in units]
    v2s = []
    for p, h in units:
        v_pair = v[prow(p), h * DV:(h + 1) * DV]
        v2s.append(jnp.where(own, jnp.concatenate([v_pair, v_pair], axis=1), 0.0).astype(BF16))
    sel3 = (lax.broadcasted_iota(jnp.int32, (n_b, 3 * rows), 1) % rows
            == lax.broadcasted_iota(jnp.int32, (n_b, 3 * rows), 0) * seq + (seq - 1))
    sel3 = jnp.where(sel3, 1.0, 0.0).astype(BF16)
    n_in = jnp.concatenate(_split3(n_rows), axis=0)
    m_in = jnp.concatenate(_split3(m_t), axis=0)
    d_cs = [lax.dot_general(kw_pairs[i], v2s[i], (((0,), (0,)), ((), ())),
                            preferred_element_type=F32) for i in range(len(units))]
    n_out_ref[...] = dot(sel3, n_in)
    m_out_ref[...] = dot(sel3, m_in)

    first = lax.broadcasted_iota(jnp.int32, (2 * seq, DV), 0) < seq
    for i, (p, h) in enumerate(units):
        r0 = p * 2 * seq
        qc_s[prow(p), h * DV:(h + 1) * DV] = jnp.where(first, qcs[i][:, :DV], qcs[i][:, DV:])
        dec_a = decay_exp[r0 + seq - 1:r0 + seq, h * DV:(h + 1) * DV]
        dec_b = decay_exp[r0 + 2 * seq - 1:r0 + 2 * seq, h * DV:(h + 1) * DV]
        c_new_ref[2 * p, h] = dec_a * c0_ref[2 * p, h] + d_cs[i][:, :DV]
        c_new_ref[2 * p + 1, h] = dec_b * c0_ref[2 * p + 1, h] + d_cs[i][:, DV:]

    num = num + qc_s[...] * decay_exp
    inv_scale = 1.0 / jnp.maximum(jnp.abs(den), jnp.exp(-m_t))
    hh = num * dot(split_cat(inv_scale), e128)
    ms = dot(split_cat(hh * hh), seg128_ref[...]) * (1.0 / DV)
    hn = hh * dot(split_cat(lax.rsqrt(ms + EPS)), e128) * gain_ref[...]
    hg_ref[...] = (_sigmoid(o_ref[...]) * hn).astype(hg_ref.dtype)


def mlstm_sample(q, k, v, o, gate, b_i, b_f, gain, state_c, c_carry, l, n0, m0, batch, seq):
    hq, hv = N_HEADS * DQK, N_HEADS * DV
    c_blk = pl.BlockSpec((None, SAMPLE_BATCH_BLOCK, N_HEADS, DQK, DV), lambda i: (l, i, 0, 0, 0))
    c_all = pl.BlockSpec((state_c.shape[0], SAMPLE_BATCH_BLOCK, N_HEADS, DQK, DV),
                         lambda i: (0, i, 0, 0, 0))
    rows = SAMPLE_BATCH_BLOCK * seq
    pad8 = lambda a: jnp.pad(a.reshape(1, N_HEADS), ((0, 0), (0, LANES - N_HEADS)))
    m0_rows = jnp.pad(jnp.repeat(m0, seq, axis=0), ((0, 0), (0, LANES - N_HEADS)))
    n0_rows = jnp.repeat(n0.reshape(batch, hq), seq, axis=0)
    lane = jnp.arange(LANES)
    seg64 = (jnp.arange(hq)[:, None] // DQK == lane[None, :]).astype(BF16)
    seg128 = (jnp.arange(hv)[:, None] // DV == lane[None, :]).astype(BF16)
    tok = lambda w: pl.BlockSpec((rows, w), lambda i: (i, 0))
    cst = lambda a: pl.BlockSpec(a.shape, lambda i: (0,) * a.ndim)
    x3 = lambda a: jnp.tile(a, (3, 1))
    consts = [pad8(b_i), pad8(b_f), gain.reshape(1, hv),
              x3(seg64), x3(seg128), x3(seg64.T), x3(seg128.T)]
    hg, c_new, n_new, m_new = pl.pallas_call(
        functools.partial(_mlstm_sample_kernel, seq=seq),
        grid=(batch // SAMPLE_BATCH_BLOCK,),
        in_specs=[tok(hq), tok(hq), tok(hv), tok(hv), tok(LANES), tok(LANES), tok(hq),
                  c_blk, pl.BlockSpec(memory_space=pl.ANY)]
                 + [cst(a) for a in consts],
        out_specs=[tok(hv), c_all if l == 0 else c_blk,
                   pl.BlockSpec((SAMPLE_BATCH_BLOCK, hq), lambda i: (i, 0)),
                   pl.BlockSpec((SAMPLE_BATCH_BLOCK, LANES), lambda i: (i, 0))],
        out_shape=[jax.ShapeDtypeStruct((batch * seq, hv), BF16),
                   jax.ShapeDtypeStruct(state_c.shape, F32),
                   jax.ShapeDtypeStruct((batch, hq), F32),
                   jax.ShapeDtypeStruct((batch, LANES), F32)],
        scratch_shapes=[pltpu.VMEM((rows, hv), F32)],
        input_output_aliases={8: 1} if l > 0 else {},
        compiler_params=_cparams("parallel"),
        name="mlstm_sample",
    )(q, k, v, o, gate, m0_rows, n0_rows, state_c, c_carry, *consts)
    return hg, c_new, n_new.reshape(batch, N_HEADS, DQK), m_new[:, :N_HEADS]


def _bias_table_kernel(rb_ref, sink_ref, o_ref, *, sink_col):
    n_layers, _, nq, ns = o_ref.shape
    col = lax.broadcasted_iota(jnp.int32, (nq, ns), 1)
    dist = lax.broadcasted_iota(jnp.int32, (nq, ns), 0) + WINDOW - col
    n = jnp.maximum(dist, 0)
    large = MAX_EXACT + jnp.floor(
        jnp.log(jnp.maximum(n, 1).astype(F32) / MAX_EXACT)
        / math.log(MAX_DISTANCE / MAX_EXACT) * (N_BUCKETS - MAX_EXACT)).astype(jnp.int32)
    large = jnp.minimum(large, N_BUCKETS - 1)
    bucket = jnp.where(n < MAX_EXACT, n, large)
    valid = (dist >= 0) & (dist < WINDOW)
    for h in range(Q_HEADS):
        acc = jnp.zeros((nq, ns), F32)
        for bkt in range(N_BUCKETS):
            acc = jnp.where(bucket == bkt, rb_ref[bkt, h], acc)
        acc = jnp.where(valid, acc, NEG_INF)
        for layer in range(n_layers):
            o_ref[layer, h] = jnp.where(col == sink_col, sink_ref[layer, h], acc)


def bias_table(rel_bias, sinks, nq, ns, sink_col):
    return pl.pallas_call(
        functools.partial(_bias_table_kernel, sink_col=sink_col),
        in_specs=[pl.BlockSpec(memory_space=pltpu.SMEM), pl.BlockSpec(memory_space=pltpu.SMEM)],
        out_specs=pl.BlockSpec(memory_space=pltpu.VMEM),
        out_shape=jax.ShapeDtypeStruct((sinks.shape[0], Q_HEADS, nq, ns), F32),
        name="bias_table",
    )(rel_bias, sinks)


PROMPT_SINK_KEY = 0


def _attn_prompt_kernel(q_ref, kp_ref, kc_ref, vp_ref, vc_ref, bias_ref, o_ref):
    dk = KV_HEADS * HEAD_DIM
    kvh = range(KV_HEADS)
    sink_row = lax.broadcasted_iota(jnp.int32, (2 * WINDOW, dk), 0) == PROMPT_SINK_KEY
    lane_head = lax.broadcasted_iota(jnp.int32, (2 * WINDOW, dk), 1) // HEAD_DIM
    out_head = lax.broadcasted_iota(jnp.int32, (GROUP * WINDOW, dk), 1) // HEAD_DIM
    zero = jnp.zeros((), BF16)
    first_of_sequence = pl.program_id(1) == 0
    for j in range(q_ref.shape[1] // WINDOW):
        rows = slice(j * WINDOW, (j + 1) * WINDOW)
        before = slice((j - 1) * WINDOW, j * WINDOW)
        k_prev, v_prev = (kp_ref[0], vp_ref[0]) if j == 0 else (kc_ref[0, before], vc_ref[0, before])
        variant = jnp.where(first_of_sequence, 0, 1) if j == 0 else 1
        k_all = jnp.where(sink_row, zero, jnp.concatenate([k_prev, kc_ref[0, rows]], axis=0))
        v_all = jnp.where(sink_row, zero, jnp.concatenate([v_prev, vc_ref[0, rows]], axis=0))
        k_bd = jnp.concatenate([jnp.where(lane_head == kh, k_all, zero) for kh in kvh], axis=0)
        v_bd = jnp.concatenate([jnp.where(lane_head == kh, v_all, zero) for kh in kvh], axis=0)
        q_cat = jnp.concatenate([q_ref[0, rows, g * dk:(g + 1) * dk] for g in range(GROUP)],
                                axis=0)
        s = lax.dot_general(q_cat, k_bd, (((1,), (1,)), ((), ())), preferred_element_type=F32)
        es, invs = [], []
        for kh in kvh:
            sk = s[:, kh * 2 * WINDOW:(kh + 1) * 2 * WINDOW] + bias_ref[variant, kh]
            e = jnp.exp(sk - jnp.max(sk, axis=1, keepdims=True))
            invs.append(1.0 / jnp.sum(e, axis=1, keepdims=True))
            es.append(e.astype(BF16))
        pv = jnp.dot(jnp.concatenate(es, axis=1), v_bd, preferred_element_type=F32)
        inv = invs[KV_HEADS - 1]
        for kh in range(KV_HEADS - 1):
            inv = jnp.where(out_head == kh, invs[kh], inv)
        o = pv * inv
        for g in range(GROUP):
            o_ref[0, rows, g * dk:(g + 1) * dk] = o[g * WINDOW:(g + 1) * WINDOW].astype(o_ref.dtype)


def attn_prompt(q, k, v, table, batch, seq):
    d = Q_HEADS * HEAD_DIM
    dk = KV_HEADS * HEAD_DIM
    nq = ATTN_PROMPT_BLOCKS
    r3 = lambda a: a.reshape(batch, seq, a.shape[-1])
    prev = pl.BlockSpec((1, WINDOW, dk), lambda b, i: (b, jnp.maximum(i * nq - 1, 0), 0))
    cur = pl.BlockSpec((1, nq * WINDOW, dk), lambda b, i: (b, i, 0))
    table = table.reshape(KV_HEADS, GROUP * WINDOW, 2 * WINDOW)
    key = jnp.arange(2 * WINDOW)
    no_prev = jnp.where((key < WINDOW) & (key != PROMPT_SINK_KEY), NEG_INF, table)
    tables = jnp.stack([no_prev, table])
    out = pl.pallas_call(
        _attn_prompt_kernel,
        grid=(batch, seq // (nq * WINDOW)),
        in_specs=[pl.BlockSpec((1, nq * WINDOW, d), lambda b, i: (b, i, 0)),
                  prev, cur, prev, cur,
                  pl.BlockSpec(tables.shape, lambda b, i: (0, 0, 0, 0))],
        out_specs=pl.BlockSpec((1, nq * WINDOW, d), lambda b, i: (b, i, 0)),
        out_shape=jax.ShapeDtypeStruct((batch, seq, d), BF16),
        compiler_params=_cparams("parallel", "arbitrary"),
        name="attn_prompt",
    )(r3(q), r3(k), r3(k), r3(v), r3(v), tables)
    return out.reshape(batch * seq, d)


SAMPLE_NEW_KEYS = 16


def _attn_sample_kernel(q_ref, kct_ref, vct_ref, kn_ref, vn_ref, knt_ref, vnt_ref, bias_c_ref,
                        bias_n_ref, o_ref, *cache_out_refs, seq):
    n_b = q_ref.shape[0]
    dk = KV_HEADS * HEAD_DIM
    pairs = [(b, kh) for b in range(n_b) for kh in range(KV_HEADS)]
    kslice = lambda kh: slice(kh * HEAD_DIM, (kh + 1) * HEAD_DIM)
    head_lanes = lambda kh, g: slice((g * KV_HEADS + kh) * HEAD_DIM, (g * KV_HEADS + kh + 1) * HEAD_DIM)
    pad = jnp.zeros((SAMPLE_NEW_KEYS - seq, dk), F32)
    nt = (((1,), (1,)), ((), ()))
    dot = functools.partial(jnp.dot, preferred_element_type=F32)
    dot_nt = functools.partial(lax.dot_general, dimension_numbers=nt, preferred_element_type=F32)

    if cache_out_refs:
        lane = lax.broadcasted_iota(jnp.int32, (HEAD_DIM, WINDOW), 1)
        steps_per_block = knt_ref.shape[1] // (n_b * seq)
        first_token = (pl.program_id(0) % steps_per_block) * (n_b * seq)
        for out_ref, old_ref, newt_ref in zip(cache_out_refs, (kct_ref, vct_ref), (knt_ref, vnt_ref)):
            for b, kh in pairs:
                moved = pltpu.roll(old_ref[b, kh], WINDOW - seq, 1)
                shift = (2 * WINDOW - seq - first_token - b * seq) % WINDOW
                fresh = pltpu.roll(newt_ref[kslice(kh), :], shift, 1)
                out_ref[b, kh] = jnp.where(lane >= WINDOW - seq, fresh, moved)

    qss = [jnp.concatenate([q_ref[b, :, head_lanes(kh, g)] for g in range(GROUP)],
                           axis=0).astype(BF16) for b, kh in pairs]
    kcts = [kct_ref[b, kh].astype(BF16) for b, kh in pairs]
    vcts = [vct_ref[b, kh].astype(BF16) for b, kh in pairs]
    kns = [jnp.concatenate([kn_ref[b], pad], axis=0).astype(BF16) for b in range(n_b)]
    vns = [jnp.concatenate([vn_ref[b], pad], axis=0).astype(BF16) for b in range(n_b)]
    s_cs = [dot(qss[i], kcts[i]) for i in range(len(pairs))]
    s_ns = [dot_nt(qss[i], kns[b][:, kslice(kh)]) for i, (b, kh) in enumerate(pairs)]
    e_cs, e_ns, denoms = [], [], []
    for i, (b, kh) in enumerate(pairs):
        s_c = s_cs[i] + bias_c_ref[kh]
        s_n = s_ns[i] + bias_n_ref[kh]
        m = jnp.maximum(jnp.max(s_c, axis=1, keepdims=True), jnp.max(s_n, axis=1, keepdims=True))
        e_c, e_n = jnp.exp(s_c - m), jnp.exp(s_n - m)
        denoms.append(jnp.sum(e_c, axis=1, keepdims=True) + jnp.sum(e_n, axis=1, keepdims=True))
        e_cs.append(e_c.astype(BF16))
        e_ns.append(e_n.astype(BF16))
    pv_cs = [dot_nt(e_cs[i], vcts[i]) for i in range(len(pairs))]
    pv_ns = [dot(e_ns[i], vns[b][:, kslice(kh)]) for i, (b, kh) in enumerate(pairs)]
    for i, (b, kh) in enumerate(pairs):
        o = (pv_cs[i] + pv_ns[i]) / denoms[i]
        for g in range(GROUP):
            o_ref[b, :, head_lanes(kh, g)] = o[g * seq:(g + 1) * seq].astype(o_ref.dtype)


def attn_sample(q, k_new, v_new, cache_k, cache_v, table, batch, seq, write_cache):
    d = Q_HEADS * HEAD_DIM
    dk = KV_HEADS * HEAD_DIM
    nbk = ATTN_SAMPLE_BLOCK
    assert WINDOW % (nbk * seq) == 0
    new = pl.BlockSpec((nbk, seq, dk), lambda i: (i, 0, 0))
    old = pl.BlockSpec((nbk, KV_HEADS, HEAD_DIM, WINDOW), lambda i: (i, 0, 0, 0))
    newt = pl.BlockSpec((dk, WINDOW), lambda i: (0, i * nbk * seq // WINDOW))
    to_t = lambda c: jnp.transpose(c, (0, 2, 3, 1))
    table = table.reshape(KV_HEADS, GROUP * seq, 2 * WINDOW)
    n_cache = 2 if write_cache else 0
    out = pl.pallas_call(
        functools.partial(_attn_sample_kernel, seq=seq),
        grid=(batch // nbk,),
        in_specs=[pl.BlockSpec((nbk, seq, d), lambda i: (i, 0, 0)),
                  old, old, new, new, newt, newt,
                  pl.BlockSpec((KV_HEADS, GROUP * seq, WINDOW), lambda i: (0, 0, 0)),
                  pl.BlockSpec((KV_HEADS, GROUP * seq, SAMPLE_NEW_KEYS), lambda i: (0, 0, 0))],
        out_specs=[pl.BlockSpec((nbk, seq, d), lambda i: (i, 0, 0))] + [old] * n_cache,
        out_shape=[jax.ShapeDtypeStruct((batch, seq, d), F32)]
                  + [jax.ShapeDtypeStruct((batch, KV_HEADS, HEAD_DIM, WINDOW), F32)] * n_cache,
        compiler_params=_cparams("parallel"),
        name="attn_sample",
    )(q.reshape(batch, seq, d), to_t(cache_k), to_t(cache_v),
      k_new.reshape(batch, seq, dk), v_new.reshape(batch, seq, dk), k_new.T, v_new.T,
      table[:, :, :WINDOW], table[:, :, WINDOW:WINDOW + SAMPLE_NEW_KEYS])
    return (out[0].reshape(batch * seq, d),) + tuple(jnp.transpose(c, (0, 3, 1, 2)) for c in out[1:])


def _trunk(x, state, cache, w, tm, tm_mlp):
    batch, seq, d = x.shape
    x = x.reshape(batch * seq, d)
    hq, hv = N_HEADS * DQK, N_HEADS * DV
    depth = w["norm_mix"].shape[0]
    n_a = w["w_in"].shape[0]
    prompt = state is None
    act = BF16 if prompt else F32
    cs, ns, ms = [], [], []
    c_stack = None if prompt else state[0]
    for l in range(depth):
        if l < n_a:
            gates = (w["b_igate"][l], w["b_fgate"][l], w["mlstm_norm"][l])
            q, k, v, o, gate = mlstm_inproj(x, w["norm_mix"], w["w_in"], l, tm, prompt, act)
            if prompt:
                a, c_new, n_new, m_new = mlstm_prompt(q, k, v, o, gate, *gates, batch, seq)
                cs.append(c_new)
            else:
                a, c_stack, n_new, m_new = mlstm_sample(q, k, v, o, gate, *gates, state[0], c_stack,
                                                        l, state[1][l], state[2][l], batch, seq)
            ns.append(n_new); ms.append(m_new)
            w_o, lo = w["w_mlstm_out"], l
        else:
            j = l - n_a
            q_proj = (w["norm_mix"], l, w["w_q"], j, ((0, Q_HEADS * HEAD_DIM),), (act,))
            if j == 0:
                dk = KV_HEADS * HEAD_DIM
                kv_proj = (w["kv_norm"], 0, w["w_kv"], 0, ((0, dk), (dk, dk), (0, dk), (dk, dk)),
                           (F32, F32, BF16, BF16))
                (k_new, v_new, k16, v16), (q,) = norm_matmul(x, (kv_proj, q_proj), tm_mlp)
                tables = bias_table(w["rel_bias"], w["attn_sinks"], WINDOW if prompt else seq,
                                    2 * WINDOW, PROMPT_SINK_KEY if prompt else WINDOW + seq)
            else:
                ((q,),) = norm_matmul(x, (q_proj,), tm_mlp)
            if prompt:
                a = attn_prompt(q, k16, v16, tables[j], batch, seq)
            elif j == 0:
                a, win_k, win_v = attn_sample(q, k_new, v_new, cache[0], cache[1], tables[j],
                                              batch, seq, True)
            else:
                (a,) = attn_sample(q, k_new, v_new, cache[0], cache[1], tables[j],
                                   batch, seq, False)
            w_o, lo = w["w_attn_out"], j
        x = mlp(a, w_o, lo, x, w["norm_ffn"], w["w_up"], w["w_down"], l, w["final_norm"],
                l == depth - 1, tm_mlp, MLP_FF_BLOCK if prompt else MLP_FF_CHUNK)
    if prompt:
        dk = KV_HEADS * HEAD_DIM
        win_k = k_new.reshape(batch, seq, dk)[:, -WINDOW:]
        win_v = v_new.reshape(batch, seq, dk)[:, -WINDOW:]
        c_stack = jnp.stack(cs)
    shp = (batch, WINDOW, KV_HEADS, HEAD_DIM)
    return (x.reshape(batch, seq, d), c_stack, jnp.stack(ns), jnp.stack(ms),
            win_k.reshape(shp), win_v.reshape(shp))


def kernel(x_prompt, x_sample, state_C, state_n, state_m, cache_k, cache_v, norm_mix, norm_ffn,
           w_mlstm_in, b_igate, b_fgate, mlstm_norm, w_mlstm_out, kv_norm, w_kv, w_q, attn_sinks,
           w_attn_out, rel_bias, w_up, w_down, final_norm):
    n_b, d = w_q.shape[0], w_q.shape[1]
    heads = (KV_HEADS, GROUP, HEAD_DIM)
    w_q_perm = (w_q * HEAD_DIM ** -0.5).astype(BF16).reshape((n_b, d) + heads)
    w_q_perm = w_q_perm.transpose(0, 1, 3, 2, 4).reshape(n_b, d, d)
    w_ao_perm = w_attn_out.astype(BF16).reshape((n_b,) + heads + (d,))
    w_ao_perm = w_ao_perm.transpose(0, 2, 1, 3, 4).reshape(n_b, d, d)
    w = dict(norm_mix=norm_mix[:, None, :], norm_ffn=norm_ffn[:, None, :],
             w_in=w_mlstm_in.astype(BF16), b_igate=b_igate, b_fgate=b_fgate,
             mlstm_norm=mlstm_norm, w_mlstm_out=w_mlstm_out.astype(BF16),
             kv_norm=kv_norm[None, None, :], w_kv=w_kv.astype(BF16)[None],
             w_q=w_q_perm, attn_sinks=attn_sinks, w_attn_out=w_ao_perm, rel_bias=rel_bias,
             w_up=w_up.astype(BF16), w_down=w_down.astype(BF16), final_norm=final_norm)
    y_p, c_p, n_p, m_p, k_p, v_p = _trunk(x_prompt, None, None, w, 512, 1024)
    y_s, c_s, n_s, m_s, k_s, v_s = _trunk(x_sample, (state_C, state_n, state_m),
                                          (cache_k, cache_v), w, 512, 512)
    return (y_p, y_s, c_p, n_p, m_p, k_p, v_p, c_s, n_s, m_s, k_s, v_s)
```

```python
import functools
import math

import jax
import jax.numpy as jnp
from jax import lax
from jax.experimental import pallas as pl
from jax.experimental.pallas import tpu as pltpu

F32 = jnp.float32
BF16 = jnp.bfloat16
EPS = 1e-6
LOG2_E = 1.4426950408889634
NEG_INF = float("-inf")

N_HEADS = 8
DQK = 64
DV = 128
Q_HEADS = 16
KV_HEADS = 4
GROUP = Q_HEADS // KV_HEADS
HEAD_DIM = 64
WINDOW = 128
N_BUCKETS = 32
MAX_EXACT = N_BUCKETS // 2
MAX_DISTANCE = 128

LANES = 128
VMEM_LIMIT = 48 * 1024 * 1024
MLP_VMEM_LIMIT = 56 * 1024 * 1024

PROMPT_CHUNK = 128
PROMPT_HEAD_GROUP = 4
PROMPT_BLOCK = 512
SAMPLE_BATCH_BLOCK = 16
ATTN_SAMPLE_BLOCK = 16
ATTN_PROMPT_BLOCKS = 8
MLP_ROW_CHUNK = 512
MLP_FF_BLOCK = 2048
MLP_FF_CHUNK = 1024


def _cparams(*sem, vmem_limit=VMEM_LIMIT):
    return pltpu.CompilerParams(dimension_semantics=sem, vmem_limit_bytes=vmem_limit)


def _rms(x, g):
    return x * lax.rsqrt(jnp.mean(x * x, axis=-1, keepdims=True) + EPS) * g


def _split3(x):
    hi = x.astype(BF16)
    r = x - hi.astype(F32)
    mid = r.astype(BF16)
    lo = (r - mid.astype(F32)).astype(BF16)
    return hi, mid, lo


def _log_sigmoid(x):
    return jnp.minimum(x, 0.0) - jnp.log1p(jnp.exp(-jnp.abs(x)))


def _sigmoid(x):
    return 1.0 / (1.0 + jnp.exp(-x))


def _norm_mm_kernel(x_ref, *refs, splits):
    n_proj = len(splits)
    out_refs = iter(refs[2 * n_proj:])
    x = x_ref[...]
    xhat = x * lax.rsqrt(jnp.mean(x * x, axis=-1, keepdims=True) + EPS)
    ys = [(xhat * refs[2 * p][...]).astype(BF16) for p in range(n_proj)]
    rs = [jnp.dot(ys[p], refs[2 * p + 1][...], preferred_element_type=F32) for p in range(n_proj)]
    for p in range(n_proj):
        for off, n in splits[p]:
            o_ref = next(out_refs)
            o_ref[...] = rs[p][:, off:off + n].astype(o_ref.dtype)


def _layer_spec(arr, layer):
    idx = (layer,) + (0,) * (arr.ndim - 1)
    return pl.BlockSpec((None,) + arr.shape[1:], lambda *_: idx)


def norm_matmul(x, projections, tm):
    m, d = x.shape
    assert m % tm == 0
    params, param_specs, out_specs, out_shapes = [], [], [], []
    for g, lg, w, lw, splits, dtypes in projections:
        assert all(off + n <= w.shape[-1] for off, n in splits)
        params += [g, w]
        param_specs += [_layer_spec(g, lg), _layer_spec(w, lw)]
        out_specs += [pl.BlockSpec((tm, n), lambda i: (i, 0)) for _, n in splits]
        out_shapes += [jax.ShapeDtypeStruct((m, n), dt) for (_, n), dt in zip(splits, dtypes)]
    outs = pl.pallas_call(
        functools.partial(_norm_mm_kernel, splits=tuple(p[4] for p in projections)),
        grid=(m // tm,),
        in_specs=[pl.BlockSpec((tm, d), lambda i: (i, 0))] + param_specs,
        out_specs=out_specs,
        out_shape=out_shapes,
        compiler_params=_cparams("parallel"),
        name="norm_matmul",
    )(x, *params)
    outs, grouped = list(outs), []
    for p in projections:
        grouped.append(outs[:len(p[4])])
        outs = outs[len(p[4]):]
    return grouped


def _mlp_kernel(a_ref, wo_ref, x_ref, g_ref, wu_ref, wd_ref, gf_ref, o_ref, xn_ref, *,
                final_norm, row_chunk, ff_chunk):
    j = pl.program_id(1)
    tm = x_ref.shape[0]

    @pl.when(j == 0)
    def _():
        for r in range(0, tm, row_chunk):
            rows = slice(r, r + row_chunk)
            x = x_ref[rows, :] + jnp.dot(a_ref[rows, :].astype(BF16), wo_ref[...],
                                         preferred_element_type=F32)
            xn_ref[rows, :] = _rms(x, g_ref[...]).astype(BF16)
            o_ref[rows, :] = x

    for c in range(0, wu_ref.shape[1], ff_chunk):
        h = jnp.dot(xn_ref[...], wu_ref[:, c:c + ff_chunk], preferred_element_type=F32)
        h = jnp.maximum(h, 0.0)
        o_ref[...] += jnp.dot((h * h).astype(BF16), wd_ref[c:c + ff_chunk, :],
                              preferred_element_type=F32)

    if final_norm:
        @pl.when(j == pl.num_programs(1) - 1)
        def _():
            o_ref[...] = _rms(o_ref[...], gf_ref[...])


def mlp(a, w_o, lo, x, g, w_up, w_down, l, g_final, final_norm, tm, tf):
    m, d = x.shape
    ff = w_up.shape[-1]
    return pl.pallas_call(
        functools.partial(_mlp_kernel, final_norm=final_norm, row_chunk=min(tm, MLP_ROW_CHUNK),
                          ff_chunk=MLP_FF_CHUNK),
        grid=(m // tm, ff // tf),
        in_specs=[pl.BlockSpec((tm, d), lambda i, j: (i, 0)),
                  _layer_spec(w_o, lo),
                  pl.BlockSpec((tm, d), lambda i, j: (i, 0)),
                  _layer_spec(g, l),
                  pl.BlockSpec((None, d, tf), lambda i, j: (l, 0, j)),
                  pl.BlockSpec((None, tf, d), lambda i, j: (l, j, 0)),
                  pl.BlockSpec((1, d), lambda i, j: (0, 0))],
        out_specs=pl.BlockSpec((tm, d), lambda i, j: (i, 0)),
        out_shape=jax.ShapeDtypeStruct((m, d), F32),
        scratch_shapes=[pltpu.VMEM((tm, d), BF16)],
        compiler_params=_cparams("parallel", "arbitrary",
                                 vmem_limit=MLP_VMEM_LIMIT if tf > MLP_FF_CHUNK else VMEM_LIMIT),
        name="mlp",
    )(a, w_o, x, g, w_up, w_down, g_final.reshape(1, d))


def _mlstm_inproj_kernel(x_ref, g_ref, w_ref, q_ref, k_ref, v_ref, o_ref, gate_ref, *, k_transposed):
    hq, hv = N_HEADS * DQK, N_HEADS * DV
    tm = x_ref.shape[0]
    y = _rms(x_ref[...], g_ref[...]).astype(BF16)
    r = jnp.dot(y, w_ref[...], preferred_element_type=F32)
    q_ref[...] = r[:, :hq].astype(q_ref.dtype)
    k = r[:, hq:2 * hq] * DQK ** -0.5
    k_ref[...] = (k.T if k_transposed else k).astype(k_ref.dtype)
    v_ref[...] = r[:, 2 * hq:2 * hq + hv].astype(v_ref.dtype)
    o_ref[...] = r[:, 2 * hq + hv:2 * hq + 2 * hv]
    gate_ref[...] = jnp.concatenate(
        [r[:, 2 * hq + 2 * hv:], jnp.zeros((tm, LANES - 2 * N_HEADS), F32)], axis=1)


def mlstm_inproj(x, g, w, l, tm, k_transposed, act):
    m, d = x.shape
    hq, hv = N_HEADS * DQK, N_HEADS * DV
    assert w.shape[-1] == 2 * hq + 2 * hv + 2 * N_HEADS
    row = lambda width: pl.BlockSpec((tm, width), lambda i: (i, 0))
    k_spec = pl.BlockSpec((hq, tm), lambda i: (0, i)) if k_transposed else row(hq)
    k_shape = (hq, m) if k_transposed else (m, hq)
    return pl.pallas_call(
        functools.partial(_mlstm_inproj_kernel, k_transposed=k_transposed),
        grid=(m // tm,),
        in_specs=[row(d), _layer_spec(g, l), _layer_spec(w, l)],
        out_specs=[row(hq), k_spec, row(hv), row(hv), row(LANES)],
        out_shape=[jax.ShapeDtypeStruct((m, hq), act), jax.ShapeDtypeStruct(k_shape, act),
                   jax.ShapeDtypeStruct((m, hv), act), jax.ShapeDtypeStruct((m, hv), F32),
                   jax.ShapeDtypeStruct((m, LANES), F32)],
        compiler_params=_cparams("parallel"),
        name="mlstm_inproj",
    )(x, g, w)


def _chunk_scan(x, pos, op, fill, length):
    k = 1
    while k < length:
        x = op(x, jnp.where(pos >= k, pltpu.roll(x, k, 0), fill))
        k *= 2
    return x


def _mlstm_prompt_kernel(q_ref, kt_ref, v_ref, o_ref, gate_ref, bi_ref, bf_ref, gain_ref,
                         hg_ref, cx_out_ref, m_out_ref, cx_s, m_s, *, chunk):
    t_blk = q_ref.shape[1]
    n_chunks = t_blk // chunk
    step = pl.program_id(1)

    @pl.when(step == 0)
    def _():
        cx_s[...] = jnp.zeros_like(cx_s)
        m_s[...] = jnp.zeros_like(m_s)

    gates = gate_ref[0]
    li = gates + bi_ref[...]
    lf = _log_sigmoid(pltpu.roll(gates, LANES - N_HEADS, 1) + bf_ref[...])
    pos = lax.broadcasted_iota(jnp.int32, (t_blk, 1), 0) % chunk
    b = _chunk_scan(lf, pos, jnp.add, 0.0, chunk)
    c = li - b
    cm = _chunk_scan(c, pos, jnp.maximum, NEG_INF, chunk)

    m_prev = m_s[...]
    xs, w_inters, e_negms, wks = [], [], [], []
    for ck in range(n_chunks):
        sl = slice(ck * chunk, (ck + 1) * chunk)
        m_t = b[sl] + jnp.maximum(m_prev, cm[sl])
        m_new = m_t[chunk - 1:chunk]
        b_last = b[(ck + 1) * chunk - 1:(ck + 1) * chunk]
        xs.append((b[sl] - m_t) * LOG2_E)
        w_inters.append(jnp.exp(b[sl] + m_prev - m_t))
        e_negms.append(jnp.exp(-m_t))
        wks.append(jnp.exp(c[sl] + (b_last - m_new)))
        m_prev = m_new
    m_s[...] = m_prev
    c_t = (c * LOG2_E).T
    wk_t = jnp.concatenate(wks, axis=0).T

    ri = lax.broadcasted_iota(jnp.int32, (chunk, chunk), 0)
    ci = lax.broadcasted_iota(jnp.int32, (chunk, chunk), 1)
    causal = ci <= ri
    ones_blk = jnp.ones((chunk, DV), BF16)
    mean_sq = jnp.full((DV, DV), 1.0 / DV, BF16)
    heads = range(N_HEADS)
    dot = functools.partial(jnp.dot, preferred_element_type=F32)

    cxs = {h: cx_s[h] for h in heads}
    head_groups = [heads[g:g + PROMPT_HEAD_GROUP] for g in range(0, N_HEADS, PROMPT_HEAD_GROUP)]
    for ck in range(n_chunks):
        sl = slice(ck * chunk, (ck + 1) * chunk)
        for hs in head_groups:
            qs = {h: q_ref[0, sl, h * DQK:(h + 1) * DQK] for h in hs}
            kts = {h: kt_ref[h * DQK:(h + 1) * DQK, sl] for h in hs}
            vs = {h: v_ref[0, sl, h * DV:(h + 1) * DV] for h in hs}
            cx16s = {h: cxs[h].astype(BF16) for h in hs}
            kws = {h: (kts[h].astype(F32) * wk_t[h:h + 1, sl]).astype(BF16) for h in hs}
            vones = {h: jnp.concatenate([vs[h], ones_blk], axis=1) for h in hs}
            ss = {h: dot(qs[h], kts[h]) for h in hs}
            qcs = {h: dot(qs[h], cx16s[h]) for h in hs}
            dcs = {h: dot(kws[h], vones[h]) for h in hs}
            for h in hs:
                cxs[h] = w_inters[ck][chunk - 1:chunk, h:h + 1] * cxs[h] + dcs[h]
            a16s = {}
            for h in hs:
                dm = xs[ck][:, h:h + 1] + c_t[h:h + 1, sl]
                a16s[h] = (ss[h] * jnp.exp2(jnp.where(causal, dm, NEG_INF))).astype(BF16)
            avs = {h: dot(a16s[h], vones[h]) for h in hs}
            hhs = {}
            for h in hs:
                wi = jnp.broadcast_to(w_inters[ck][:, h:h + 1], (chunk, DV))
                en = jnp.broadcast_to(e_negms[ck][:, h:h + 1], (chunk, DV))
                den = avs[h][:, DV:] + wi * qcs[h][:, DV:]
                inv = 1.0 / jnp.maximum(jnp.abs(den), en)
                hhs[h] = (avs[h][:, :DV] + qcs[h][:, :DV] * wi) * inv
            sq16s = {h: (hhs[h] * hhs[h]).astype(BF16) for h in hs}
            mss = {h: dot(sq16s[h], mean_sq) for h in hs}
            for h in hs:
                hn = hhs[h] * lax.rsqrt(mss[h] + EPS)
                hn = hn * gain_ref[:, h * DV:(h + 1) * DV]
                og = _sigmoid(o_ref[0, sl, h * DV:(h + 1) * DV])
                hg_ref[0, sl, h * DV:(h + 1) * DV] = (og * hn).astype(hg_ref.dtype)
    for h in heads:
        cx_s[h] = cxs[h]

    @pl.when(step == pl.num_programs(1) - 1)
    def _():
        cx_out_ref[0] = cx_s[...]
        m_out_ref[0] = m_s[...]


def mlstm_prompt(q, kt, v, o, gate, b_i, b_f, gain, batch, seq):
    hq, hv = N_HEADS * DQK, N_HEADS * DV
    t = PROMPT_BLOCK
    nblk = seq // t
    r3 = lambda a: a.reshape(batch, seq, a.shape[-1])
    pad8 = lambda a: jnp.pad(a.reshape(1, N_HEADS), ((0, 0), (0, LANES - N_HEADS)))
    tok = lambda w: pl.BlockSpec((1, t, w), lambda bb, s: (bb, s, 0))
    cst = lambda w: pl.BlockSpec((1, w), lambda bb, s: (0, 0))
    hg, cx, m_new = pl.pallas_call(
        functools.partial(_mlstm_prompt_kernel, chunk=PROMPT_CHUNK),
        grid=(batch, nblk),
        in_specs=[tok(hq), pl.BlockSpec((hq, t), lambda bb, s: (0, bb * nblk + s)),
                  tok(hv), tok(hv), tok(LANES), cst(LANES), cst(LANES), cst(hv)],
        out_specs=[tok(hv),
                   pl.BlockSpec((1, N_HEADS, DQK, 2 * DV), lambda bb, s: (bb, 0, 0, 0)),
                   pl.BlockSpec((1, 1, LANES), lambda bb, s: (bb, 0, 0))],
        out_shape=[jax.ShapeDtypeStruct((batch, seq, hv), BF16),
                   jax.ShapeDtypeStruct((batch, N_HEADS, DQK, 2 * DV), F32),
                   jax.ShapeDtypeStruct((batch, 1, LANES), F32)],
        scratch_shapes=[pltpu.VMEM((N_HEADS, DQK, 2 * DV), F32), pltpu.VMEM((1, LANES), F32)],
        compiler_params=_cparams("parallel", "arbitrary"),
        name="mlstm_prompt",
    )(r3(q), kt, r3(v), r3(o), r3(gate), pad8(b_i), pad8(b_f), gain.reshape(1, hv))
    return hg.reshape(batch * seq, hv), cx[..., :DV], cx[..., DV], m_new[:, 0, :N_HEADS]


def _mlstm_sample_kernel(q_ref, k_ref, v_ref, o_ref, gate_ref, m0_ref, n0_ref, c0_ref, c_carry_ref,
                         bi_ref, bf_ref, gain_ref, seg64_ref, seg128_ref, e64_ref, e128_ref,
                         hg_ref, c_out_ref, n_out_ref, m_out_ref,
                         qc_s, *, seq):
    del c_carry_ref
    spans_layers = len(c_out_ref.shape) == 5
    c_new_ref = c_out_ref.at[0] if spans_layers else c_out_ref
    if spans_layers:
        for later in range(1, c_out_ref.shape[0]):
            c_out_ref[later] = jnp.zeros(c_out_ref.shape[1:], F32)
    rows = q_ref.shape[0]
    n_b = rows // seq
    tpos = lax.broadcasted_iota(jnp.int32, (rows, 1), 0) % seq
    dot = functools.partial(jnp.dot, preferred_element_type=F32)

    def shift(x, d):
        return x if d == 0 else pltpu.roll(x, d, 0)

    def unshift(x, d):
        return x if d == 0 else pltpu.roll(x, rows - d, 0)

    def split_cat(x):
        return jnp.concatenate(_split3(x), axis=1)


    q = q_ref[...].astype(BF16).astype(F32)
    k = k_ref[...].astype(BF16).astype(F32)
    v = v_ref[...].astype(BF16).astype(F32)

    units = [(p, h) for p in range(n_b // 2) for h in range(N_HEADS)]
    prow = lambda p: slice(p * 2 * seq, (p + 1) * 2 * seq)
    q_pairs = [q[prow(p), h * DQK:(h + 1) * DQK].astype(BF16) for p, h in units]
    c_cats = [jnp.concatenate([c0_ref[2 * p, h], c0_ref[2 * p + 1, h]], axis=1).astype(BF16)
              for p, h in units]
    qcs = [dot(q_pairs[i], c_cats[i]) for i in range(len(units))]

    gates = gate_ref[...]
    li = gates + bi_ref[...]
    lf = _log_sigmoid(pltpu.roll(gates, LANES - N_HEADS, 1) + bf_ref[...])
    b = lf
    for d in range(1, seq):
        b = b + jnp.where(tpos >= d, shift(lf, d), 0.0)
    inter = b + m0_ref[...]
    dvals = []
    m_t = inter
    for d in range(seq):
        dd = jnp.where(tpos >= d, b - shift(b, d) + shift(li, d), NEG_INF)
        dvals.append(dd)
        m_t = jnp.maximum(m_t, dd)
    w_inter = jnp.exp(inter - m_t)
    w_intra = [jnp.exp(dd - m_t) for dd in dvals]
    last = lambda x: functools.reduce(
        lambda acc, d: jnp.where(tpos == seq - 1 - d, unshift(x, d), acc), range(1, seq), x)
    wk = jnp.exp(last(b) - b + li - last(m_t))

    qk_in = [split_cat(q * shift(k, d)) for d in range(seq)]
    qn_in = split_cat(q * n0_ref[...].astype(BF16).astype(F32))
    wk_in, wi_in = split_cat(wk), split_cat(w_inter)
    seg64, e64, e128 = seg64_ref[...], e64_ref[...], e128_ref[...]
    qks = [dot(x, seg64) for x in qk_in]
    qn = dot(qn_in, seg64)
    wk_exp = dot(wk_in, e64)
    decay64 = dot(wi_in, e64)
    decay_exp = dot(wi_in, e128)

    a_s = [qks[d] * w_intra[d] for d in range(seq)]
    e128_1 = e128[:LANES]
    a_exps = [dot(a.astype(BF16), e128_1) for a in a_s]
    den = w_inter * qn
    num = jnp.zeros((rows, N_HEADS * DV), F32)
    for d in range(seq):
        den = den + a_s[d]
        num = num + a_exps[d] * shift(v, d)

    kw = k * wk_exp
    ksum = kw
    for d in range(1, seq):
        ksum = ksum + shift(kw, d)
    n_rows = decay64 * n0_ref[...] + ksum
    own = ((lax.broadcasted_iota(jnp.int32, (2 * seq, 2 * DV), 0) < seq)
           == (lax.broadcasted_iota(jnp.int32, (2 * seq, 2 * DV), 1) < DV))
    kw_pairs = [kw[prow(p), h * DQK:(h + 1) * DQK].astype(BF16) for p, h in units]
    v2s = []
    for p, h in units:
        v_pair = v[prow(p), h * DV:(h + 1) * DV]
        v2s.append(jnp.where(own, jnp.concatenate([v_pair, v_pair], axis=1), 0.0).astype(BF16))
    sel3 = (lax.broadcasted_iota(jnp.int32, (n_b, 3 * rows), 1) % rows
            == lax.broadcasted_iota(jnp.int32, (n_b, 3 * rows), 0) * seq + (seq - 1))
    sel3 = jnp.where(sel3, 1.0, 0.0).astype(BF16)
    n_in = jnp.concatenate(_split3(n_rows), axis=0)
    m_in = jnp.concatenate(_split3(m_t), axis=0)
    d_cs = [lax.dot_general(kw_pairs[i], v2s[i], (((0,), (0,)), ((), ())),
                            preferred_element_type=F32) for i in range(len(units))]
    n_out_ref[...] = dot(sel3, n_in)
    m_out_ref[...] = dot(sel3, m_in)

    first = lax.broadcasted_iota(jnp.int32, (2 * seq, DV), 0) < seq
    for i, (p, h) in enumerate(units):
        r0 = p * 2 * seq
        qc_s[prow(p), h * DV:(h + 1) * DV] = jnp.where(first, qcs[i][:, :DV], qcs[i][:, DV:])
        dec_a = decay_exp[r0 + seq - 1:r0 + seq, h * DV:(h + 1) * DV]
        dec_b = decay_exp[r0 + 2 * seq - 1:r0 + 2 * seq, h * DV:(h + 1) * DV]
        c_new_ref[2 * p, h] = dec_a * c0_ref[2 * p, h] + d_cs[i][:, :DV]
        c_new_ref[2 * p + 1, h] = dec_b * c0_ref[2 * p + 1, h] + d_cs[i][:, DV:]

    num = num + qc_s[...] * decay_exp
    inv_scale = 1.0 / jnp.maximum(jnp.abs(den), jnp.exp(-m_t))
    hh = num * dot(split_cat(inv_scale), e128)
    ms = dot(split_cat(hh * hh), seg128_ref[...]) * (1.0 / DV)
    hn = hh * dot(split_cat(lax.rsqrt(ms + EPS)), e128) * gain_ref[...]
    hg_ref[...] = (_sigmoid(o_ref[...]) * hn).astype(hg_ref.dtype)


def mlstm_sample(q, k, v, o, gate, b_i, b_f, gain, state_c, c_carry, l, n0, m0, batch, seq):
    hq, hv = N_HEADS * DQK, N_HEADS * DV
    c_blk = pl.BlockSpec((None, SAMPLE_BATCH_BLOCK, N_HEADS, DQK, DV), lambda i: (l, i, 0, 0, 0))
    c_all = pl.BlockSpec((state_c.shape[0], SAMPLE_BATCH_BLOCK, N_HEADS, DQK, DV),
                         lambda i: (0, i, 0, 0, 0))
    rows = SAMPLE_BATCH_BLOCK * seq
    pad8 = lambda a: jnp.pad(a.reshape(1, N_HEADS), ((0, 0), (0, LANES - N_HEADS)))
    m0_rows = jnp.pad(jnp.repeat(m0, seq, axis=0), ((0, 0), (0, LANES - N_HEADS)))
    n0_rows = jnp.repeat(n0.reshape(batch, hq), seq, axis=0)
    lane = jnp.arange(LANES)
    seg64 = (jnp.arange(hq)[:, None] // DQK == lane[None, :]).astype(BF16)
    seg128 = (jnp.arange(hv)[:, None] // DV == lane[None, :]).astype(BF16)
    tok = lambda w: pl.BlockSpec((rows, w), lambda i: (i, 0))
    cst = lambda a: pl.BlockSpec(a.shape, lambda i: (0,) * a.ndim)
    x3 = lambda a: jnp.tile(a, (3, 1))
    consts = [pad8(b_i), pad8(b_f), gain.reshape(1, hv),
              x3(seg64), x3(seg128), x3(seg64.T), x3(seg128.T)]
    hg, c_new, n_new, m_new = pl.pallas_call(
        functools.partial(_mlstm_sample_kernel, seq=seq),
        grid=(batch // SAMPLE_BATCH_BLOCK,),
        in_specs=[tok(hq), tok(hq), tok(hv), tok(hv), tok(LANES), tok(LANES), tok(hq),
                  c_blk, pl.BlockSpec(memory_space=pl.ANY)]
                 + [cst(a) for a in consts],
        out_specs=[tok(hv), c_all if l == 0 else c_blk,
                   pl.BlockSpec((SAMPLE_BATCH_BLOCK, hq), lambda i: (i, 0)),
                   pl.BlockSpec((SAMPLE_BATCH_BLOCK, LANES), lambda i: (i, 0))],
        out_shape=[jax.ShapeDtypeStruct((batch * seq, hv), BF16),
                   jax.ShapeDtypeStruct(state_c.shape, F32),
                   jax.ShapeDtypeStruct((batch, hq), F32),
                   jax.ShapeDtypeStruct((batch, LANES), F32)],
        scratch_shapes=[pltpu.VMEM((rows, hv), F32)],
        input_output_aliases={8: 1} if l > 0 else {},
        compiler_params=_cparams("parallel"),
        name="mlstm_sample",
    )(q, k, v, o, gate, m0_rows, n0_rows, state_c, c_carry, *consts)
    return hg, c_new, n_new.reshape(batch, N_HEADS, DQK), m_new[:, :N_HEADS]


def _bias_table_kernel(rb_ref, sink_ref, o_ref, *, sink_col):
    n_layers, _, nq, ns = o_ref.shape
    col = lax.broadcasted_iota(jnp.int32, (nq, ns), 1)
    dist = lax.broadcasted_iota(jnp.int32, (nq, ns), 0) + WINDOW - col
    n = jnp.maximum(dist, 0)
    large = MAX_EXACT + jnp.floor(
        jnp.log(jnp.maximum(n, 1).astype(F32) / MAX_EXACT)
        / math.log(MAX_DISTANCE / MAX_EXACT) * (N_BUCKETS - MAX_EXACT)).astype(jnp.int32)
    large = jnp.minimum(large, N_BUCKETS - 1)
    bucket = jnp.where(n < MAX_EXACT, n, large)
    valid = (dist >= 0) & (dist < WINDOW)
    for h in range(Q_HEADS):
        acc = jnp.zeros((nq, ns), F32)
        for bkt in range(N_BUCKETS):
            acc = jnp.where(bucket == bkt, rb_ref[bkt, h], acc)
        acc = jnp.where(valid, acc, NEG_INF)
        for layer in range(n_layers):
            o_ref[layer, h] = jnp.where(col == sink_col, sink_ref[layer, h], acc)


def bias_table(rel_bias, sinks, nq, ns, sink_col):
    return pl.pallas_call(
        functools.partial(_bias_table_kernel, sink_col=sink_col),
        in_specs=[pl.BlockSpec(memory_space=pltpu.SMEM), pl.BlockSpec(memory_space=pltpu.SMEM)],
        out_specs=pl.BlockSpec(memory_space=pltpu.VMEM),
        out_shape=jax.ShapeDtypeStruct((sinks.shape[0], Q_HEADS, nq, ns), F32),
        name="bias_table",
    )(rel_bias, sinks)


PROMPT_SINK_KEY = 0


def _attn_prompt_kernel(q_ref, kp_ref, kc_ref, vp_ref, vc_ref, bias_ref, o_ref):
    dk = KV_HEADS * HEAD_DIM
    kvh = range(KV_HEADS)
    sink_row = lax.broadcasted_iota(jnp.int32, (2 * WINDOW, dk), 0) == PROMPT_SINK_KEY
    lane_head = lax.broadcasted_iota(jnp.int32, (2 * WINDOW, dk), 1) // HEAD_DIM
    out_head = lax.broadcasted_iota(jnp.int32, (GROUP * WINDOW, dk), 1) // HEAD_DIM
    zero = jnp.zeros((), BF16)
    first_of_sequence = pl.program_id(1) == 0
    for j in range(q_ref.shape[1] // WINDOW):
        rows = slice(j * WINDOW, (j + 1) * WINDOW)
        before = slice((j - 1) * WINDOW, j * WINDOW)
        k_prev, v_prev = (kp_ref[0], vp_ref[0]) if j == 0 else (kc_ref[0, before], vc_ref[0, before])
        variant = jnp.where(first_of_sequence, 0, 1) if j == 0 else 1
        k_all = jnp.where(sink_row, zero, jnp.concatenate([k_prev, kc_ref[0, rows]], axis=0))
        v_all = jnp.where(sink_row, zero, jnp.concatenate([v_prev, vc_ref[0, rows]], axis=0))
        k_bd = jnp.concatenate([jnp.where(lane_head == kh, k_all, zero) for kh in kvh], axis=0)
        v_bd = jnp.concatenate([jnp.where(lane_head == kh, v_all, zero) for kh in kvh], axis=0)
        q_cat = jnp.concatenate([q_ref[0, rows, g * dk:(g + 1) * dk] for g in range(GROUP)],
                                axis=0)
        s = lax.dot_general(q_cat, k_bd, (((1,), (1,)), ((), ())), preferred_element_type=F32)
        es, invs = [], []
        for kh in kvh:
            sk = s[:, kh * 2 * WINDOW:(kh + 1) * 2 * WINDOW] + bias_ref[variant, kh]
            e = jnp.exp(sk - jnp.max(sk, axis=1, keepdims=True))
            invs.append(1.0 / jnp.sum(e, axis=1, keepdims=True))
            es.append(e.astype(BF16))
        pv = jnp.dot(jnp.concatenate(es, axis=1), v_bd, preferred_element_type=F32)
        inv = invs[KV_HEADS - 1]
        for kh in range(KV_HEADS - 1):
            inv = jnp.where(out_head == kh, invs[kh], inv)
        o = pv * inv
        for g in range(GROUP):
            o_ref[0, rows, g * dk:(g + 1) * dk] = o[g * WINDOW:(g + 1) * WINDOW].astype(o_ref.dtype)


def attn_prompt(q, k, v, table, batch, seq):
    d = Q_HEADS * HEAD_DIM
    dk = KV_HEADS * HEAD_DIM
    nq = ATTN_PROMPT_BLOCKS
    r3 = lambda a: a.reshape(batch, seq, a.shape[-1])
    prev = pl.BlockSpec((1, WINDOW, dk), lambda b, i: (b, jnp.maximum(i * nq - 1, 0), 0))
    cur = pl.BlockSpec((1, nq * WINDOW, dk), lambda b, i: (b, i, 0))
    table = table.reshape(KV_HEADS, GROUP * WINDOW, 2 * WINDOW)
    key = jnp.arange(2 * WINDOW)
    no_prev = jnp.where((key < WINDOW) & (key != PROMPT_SINK_KEY), NEG_INF, table)
    tables = jnp.stack([no_prev, table])
    out = pl.pallas_call(
        _attn_prompt_kernel,
        grid=(batch, seq // (nq * WINDOW)),
        in_specs=[pl.BlockSpec((1, nq * WINDOW, d), lambda b, i: (b, i, 0)),
                  prev, cur, prev, cur,
                  pl.BlockSpec(tables.shape, lambda b, i: (0, 0, 0, 0))],
        out_specs=pl.BlockSpec((1, nq * WINDOW, d), lambda b, i: (b, i, 0)),
        out_shape=jax.ShapeDtypeStruct((batch, seq, d), BF16),
        compiler_params=_cparams("parallel", "arbitrary"),
        name="attn_prompt",
    )(r3(q), r3(k), r3(k), r3(v), r3(v), tables)
    return out.reshape(batch * seq, d)


SAMPLE_NEW_KEYS = 16


def _attn_sample_kernel(q_ref, kct_ref, vct_ref, kn_ref, vn_ref, knt_ref, vnt_ref, bias_c_ref,
                        bias_n_ref, o_ref, *cache_out_refs, seq):
    n_b = q_ref.shape[0]
    dk = KV_HEADS * HEAD_DIM
    pairs = [(b, kh) for b in range(n_b) for kh in range(KV_HEADS)]
    kslice = lambda kh: slice(kh * HEAD_DIM, (kh + 1) * HEAD_DIM)
    head_lanes = lambda kh, g: slice((g * KV_HEADS + kh) * HEAD_DIM, (g * KV_HEADS + kh + 1) * HEAD_DIM)
    pad = jnp.zeros((SAMPLE_NEW_KEYS - seq, dk), F32)
    nt = (((1,), (1,)), ((), ()))
    dot = functools.partial(jnp.dot, preferred_element_type=F32)
    dot_nt = functools.partial(lax.dot_general, dimension_numbers=nt, preferred_element_type=F32)

    if cache_out_refs:
        lane = lax.broadcasted_iota(jnp.int32, (HEAD_DIM, WINDOW), 1)
        steps_per_block = knt_ref.shape[1] // (n_b * seq)
        first_token = (pl.program_id(0) % steps_per_block) * (n_b * seq)
        for out_ref, old_ref, newt_ref in zip(cache_out_refs, (kct_ref, vct_ref), (knt_ref, vnt_ref)):
            for b, kh in pairs:
                moved = pltpu.roll(old_ref[b, kh], WINDOW - seq, 1)
                shift = (2 * WINDOW - seq - first_token - b * seq) % WINDOW
                fresh = pltpu.roll(newt_ref[kslice(kh), :], shift, 1)
                out_ref[b, kh] = jnp.where(lane >= WINDOW - seq, fresh, moved)

    qss = [jnp.concatenate([q_ref[b, :, head_lanes(kh, g)] for g in range(GROUP)],
                           axis=0).astype(BF16) for b, kh in pairs]
    kcts = [kct_ref[b, kh].astype(BF16) for b, kh in pairs]
    vcts = [vct_ref[b, kh].astype(BF16) for b, kh in pairs]
    kns = [jnp.concatenate([kn_ref[b], pad], axis=0).astype(BF16) for b in range(n_b)]
    vns = [jnp.concatenate([vn_ref[b], pad], axis=0).astype(BF16) for b in range(n_b)]
    s_cs = [dot(qss[i], kcts[i]) for i in range(len(pairs))]
    s_ns = [dot_nt(qss[i], kns[b][:, kslice(kh)]) for i, (b, kh) in enumerate(pairs)]
    e_cs, e_ns, denoms = [], [], []
    for i, (b, kh) in enumerate(pairs):
        s_c = s_cs[i] + bias_c_ref[kh]
        s_n = s_ns[i] + bias_n_ref[kh]
        m = jnp.maximum(jnp.max(s_c, axis=1, keepdims=True), jnp.max(s_n, axis=1, keepdims=True))
        e_c, e_n = jnp.exp(s_c - m), jnp.exp(s_n - m)
        denoms.append(jnp.sum(e_c, axis=1, keepdims=True) + jnp.sum(e_n, axis=1, keepdims=True))
        e_cs.append(e_c.astype(BF16))
        e_ns.append(e_n.astype(BF16))
    pv_cs = [dot_nt(e_cs[i], vcts[i]) for i in range(len(pairs))]
    pv_ns = [dot(e_ns[i], vns[b][:, kslice(kh)]) for i, (b, kh) in enumerate(pairs)]
    for i, (b, kh) in enumerate(pairs):
        o = (pv_cs[i] + pv_ns[i]) / denoms[i]
        for g in range(GROUP):
            o_ref[b, :, head_lanes(kh, g)] = o[g * seq:(g + 1) * seq].astype(o_ref.dtype)


def attn_sample(q, k_new, v_new, cache_k, cache_v, table, batch, seq, write_cache):
    d = Q_HEADS * HEAD_DIM
    dk = KV_HEADS * HEAD_DIM
    nbk = ATTN_SAMPLE_BLOCK
    assert WINDOW % (nbk * seq) == 0
    new = pl.BlockSpec((nbk, seq, dk), lambda i: (i, 0, 0))
    old = pl.BlockSpec((nbk, KV_HEADS, HEAD_DIM, WINDOW), lambda i: (i, 0, 0, 0))
    newt = pl.BlockSpec((dk, WINDOW), lambda i: (0, i * nbk * seq // WINDOW))
    to_t = lambda c: jnp.transpose(c, (0, 2, 3, 1))
    table = table.reshape(KV_HEADS, GROUP * seq, 2 * WINDOW)
    n_cache = 2 if write_cache else 0
    out = pl.pallas_call(
        functools.partial(_attn_sample_kernel, seq=seq),
        grid=(batch // nbk,),
        in_specs=[pl.BlockSpec((nbk, seq, d), lambda i: (i, 0, 0)),
                  old, old, new, new, newt, newt,
                  pl.BlockSpec((KV_HEADS, GROUP * seq, WINDOW), lambda i: (0, 0, 0)),
                  pl.BlockSpec((KV_HEADS, GROUP * seq, SAMPLE_NEW_KEYS), lambda i: (0, 0, 0))],
        out_specs=[pl.BlockSpec((nbk, seq, d), lambda i: (i, 0, 0))] + [old] * n_cache,
        out_shape=[jax.ShapeDtypeStruct((batch, seq, d), F32)]
                  + [jax.ShapeDtypeStruct((batch, KV_HEADS, HEAD_DIM, WINDOW), F32)] * n_cache,
        compiler_params=_cparams("parallel"),
        name="attn_sample",
    )(q.reshape(batch, seq, d), to_t(cache_k), to_t(cache_v),
      k_new.reshape(batch, seq, dk), v_new.reshape(batch, seq, dk), k_new.T, v_new.T,
      table[:, :, :WINDOW], table[:, :, WINDOW:WINDOW + SAMPLE_NEW_KEYS])
    return (out[0].reshape(batch * seq, d),) + tuple(jnp.transpose(c, (0, 3, 1, 2)) for c in out[1:])


def _trunk(x, state, cache, w, tm, tm_mlp):
    batch, seq, d = x.shape
    x = x.reshape(batch * seq, d)
    hq, hv = N_HEADS * DQK, N_HEADS * DV
    depth = w["norm_mix"].shape[0]
    n_a = w["w_in"].shape[0]
    prompt = state is None
    act = BF16 if prompt else F32
    cs, ns, ms = [], [], []
    c_stack = None if prompt else state[0]
    for l in range(depth):
        if l < n_a:
            gates = (w["b_igate"][l], w["b_fgate"][l], w["mlstm_norm"][l])
            q, k, v, o, gate = mlstm_inproj(x, w["norm_mix"], w["w_in"], l, tm, prompt, act)
            if prompt:
                a, c_new, n_new, m_new = mlstm_prompt(q, k, v, o, gate, *gates, batch, seq)
                cs.append(c_new)
            else:
                a, c_stack, n_new, m_new = mlstm_sample(q, k, v, o, gate, *gates, state[0], c_stack,
                                                        l, state[1][l], state[2][l], batch, seq)
            ns.append(n_new); ms.append(m_new)
            w_o, lo = w["w_mlstm_out"], l
        else:
            j = l - n_a
            q_proj = (w["norm_mix"], l, w["w_q"], j, ((0, Q_HEADS * HEAD_DIM),), (act,))
            if j == 0:
                dk = KV_HEADS * HEAD_DIM
                kv_proj = (w["kv_norm"], 0, w["w_kv"], 0, ((0, dk), (dk, dk), (0, dk), (dk, dk)),
                           (F32, F32, BF16, BF16))
                (k_new, v_new, k16, v16), (q,) = norm_matmul(x, (kv_proj, q_proj), tm_mlp)
                tables = bias_table(w["rel_bias"], w["attn_sinks"], WINDOW if prompt else seq,
                                    2 * WINDOW, PROMPT_SINK_KEY if prompt else WINDOW + seq)
            else:
                ((q,),) = norm_matmul(x, (q_proj,), tm_mlp)
            if prompt:
                a = attn_prompt(q, k16, v16, tables[j], batch, seq)
            elif j == 0:
                a, win_k, win_v = attn_sample(q, k_new, v_new, cache[0], cache[1], tables[j],
                                              batch, seq, True)
            else:
                (a,) = attn_sample(q, k_new, v_new, cache[0], cache[1], tables[j],
                                   batch, seq, False)
            w_o, lo = w["w_attn_out"], j
        x = mlp(a, w_o, lo, x, w["norm_ffn"], w["w_up"], w["w_down"], l, w["final_norm"],
                l == depth - 1, tm_mlp, MLP_FF_BLOCK if prompt else MLP_FF_CHUNK)
    if prompt:
        dk = KV_HEADS * HEAD_DIM
        win_k = k_new.reshape(batch, seq, dk)[:, -WINDOW:]
        win_v = v_new.reshape(batch, seq, dk)[:, -WINDOW:]
        c_stack = jnp.stack(cs)
    shp = (batch, WINDOW, KV_HEADS, HEAD_DIM)
    return (x.reshape(batch, seq, d), c_stack, jnp.stack(ns), jnp.stack(ms),
            win_k.reshape(shp), win_v.reshape(shp))


def kernel(x_prompt, x_sample, state_C, state_n, state_m, cache_k, cache_v, norm_mix, norm_ffn,
           w_mlstm_in, b_igate, b_fgate, mlstm_norm, w_mlstm_out, kv_norm, w_kv, w_q, attn_sinks,
           w_attn_out, rel_bias, w_up, w_down, final_norm):
    n_b, d = w_q.shape[0], w_q.shape[1]
    heads = (KV_HEADS, GROUP, HEAD_DIM)
    w_q_perm = (w_q * HEAD_DIM ** -0.5).astype(BF16).reshape((n_b, d) + heads)
    w_q_perm = w_q_perm.transpose(0, 1, 3, 2, 4).reshape(n_b, d, d)
    w_ao_perm = w_attn_out.astype(BF16).reshape((n_b,) + heads + (d,))
    w_ao_perm = w_ao_perm.transpose(0, 2, 1, 3, 4).reshape(n_b, d, d)
    w = dict(norm_mix=norm_mix[:, None, :], norm_ffn=norm_ffn[:, None, :],
             w_in=w_mlstm_in.astype(BF16), b_igate=b_igate, b_fgate=b_fgate,
             mlstm_norm=mlstm_norm, w_mlstm_out=w_mlstm_out.astype(BF16),
             kv_norm=kv_norm[None, None, :], w_kv=w_kv.astype(BF16)[None],
             w_q=w_q_perm, attn_sinks=attn_sinks, w_attn_out=w_ao_perm, rel_bias=rel_bias,
             w_up=w_up.astype(BF16), w_down=w_down.astype(BF16), final_norm=final_norm)
    y_p, c_p, n_p, m_p, k_p, v_p = _trunk(x_prompt, None, None, w, 512, 1024)
    y_s, c_s, n_s, m_s, k_s, v_s = _trunk(x_sample, (state_C, state_n, state_m),
                                          (cache_k, cache_v), w, 512, 512)
    return (y_p, y_s, c_p, n_p, m_p, k_p, v_p, c_s, n_s, m_s, k_s, v_s)
```

```python
import functools
import math

import jax
import jax.numpy as jnp
from jax import lax
from jax.experimental import pallas as pl
from jax.experimental.pallas import tpu as pltpu

F32 = jnp.float32
BF16 = jnp.bfloat16
EPS = 1e-6
LOG2_E = 1.4426950408889634
NEG_INF = float("-inf")

N_HEADS = 8
DQK = 64
DV = 128
Q_HEADS = 16
KV_HEADS = 4
GROUP = Q_HEADS // KV_HEADS
HEAD_DIM = 64
WINDOW = 128
N_BUCKETS = 32
MAX_EXACT = N_BUCKETS // 2
MAX_DISTANCE = 128

LANES = 128
VMEM_LIMIT = 48 * 1024 * 1024
MLP_VMEM_LIMIT = 56 * 1024 * 1024

PROMPT_CHUNK = 128
PROMPT_HEAD_GROUP = 4
PROMPT_BLOCK = 512
SAMPLE_BATCH_BLOCK = 16
ATTN_SAMPLE_BLOCK = 16
ATTN_PROMPT_BLOCKS = 8
MLP_ROW_CHUNK = 512
MLP_FF_BLOCK = 2048
MLP_FF_CHUNK = 512
MLP_FF_BLOCK_SAMPLE = 1024


def _cparams(*sem, vmem_limit=VMEM_LIMIT):
    return pltpu.CompilerParams(dimension_semantics=sem, vmem_limit_bytes=vmem_limit)


def _rms(x, g):
    return x * lax.rsqrt(jnp.mean(x * x, axis=-1, keepdims=True) + EPS) * g


def _split3(x):
    hi = x.astype(BF16)
    r = x - hi.astype(F32)
    mid = r.astype(BF16)
    lo = (r - mid.astype(F32)).astype(BF16)
    return hi, mid, lo


def _log_sigmoid(x):
    return jnp.minimum(x, 0.0) - jnp.log1p(jnp.exp(-jnp.abs(x)))


def _sigmoid(x):
    return 1.0 / (1.0 + jnp.exp(-x))


def _norm_mm_kernel(x_ref, *refs, splits):
    n_proj = len(splits)
    out_refs = iter(refs[2 * n_proj:])
    x = x_ref[...]
    xhat = x * lax.rsqrt(jnp.mean(x * x, axis=-1, keepdims=True) + EPS)
    ys = [(xhat * refs[2 * p][...]).astype(BF16) for p in range(n_proj)]
    rs = [jnp.dot(ys[p], refs[2 * p + 1][...], preferred_element_type=F32) for p in range(n_proj)]
    for p in range(n_proj):
        for off, n in splits[p]:
            o_ref = next(out_refs)
            o_ref[...] = rs[p][:, off:off + n].astype(o_ref.dtype)


def _layer_spec(arr, layer):
    idx = (layer,) + (0,) * (arr.ndim - 1)
    return pl.BlockSpec((None,) + arr.shape[1:], lambda *_: idx)


def norm_matmul(x, projections, tm):
    m, d = x.shape
    assert m % tm == 0
    params, param_specs, out_specs, out_shapes = [], [], [], []
    for g, lg, w, lw, splits, dtypes in projections:
        assert all(off + n <= w.shape[-1] for off, n in splits)
        params += [g, w]
        param_specs += [_layer_spec(g, lg), _layer_spec(w, lw)]
        out_specs += [pl.BlockSpec((tm, n), lambda i: (i, 0)) for _, n in splits]
        out_shapes += [jax.ShapeDtypeStruct((m, n), dt) for (_, n), dt in zip(splits, dtypes)]
    outs = pl.pallas_call(
        functools.partial(_norm_mm_kernel, splits=tuple(p[4] for p in projections)),
        grid=(m // tm,),
        in_specs=[pl.BlockSpec((tm, d), lambda i: (i, 0))] + param_specs,
        out_specs=out_specs,
        out_shape=out_shapes,
        compiler_params=_cparams("parallel"),
        name="norm_matmul",
    )(x, *params)
    outs, grouped = list(outs), []
    for p in projections:
        grouped.append(outs[:len(p[4])])
        outs = outs[len(p[4]):]
    return grouped


def _mlp_kernel(a_ref, wo_ref, x_ref, g_ref, wu_ref, wd_ref, gf_ref, o_ref, xn_ref, *,
                final_norm, row_chunk, ff_chunk):
    j = pl.program_id(1)
    tm = x_ref.shape[0]

    @pl.when(j == 0)
    def _():
        for r in range(0, tm, row_chunk):
            rows = slice(r, r + row_chunk)
            x = x_ref[rows, :] + jnp.dot(a_ref[rows, :].astype(BF16), wo_ref[...],
                                         preferred_element_type=F32)
            xn_ref[rows, :] = _rms(x, g_ref[...]).astype(BF16)
            o_ref[rows, :] = x

    for c in range(0, wu_ref.shape[1], ff_chunk):
        h = jnp.dot(xn_ref[...], wu_ref[:, c:c + ff_chunk], preferred_element_type=F32)
        h = jnp.maximum(h, 0.0)
        o_ref[...] += jnp.dot((h * h).astype(BF16), wd_ref[c:c + ff_chunk, :],
                              preferred_element_type=F32)

    if final_norm:
        @pl.when(j == pl.num_programs(1) - 1)
        def _():
            o_ref[...] = _rms(o_ref[...], gf_ref[...])


def mlp(a, w_o, lo, x, g, w_up, w_down, l, g_final, final_norm, tm, tf):
    m, d = x.shape
    ff = w_up.shape[-1]
    return pl.pallas_call(
        functools.partial(_mlp_kernel, final_norm=final_norm, row_chunk=min(tm, MLP_ROW_CHUNK),
                          ff_chunk=MLP_FF_CHUNK),
        grid=(m // tm, ff // tf),
        in_specs=[pl.BlockSpec((tm, d), lambda i, j: (i, 0)),
                  _layer_spec(w_o, lo),
                  pl.BlockSpec((tm, d), lambda i, j: (i, 0)),
                  _layer_spec(g, l),
                  pl.BlockSpec((None, d, tf), lambda i, j: (l, 0, j)),
                  pl.BlockSpec((None, tf, d), lambda i, j: (l, j, 0)),
                  pl.BlockSpec((1, d), lambda i, j: (0, 0))],
        out_specs=pl.BlockSpec((tm, d), lambda i, j: (i, 0)),
        out_shape=jax.ShapeDtypeStruct((m, d), F32),
        scratch_shapes=[pltpu.VMEM((tm, d), BF16)],
        compiler_params=_cparams("parallel", "arbitrary",
                                 vmem_limit=MLP_VMEM_LIMIT if tf > MLP_FF_BLOCK_SAMPLE else VMEM_LIMIT),
        name="mlp",
    )(a, w_o, x, g, w_up, w_down, g_final.reshape(1, d))


def _mlstm_inproj_kernel(x_ref, g_ref, w_ref, q_ref, k_ref, v_ref, o_ref, gate_ref, *, k_transposed):
    hq, hv = N_HEADS * DQK, N_HEADS * DV
    tm = x_ref.shape[0]
    y = _rms(x_ref[...], g_ref[...]).astype(BF16)
    r = jnp.dot(y, w_ref[...], preferred_element_type=F32)
    q_ref[...] = r[:, :hq].astype(q_ref.dtype)
    k = r[:, hq:2 * hq] * DQK ** -0.5
    k_ref[...] = (k.T if k_transposed else k).astype(k_ref.dtype)
    v_ref[...] = r[:, 2 * hq:2 * hq + hv].astype(v_ref.dtype)
    o_ref[...] = r[:, 2 * hq + hv:2 * hq + 2 * hv]
    gate_ref[...] = jnp.concatenate(
        [r[:, 2 * hq + 2 * hv:], jnp.zeros((tm, LANES - 2 * N_HEADS), F32)], axis=1)


def mlstm_inproj(x, g, w, l, tm, k_transposed, act):
    m, d = x.shape
    hq, hv = N_HEADS * DQK, N_HEADS * DV
    assert w.shape[-1] == 2 * hq + 2 * hv + 2 * N_HEADS
    row = lambda width: pl.BlockSpec((tm, width), lambda i: (i, 0))
    k_spec = pl.BlockSpec((hq, tm), lambda i: (0, i)) if k_transposed else row(hq)
    k_shape = (hq, m) if k_transposed else (m, hq)
    return pl.pallas_call(
        functools.partial(_mlstm_inproj_kernel, k_transposed=k_transposed),
        grid=(m // tm,),
        in_specs=[row(d), _layer_spec(g, l), _layer_spec(w, l)],
        out_specs=[row(hq), k_spec, row(hv), row(hv), row(LANES)],
        out_shape=[jax.ShapeDtypeStruct((m, hq), act), jax.ShapeDtypeStruct(k_shape, act),
                   jax.ShapeDtypeStruct((m, hv), act), jax.ShapeDtypeStruct((m, hv), F32),
                   jax.ShapeDtypeStruct((m, LANES), F32)],
        compiler_params=_cparams("parallel"),
        name="mlstm_inproj",
    )(x, g, w)


def _chunk_scan(x, pos, op, fill, length):
    k = 1
    while k < length:
        x = op(x, jnp.where(pos >= k, pltpu.roll(x, k, 0), fill))
        k *= 2
    return x


def _mlstm_prompt_kernel(q_ref, kt_ref, v_ref, o_ref, gate_ref, bi_ref, bf_ref, gain_ref,
                         hg_ref, cx_out_ref, m_out_ref, cx_s, m_s, *, chunk):
    t_blk = q_ref.shape[1]
    n_chunks = t_blk // chunk
    step = pl.program_id(1)

    @pl.when(step == 0)
    def _():
        cx_s[...] = jnp.zeros_like(cx_s)
        m_s[...] = jnp.zeros_like(m_s)

    gates = gate_ref[0]
    li = gates + bi_ref[...]
    lf = _log_sigmoid(pltpu.roll(gates, LANES - N_HEADS, 1) + bf_ref[...])
    pos = lax.broadcasted_iota(jnp.int32, (t_blk, 1), 0) % chunk
    b = _chunk_scan(lf, pos, jnp.add, 0.0, chunk)
    c = li - b
    cm = _chunk_scan(c, pos, jnp.maximum, NEG_INF, chunk)

    m_prev = m_s[...]
    xs, w_inters, e_negms, wks = [], [], [], []
    for ck in range(n_chunks):
        sl = slice(ck * chunk, (ck + 1) * chunk)
        m_t = b[sl] + jnp.maximum(m_prev, cm[sl])
        m_new = m_t[chunk - 1:chunk]
        b_last = b[(ck + 1) * chunk - 1:(ck + 1) * chunk]
        xs.append((b[sl] - m_t) * LOG2_E)
        w_inters.append(jnp.exp(b[sl] + m_prev - m_t))
        e_negms.append(jnp.exp(-m_t))
        wks.append(jnp.exp(c[sl] + (b_last - m_new)))
        m_prev = m_new
    m_s[...] = m_prev
    c_t = (c * LOG2_E).T
    wk_t = jnp.concatenate(wks, axis=0).T

    ri = lax.broadcasted_iota(jnp.int32, (chunk, chunk), 0)
    ci = lax.broadcasted_iota(jnp.int32, (chunk, chunk), 1)
    causal = ci <= ri
    ones_blk = jnp.ones((chunk, DV), BF16)
    mean_sq = jnp.full((DV, DV), 1.0 / DV, BF16)
    heads = range(N_HEADS)
    dot = functools.partial(jnp.dot, preferred_element_type=F32)

    cxs = {h: cx_s[h] for h in heads}
    head_groups = [heads[g:g + PROMPT_HEAD_GROUP] for g in range(0, N_HEADS, PROMPT_HEAD_GROUP)]
    for ck in range(n_chunks):
        sl = slice(ck * chunk, (ck + 1) * chunk)
        for hs in head_groups:
            qs = {h: q_ref[0, sl, h * DQK:(h + 1) * DQK] for h in hs}
            kts = {h: kt_ref[h * DQK:(h + 1) * DQK, sl] for h in hs}
            vs = {h: v_ref[0, sl, h * DV:(h + 1) * DV] for h in hs}
            cx16s = {h: cxs[h].astype(BF16) for h in hs}
            kws = {h: (kts[h].astype(F32) * wk_t[h:h + 1, sl]).astype(BF16) for h in hs}
            vones = {h: jnp.concatenate([vs[h], ones_blk], axis=1) for h in hs}
            ss = {h: dot(qs[h], kts[h]) for h in hs}
            qcs = {h: dot(qs[h], cx16s[h]) for h in hs}
            dcs = {h: dot(kws[h], vones[h]) for h in hs}
            for h in hs:
                cxs[h] = w_inters[ck][chunk - 1:chunk, h:h + 1] * cxs[h] + dcs[h]
            a16s = {}
            for h in hs:
                dm = xs[ck][:, h:h + 1] + c_t[h:h + 1, sl]
                a16s[h] = (ss[h] * jnp.exp2(jnp.where(causal, dm, NEG_INF))).astype(BF16)
            avs = {h: dot(a16s[h], vones[h]) for h in hs}
            hhs = {}
            for h in hs:
                wi = jnp.broadcast_to(w_inters[ck][:, h:h + 1], (chunk, DV))
                en = jnp.broadcast_to(e_negms[ck][:, h:h + 1], (chunk, DV))
                den = avs[h][:, DV:] + wi * qcs[h][:, DV:]
                inv = 1.0 / jnp.maximum(jnp.abs(den), en)
                hhs[h] = (avs[h][:, :DV] + qcs[h][:, :DV] * wi) * inv
            sq16s = {h: (hhs[h] * hhs[h]).astype(BF16) for h in hs}
            mss = {h: dot(sq16s[h], mean_sq) for h in hs}
            for h in hs:
                hn = hhs[h] * lax.rsqrt(mss[h] + EPS)
                hn = hn * gain_ref[:, h * DV:(h + 1) * DV]
                og = _sigmoid(o_ref[0, sl, h * DV:(h + 1) * DV])
                hg_ref[0, sl, h * DV:(h + 1) * DV] = (og * hn).astype(hg_ref.dtype)
    for h in heads:
        cx_s[h] = cxs[h]

    @pl.when(step == pl.num_programs(1) - 1)
    def _():
        cx_out_ref[0] = cx_s[...]
        m_out_ref[0] = m_s[...]


def mlstm_prompt(q, kt, v, o, gate, b_i, b_f, gain, batch, seq):
    hq, hv = N_HEADS * DQK, N_HEADS * DV
    t = PROMPT_BLOCK
    nblk = seq // t
    r3 = lambda a: a.reshape(batch, seq, a.shape[-1])
    pad8 = lambda a: jnp.pad(a.reshape(1, N_HEADS), ((0, 0), (0, LANES - N_HEADS)))
    tok = lambda w: pl.BlockSpec((1, t, w), lambda bb, s: (bb, s, 0))
    cst = lambda w: pl.BlockSpec((1, w), lambda bb, s: (0, 0))
    hg, cx, m_new = pl.pallas_call(
        functools.partial(_mlstm_prompt_kernel, chunk=PROMPT_CHUNK),
        grid=(batch, nblk),
        in_specs=[tok(hq), pl.BlockSpec((hq, t), lambda bb, s: (0, bb * nblk + s)),
                  tok(hv), tok(hv), tok(LANES), cst(LANES), cst(LANES), cst(hv)],
        out_specs=[tok(hv),
                   pl.BlockSpec((1, N_HEADS, DQK, 2 * DV), lambda bb, s: (bb, 0, 0, 0)),
                   pl.BlockSpec((1, 1, LANES), lambda bb, s: (bb, 0, 0))],
        out_shape=[jax.ShapeDtypeStruct((batch, seq, hv), BF16),
                   jax.ShapeDtypeStruct((batch, N_HEADS, DQK, 2 * DV), F32),
                   jax.ShapeDtypeStruct((batch, 1, LANES), F32)],
        scratch_shapes=[pltpu.VMEM((N_HEADS, DQK, 2 * DV), F32), pltpu.VMEM((1, LANES), F32)],
        compiler_params=_cparams("parallel", "arbitrary"),
        name="mlstm_prompt",
    )(r3(q), kt, r3(v), r3(o), r3(gate), pad8(b_i), pad8(b_f), gain.reshape(1, hv))
    return hg.reshape(batch * seq, hv), cx[..., :DV], cx[..., DV], m_new[:, 0, :N_HEADS]


def _mlstm_sample_kernel(q_ref, k_ref, v_ref, o_ref, gate_ref, m0_ref, n0_ref, c0_ref, c_carry_ref,
                         bi_ref, bf_ref, gain_ref, seg64_ref, seg128_ref, e64_ref, e128_ref,
                         hg_ref, c_out_ref, n_out_ref, m_out_ref,
                         qc_s, *, seq):
    del c_carry_ref
    spans_layers = len(c_out_ref.shape) == 5
    c_new_ref = c_out_ref.at[0] if spans_layers else c_out_ref
    if spans_layers:
        for later in range(1, c_out_ref.shape[0]):
            c_out_ref[later] = jnp.zeros(c_out_ref.shape[1:], F32)
    rows = q_ref.shape[0]
    n_b = rows // seq
    tpos = lax.broadcasted_iota(jnp.int32, (rows, 1), 0) % seq
    dot = functools.partial(jnp.dot, preferred_element_type=F32)

    def shift(x, d):
        return x if d == 0 else pltpu.roll(x, d, 0)

    def unshift(x, d):
        return x if d == 0 else pltpu.roll(x, rows - d, 0)

    def split_cat(x):
        return jnp.concatenate(_split3(x), axis=1)


    q = q_ref[...].astype(BF16).astype(F32)
    k = k_ref[...].astype(BF16).astype(F32)
    v = v_ref[...].astype(BF16).astype(F32)

    units = [(p, h) for p in range(n_b // 2) for h in range(N_HEADS)]
    prow = lambda p: slice(p * 2 * seq, (p + 1) * 2 * seq)
    q_pairs = [q[prow(p), h * DQK:(h + 1) * DQK].astype(BF16) for p, h in units]
    c_cats = [jnp.concatenate([c0_ref[2 * p, h], c0_ref[2 * p + 1, h]], axis=1).astype(BF16)
              for p, h in units]
    qcs = [dot(q_pairs[i], c_cats[i]) for i in range(len(units))]

    gates = gate_ref[...]
    li = gates + bi_ref[...]
    lf = _log_sigmoid(pltpu.roll(gates, LANES - N_HEADS, 1) + bf_ref[...])
    b = lf
    for d in range(1, seq):
        b = b + jnp.where(tpos >= d, shift(lf, d), 0.0)
    inter = b + m0_ref[...]
    dvals = []
    m_t = inter
    for d in range(seq):
        dd = jnp.where(tpos >= d, b - shift(b, d) + shift(li, d), NEG_INF)
        dvals.append(dd)
        m_t = jnp.maximum(m_t, dd)
    w_inter = jnp.exp(inter - m_t)
    w_intra = [jnp.exp(dd - m_t) for dd in dvals]
    last = lambda x: functools.reduce(
        lambda acc, d: jnp.where(tpos == seq - 1 - d, unshift(x, d), acc), range(1, seq), x)
    wk = jnp.exp(last(b) - b + li - last(m_t))

    qk_in = [split_cat(q * shift(k, d)) for d in range(seq)]
    qn_in = split_cat(q * n0_ref[...].astype(BF16).astype(F32))
    wk_in, wi_in = split_cat(wk), split_cat(w_inter)
    seg64, e64, e128 = seg64_ref[...], e64_ref[...], e128_ref[...]
    qks = [dot(x, seg64) for x in qk_in]
    qn = dot(qn_in, seg64)
    wk_exp = dot(wk_in, e64)
    decay64 = dot(wi_in, e64)
    decay_exp = dot(wi_in, e128)

    a_s = [qks[d] * w_intra[d] for d in range(seq)]
    e128_1 = e128[:LANES]
    a_exps = [dot(a.astype(BF16), e128_1) for a in a_s]
    den = w_inter * qn
    num = jnp.zeros((rows, N_HEADS * DV), F32)
    for d in range(seq):
        den = den + a_s[d]
        num = num + a_exps[d] * shift(v, d)

    kw = k * wk_exp
    ksum = kw
    for d in range(1, seq):
        ksum = ksum + shift(kw, d)
    n_rows = decay64 * n0_ref[...] + ksum
    own = ((lax.broadcasted_iota(jnp.int32, (2 * seq, 2 * DV), 0) < seq)
           == (lax.broadcasted_iota(jnp.int32, (2 * seq, 2 * DV), 1) < DV))
    kw_pairs = [kw[prow(p), h * DQK:(h + 1) * DQK].astype(BF16) for p, h in units]
    v2s = []
    for p, h in units:
        v_pair = v[prow(p), h * DV:(h + 1) * DV]
        v2s.append(jnp.where(own, jnp.concatenate([v_pair, v_pair], axis=1), 0.0).astype(BF16))
    sel3 = (lax.broadcasted_iota(jnp.int32, (n_b, 3 * rows), 1) % rows
            == lax.broadcasted_iota(jnp.int32, (n_b, 3 * rows), 0) * seq + (seq - 1))
    sel3 = jnp.where(sel3, 1.0, 0.0).astype(BF16)
    n_in = jnp.concatenate(_split3(n_rows), axis=0)
    m_in = jnp.concatenate(_split3(m_t), axis=0)
    d_cs = [lax.dot_general(kw_pairs[i], v2s[i], (((0,), (0,)), ((), ())),
                            preferred_element_type=F32) for i in range(len(units))]
    n_out_ref[...] = dot(sel3, n_in)
    m_out_ref[...] = dot(sel3, m_in)

    first = lax.broadcasted_iota(jnp.int32, (2 * seq, DV), 0) < seq
    for i, (p, h) in enumerate(units):
        r0 = p * 2 * seq
        qc_s[prow(p), h * DV:(h + 1) * DV] = jnp.where(first, qcs[i][:, :DV], qcs[i][:, DV:])
        dec_a = decay_exp[r0 + seq - 1:r0 + seq, h * DV:(h + 1) * DV]
        dec_b = decay_exp[r0 + 2 * seq - 1:r0 + 2 * seq, h * DV:(h + 1) * DV]
        c_new_ref[2 * p, h] = dec_a * c0_ref[2 * p, h] + d_cs[i][:, :DV]
        c_new_ref[2 * p + 1, h] = dec_b * c0_ref[2 * p + 1, h] + d_cs[i][:, DV:]

    num = num + qc_s[...] * decay_exp
    inv_scale = 1.0 / jnp.maximum(jnp.abs(den), jnp.exp(-m_t))
    hh = num * dot(split_cat(inv_scale), e128)
    ms = dot(split_cat(hh * hh), seg128_ref[...]) * (1.0 / DV)
    hn = hh * dot(split_cat(lax.rsqrt(ms + EPS)), e128) * gain_ref[...]
    hg_ref[...] = (_sigmoid(o_ref[...]) * hn).astype(hg_ref.dtype)


def mlstm_sample(q, k, v, o, gate, b_i, b_f, gain, state_c, c_carry, l, n0, m0, batch, seq):
    hq, hv = N_HEADS * DQK, N_HEADS * DV
    c_blk = pl.BlockSpec((None, SAMPLE_BATCH_BLOCK, N_HEADS, DQK, DV), lambda i: (l, i, 0, 0, 0))
    c_all = pl.BlockSpec((state_c.shape[0], SAMPLE_BATCH_BLOCK, N_HEADS, DQK, DV),
                         lambda i: (0, i, 0, 0, 0))
    rows = SAMPLE_BATCH_BLOCK * seq
    pad8 = lambda a: jnp.pad(a.reshape(1, N_HEADS), ((0, 0), (0, LANES - N_HEADS)))
    m0_rows = jnp.pad(jnp.repeat(m0, seq, axis=0), ((0, 0), (0, LANES - N_HEADS)))
    n0_rows = jnp.repeat(n0.reshape(batch, hq), seq, axis=0)
    lane = jnp.arange(LANES)
    seg64 = (jnp.arange(hq)[:, None] // DQK == lane[None, :]).astype(BF16)
    seg128 = (jnp.arange(hv)[:, None] // DV == lane[None, :]).astype(BF16)
    tok = lambda w: pl.BlockSpec((rows, w), lambda i: (i, 0))
    cst = lambda a: pl.BlockSpec(a.shape, lambda i: (0,) * a.ndim)
    x3 = lambda a: jnp.tile(a, (3, 1))
    consts = [pad8(b_i), pad8(b_f), gain.reshape(1, hv),
              x3(seg64), x3(seg128), x3(seg64.T), x3(seg128.T)]
    hg, c_new, n_new, m_new = pl.pallas_call(
        functools.partial(_mlstm_sample_kernel, seq=seq),
        grid=(batch // SAMPLE_BATCH_BLOCK,),
        in_specs=[tok(hq), tok(hq), tok(hv), tok(hv), tok(LANES), tok(LANES), tok(hq),
                  c_blk, pl.BlockSpec(memory_space=pl.ANY)]
                 + [cst(a) for a in consts],
        out_specs=[tok(hv), c_all if l == 0 else c_blk,
                   pl.BlockSpec((SAMPLE_BATCH_BLOCK, hq), lambda i: (i, 0)),
                   pl.BlockSpec((SAMPLE_BATCH_BLOCK, LANES), lambda i: (i, 0))],
        out_shape=[jax.ShapeDtypeStruct((batch * seq, hv), BF16),
                   jax.ShapeDtypeStruct(state_c.shape, F32),
                   jax.ShapeDtypeStruct((batch, hq), F32),
                   jax.ShapeDtypeStruct((batch, LANES), F32)],
        scratch_shapes=[pltpu.VMEM((rows, hv), F32)],
        input_output_aliases={8: 1} if l > 0 else {},
        compiler_params=_cparams("parallel"),
        name="mlstm_sample",
    )(q, k, v, o, gate, m0_rows, n0_rows, state_c, c_carry, *consts)
    return hg, c_new, n_new.reshape(batch, N_HEADS, DQK), m_new[:, :N_HEADS]


def _bias_table_kernel(rb_ref, sink_ref, o_ref, *, sink_col):
    n_layers, _, nq, ns = o_ref.shape
    col = lax.broadcasted_iota(jnp.int32, (nq, ns), 1)
    dist = lax.broadcasted_iota(jnp.int32, (nq, ns), 0) + WINDOW - col
    n = jnp.maximum(dist, 0)
    large = MAX_EXACT + jnp.floor(
        jnp.log(jnp.maximum(n, 1).astype(F32) / MAX_EXACT)
        / math.log(MAX_DISTANCE / MAX_EXACT) * (N_BUCKETS - MAX_EXACT)).astype(jnp.int32)
    large = jnp.minimum(large, N_BUCKETS - 1)
    bucket = jnp.where(n < MAX_EXACT, n, large)
    valid = (dist >= 0) & (dist < WINDOW)
    for h in range(Q_HEADS):
        acc = jnp.zeros((nq, ns), F32)
        for bkt in range(N_BUCKETS):
            acc = jnp.where(bucket == bkt, rb_ref[bkt, h], acc)
        acc = jnp.where(valid, acc, NEG_INF)
        for layer in range(n_layers):
            o_ref[layer, h] = jnp.where(col == sink_col, sink_ref[layer, h], acc)


def bias_table(rel_bias, sinks, nq, ns, sink_col):
    return pl.pallas_call(
        functools.partial(_bias_table_kernel, sink_col=sink_col),
        in_specs=[pl.BlockSpec(memory_space=pltpu.SMEM), pl.BlockSpec(memory_space=pltpu.SMEM)],
        out_specs=pl.BlockSpec(memory_space=pltpu.VMEM),
        out_shape=jax.ShapeDtypeStruct((sinks.shape[0], Q_HEADS, nq, ns), F32),
        name="bias_table",
    )(rel_bias, sinks)


PROMPT_SINK_KEY = 0


def _attn_prompt_kernel(q_ref, kp_ref, kc_ref, vp_ref, vc_ref, bias_ref, o_ref):
    dk = KV_HEADS * HEAD_DIM
    kvh = range(KV_HEADS)
    sink_row = lax.broadcasted_iota(jnp.int32, (2 * WINDOW, dk), 0) == PROMPT_SINK_KEY
    lane_head = lax.broadcasted_iota(jnp.int32, (2 * WINDOW, dk), 1) // HEAD_DIM
    out_head = lax.broadcasted_iota(jnp.int32, (GROUP * WINDOW, dk), 1) // HEAD_DIM
    zero = jnp.zeros((), BF16)
    first_of_sequence = pl.program_id(1) == 0
    for j in range(q_ref.shape[1] // WINDOW):
        rows = slice(j * WINDOW, (j + 1) * WINDOW)
        before = slice((j - 1) * WINDOW, j * WINDOW)
        k_prev, v_prev = (kp_ref[0], vp_ref[0]) if j == 0 else (kc_ref[0, before], vc_ref[0, before])
        variant = jnp.where(first_of_sequence, 0, 1) if j == 0 else 1
        k_all = jnp.where(sink_row, zero, jnp.concatenate([k_prev, kc_ref[0, rows]], axis=0))
        v_all = jnp.where(sink_row, zero, jnp.concatenate([v_prev, vc_ref[0, rows]], axis=0))
        k_bd = jnp.concatenate([jnp.where(lane_head == kh, k_all, zero) for kh in kvh], axis=0)
        v_bd = jnp.concatenate([jnp.where(lane_head == kh, v_all, zero) for kh in kvh], axis=0)
        q_cat = jnp.concatenate([q_ref[0, rows, g * dk:(g + 1) * dk] for g in range(GROUP)],
                                axis=0)
        s = lax.dot_general(q_cat, k_bd, (((1,), (1,)), ((), ())), preferred_element_type=F32)
        es, invs = [], []
        for kh in kvh:
            sk = s[:, kh * 2 * WINDOW:(kh + 1) * 2 * WINDOW] + bias_ref[variant, kh]
            e = jnp.exp(sk - jnp.max(sk, axis=1, keepdims=True))
            invs.append(1.0 / jnp.sum(e, axis=1, keepdims=True))
            es.append(e.astype(BF16))
        pv = jnp.dot(jnp.concatenate(es, axis=1), v_bd, preferred_element_type=F32)
        inv = invs[KV_HEADS - 1]
        for kh in range(KV_HEADS - 1):
            inv = jnp.where(out_head == kh, invs[kh], inv)
        o = pv * inv
        for g in range(GROUP):
            o_ref[0, rows, g * dk:(g + 1) * dk] = o[g * WINDOW:(g + 1) * WINDOW].astype(o_ref.dtype)


def attn_prompt(q, k, v, table, batch, seq):
    d = Q_HEADS * HEAD_DIM
    dk = KV_HEADS * HEAD_DIM
    nq = ATTN_PROMPT_BLOCKS
    r3 = lambda a: a.reshape(batch, seq, a.shape[-1])
    prev = pl.BlockSpec((1, WINDOW, dk), lambda b, i: (b, jnp.maximum(i * nq - 1, 0), 0))
    cur = pl.BlockSpec((1, nq * WINDOW, dk), lambda b, i: (b, i, 0))
    table = table.reshape(KV_HEADS, GROUP * WINDOW, 2 * WINDOW)
    key = jnp.arange(2 * WINDOW)
    no_prev = jnp.where((key < WINDOW) & (key != PROMPT_SINK_KEY), NEG_INF, table)
    tables = jnp.stack([no_prev, table])
    out = pl.pallas_call(
        _attn_prompt_kernel,
        grid=(batch, seq // (nq * WINDOW)),
        in_specs=[pl.BlockSpec((1, nq * WINDOW, d), lambda b, i: (b, i, 0)),
                  prev, cur, prev, cur,
                  pl.BlockSpec(tables.shape, lambda b, i: (0, 0, 0, 0))],
        out_specs=pl.BlockSpec((1, nq * WINDOW, d), lambda b, i: (b, i, 0)),
        out_shape=jax.ShapeDtypeStruct((batch, seq, d), BF16),
        compiler_params=_cparams("parallel", "arbitrary"),
        name="attn_prompt",
    )(r3(q), r3(k), r3(k), r3(v), r3(v), tables)
    return out.reshape(batch * seq, d)


SAMPLE_NEW_KEYS = 16


def _attn_sample_kernel(q_ref, kct_ref, vct_ref, kn_ref, vn_ref, knt_ref, vnt_ref, bias_c_ref,
                        bias_n_ref, o_ref, *cache_out_refs, seq):
    n_b = q_ref.shape[0]
    dk = KV_HEADS * HEAD_DIM
    pairs = [(b, kh) for b in range(n_b) for kh in range(KV_HEADS)]
    kslice = lambda kh: slice(kh * HEAD_DIM, (kh + 1) * HEAD_DIM)
    head_lanes = lambda kh, g: slice((g * KV_HEADS + kh) * HEAD_DIM, (g * KV_HEADS + kh + 1) * HEAD_DIM)
    pad = jnp.zeros((SAMPLE_NEW_KEYS - seq, dk), F32)
    nt = (((1,), (1,)), ((), ()))
    dot = functools.partial(jnp.dot, preferred_element_type=F32)
    dot_nt = functools.partial(lax.dot_general, dimension_numbers=nt, preferred_element_type=F32)

    if cache_out_refs:
        lane = lax.broadcasted_iota(jnp.int32, (HEAD_DIM, WINDOW), 1)
        steps_per_block = knt_ref.shape[1] // (n_b * seq)
        first_token = (pl.program_id(0) % steps_per_block) * (n_b * seq)
        for out_ref, old_ref, newt_ref in zip(cache_out_refs, (kct_ref, vct_ref), (knt_ref, vnt_ref)):
            for b, kh in pairs:
                moved = pltpu.roll(old_ref[b, kh], WINDOW - seq, 1)
                shift = (2 * WINDOW - seq - first_token - b * seq) % WINDOW
                fresh = pltpu.roll(newt_ref[kslice(kh), :], shift, 1)
                out_ref[b, kh] = jnp.where(lane >= WINDOW - seq, fresh, moved)

    qss = [jnp.concatenate([q_ref[b, :, head_lanes(kh, g)] for g in range(GROUP)],
                           axis=0).astype(BF16) for b, kh in pairs]
    kcts = [kct_ref[b, kh].astype(BF16) for b, kh in pairs]
    vcts = [vct_ref[b, kh].astype(BF16) for b, kh in pairs]
    kns = [jnp.concatenate([kn_ref[b], pad], axis=0).astype(BF16) for b in range(n_b)]
    vns = [jnp.concatenate([vn_ref[b], pad], axis=0).astype(BF16) for b in range(n_b)]
    s_cs = [dot(qss[i], kcts[i]) for i in range(len(pairs))]
    s_ns = [dot_nt(qss[i], kns[b][:, kslice(kh)]) for i, (b, kh) in enumerate(pairs)]
    e_cs, e_ns, denoms = [], [], []
    for i, (b, kh) in enumerate(pairs):
        s_c = s_cs[i] + bias_c_ref[kh]
        s_n = s_ns[i] + bias_n_ref[kh]
        m = jnp.maximum(jnp.max(s_c, axis=1, keepdims=True), jnp.max(s_n, axis=1, keepdims=True))
        e_c, e_n = jnp.exp(s_c - m), jnp.exp(s_n - m)
        denoms.append(jnp.sum(e_c, axis=1, keepdims=True) + jnp.sum(e_n, axis=1, keepdims=True))
        e_cs.append(e_c.astype(BF16))
        e_ns.append(e_n.astype(BF16))
    pv_cs = [dot_nt(e_cs[i], vcts[i]) for i in range(len(pairs))]
    pv_ns = [dot(e_ns[i], vns[b][:, kslice(kh)]) for i, (b, kh) in enumerate(pairs)]
    for i, (b, kh) in enumerate(pairs):
        o = (pv_cs[i] + pv_ns[i]) / denoms[i]
        for g in range(GROUP):
            o_ref[b, :, head_lanes(kh, g)] = o[g * seq:(g + 1) * seq].astype(o_ref.dtype)


def attn_sample(q, k_new, v_new, cache_k, cache_v, table, batch, seq, write_cache):
    d = Q_HEADS * HEAD_DIM
    dk = KV_HEADS * HEAD_DIM
    nbk = ATTN_SAMPLE_BLOCK
    assert WINDOW % (nbk * seq) == 0
    new = pl.BlockSpec((nbk, seq, dk), lambda i: (i, 0, 0))
    old = pl.BlockSpec((nbk, KV_HEADS, HEAD_DIM, WINDOW), lambda i: (i, 0, 0, 0))
    newt = pl.BlockSpec((dk, WINDOW), lambda i: (0, i * nbk * seq // WINDOW))
    to_t = lambda c: jnp.transpose(c, (0, 2, 3, 1))
    table = table.reshape(KV_HEADS, GROUP * seq, 2 * WINDOW)
    n_cache = 2 if write_cache else 0
    out = pl.pallas_call(
        functools.partial(_attn_sample_kernel, seq=seq),
        grid=(batch // nbk,),
        in_specs=[pl.BlockSpec((nbk, seq, d), lambda i: (i, 0, 0)),
                  old, old, new, new, newt, newt,
                  pl.BlockSpec((KV_HEADS, GROUP * seq, WINDOW), lambda i: (0, 0, 0)),
                  pl.BlockSpec((KV_HEADS, GROUP * seq, SAMPLE_NEW_KEYS), lambda i: (0, 0, 0))],
        out_specs=[pl.BlockSpec((nbk, seq, d), lambda i: (i, 0, 0))] + [old] * n_cache,
        out_shape=[jax.ShapeDtypeStruct((batch, seq, d), F32)]
                  + [jax.ShapeDtypeStruct((batch, KV_HEADS, HEAD_DIM, WINDOW), F32)] * n_cache,
        compiler_params=_cparams("parallel"),
        name="attn_sample",
    )(q.reshape(batch, seq, d), to_t(cache_k), to_t(cache_v),
      k_new.reshape(batch, seq, dk), v_new.reshape(batch, seq, dk), k_new.T, v_new.T,
      table[:, :, :WINDOW], table[:, :, WINDOW:WINDOW + SAMPLE_NEW_KEYS])
    return (out[0].reshape(batch * seq, d),) + tuple(jnp.transpose(c, (0, 3, 1, 2)) for c in out[1:])


def _trunk(x, state, cache, w, tm, tm_mlp):
    batch, seq, d = x.shape
    x = x.reshape(batch * seq, d)
    hq, hv = N_HEADS * DQK, N_HEADS * DV
    depth = w["norm_mix"].shape[0]
    n_a = w["w_in"].shape[0]
    prompt = state is None
    act = BF16 if prompt else F32
    cs, ns, ms = [], [], []
    c_stack = None if prompt else state[0]
    for l in range(depth):
        if l < n_a:
            gates = (w["b_igate"][l], w["b_fgate"][l], w["mlstm_norm"][l])
            q, k, v, o, gate = mlstm_inproj(x, w["norm_mix"], w["w_in"], l, tm, prompt, act)
            if prompt:
                a, c_new, n_new, m_new = mlstm_prompt(q, k, v, o, gate, *gates, batch, seq)
                cs.append(c_new)
            else:
                a, c_stack, n_new, m_new = mlstm_sample(q, k, v, o, gate, *gates, state[0], c_stack,
                                                        l, state[1][l], state[2][l], batch, seq)
            ns.append(n_new); ms.append(m_new)
            w_o, lo = w["w_mlstm_out"], l
        else:
            j = l - n_a
            q_proj = (w["norm_mix"], l, w["w_q"], j, ((0, Q_HEADS * HEAD_DIM),), (act,))
            if j == 0:
                dk = KV_HEADS * HEAD_DIM
                kv_proj = (w["kv_norm"], 0, w["w_kv"], 0, ((0, dk), (dk, dk), (0, dk), (dk, dk)),
                           (F32, F32, BF16, BF16))
                (k_new, v_new, k16, v16), (q,) = norm_matmul(x, (kv_proj, q_proj), tm_mlp)
                tables = bias_table(w["rel_bias"], w["attn_sinks"], WINDOW if prompt else seq,
                                    2 * WINDOW, PROMPT_SINK_KEY if prompt else WINDOW + seq)
            else:
                ((q,),) = norm_matmul(x, (q_proj,), tm_mlp)
            if prompt:
                a = attn_prompt(q, k16, v16, tables[j], batch, seq)
            elif j == 0:
                a, win_k, win_v = attn_sample(q, k_new, v_new, cache[0], cache[1], tables[j],
                                              batch, seq, True)
            else:
                (a,) = attn_sample(q, k_new, v_new, cache[0], cache[1], tables[j],
                                   batch, seq, False)
            w_o, lo = w["w_attn_out"], j
        x = mlp(a, w_o, lo, x, w["norm_ffn"], w["w_up"], w["w_down"], l, w["final_norm"],
                l == depth - 1, tm_mlp, MLP_FF_BLOCK if prompt else MLP_FF_BLOCK_SAMPLE)
    if prompt:
        dk = KV_HEADS * HEAD_DIM
        win_k = k_new.reshape(batch, seq, dk)[:, -WINDOW:]
        win_v = v_new.reshape(batch, seq, dk)[:, -WINDOW:]
        c_stack = jnp.stack(cs)
    shp = (batch, WINDOW, KV_HEADS, HEAD_DIM)
    return (x.reshape(batch, seq, d), c_stack, jnp.stack(ns), jnp.stack(ms),
            win_k.reshape(shp), win_v.reshape(shp))


def kernel(x_prompt, x_sample, state_C, state_n, state_m, cache_k, cache_v, norm_mix, norm_ffn,
           w_mlstm_in, b_igate, b_fgate, mlstm_norm, w_mlstm_out, kv_norm, w_kv, w_q, attn_sinks,
           w_attn_out, rel_bias, w_up, w_down, final_norm):
    n_b, d = w_q.shape[0], w_q.shape[1]
    heads = (KV_HEADS, GROUP, HEAD_DIM)
    w_q_perm = (w_q * HEAD_DIM ** -0.5).astype(BF16).reshape((n_b, d) + heads)
    w_q_perm = w_q_perm.transpose(0, 1, 3, 2, 4).reshape(n_b, d, d)
    w_ao_perm = w_attn_out.astype(BF16).reshape((n_b,) + heads + (d,))
    w_ao_perm = w_ao_perm.transpose(0, 2, 1, 3, 4).reshape(n_b, d, d)
    w = dict(norm_mix=norm_mix[:, None, :], norm_ffn=norm_ffn[:, None, :],
             w_in=w_mlstm_in.astype(BF16), b_igate=b_igate, b_fgate=b_fgate,
             mlstm_norm=mlstm_norm, w_mlstm_out=w_mlstm_out.astype(BF16),
             kv_norm=kv_norm[None, None, :], w_kv=w_kv.astype(BF16)[None],
             w_q=w_q_perm, attn_sinks=attn_sinks, w_attn_out=w_ao_perm, rel_bias=rel_bias,
             w_up=w_up.astype(BF16), w_down=w_down.astype(BF16), final_norm=final_norm)
    y_p, c_p, n_p, m_p, k_p, v_p = _trunk(x_prompt, None, None, w, 512, 1024)
    y_s, c_s, n_s, m_s, k_s, v_s = _trunk(x_sample, (state_C, state_n, state_m),
                                          (cache_k, cache_v), w, 512, 512)
    return (y_p, y_s, c_p, n_p, m_p, k_p, v_p, c_s, n_s, m_s, k_s, v_s)
```

```python
import functools
import math

import jax
import jax.numpy as jnp
from jax import lax
from jax.experimental import pallas as pl
from jax.experimental.pallas import tpu as pltpu

F32 = jnp.float32
BF16 = jnp.bfloat16
EPS = 1e-6
LOG2_E = 1.4426950408889634
NEG_INF = float("-inf")

N_HEADS = 8
DQK = 64
DV = 128
Q_HEADS = 16
KV_HEADS = 4
GROUP = Q_HEADS // KV_HEADS
HEAD_DIM = 64
WINDOW = 128
N_BUCKETS = 32
MAX_EXACT = N_BUCKETS // 2
MAX_DISTANCE = 128

LANES = 128
VMEM_LIMIT = 48 * 1024 * 1024
MLP_VMEM_LIMIT = 56 * 1024 * 1024

PROMPT_CHUNK = 128
PROMPT_HEAD_GROUP = 4
PROMPT_BLOCK = 512
SAMPLE_BATCH_BLOCK = 16
ATTN_SAMPLE_BLOCK = 16
ATTN_PROMPT_BLOCKS = 8
MLP_ROW_CHUNK = 512
MLP_FF_BLOCK = 2048
MLP_FF_CHUNK = 1024


def _cparams(*sem, vmem_limit=VMEM_LIMIT):
    return pltpu.CompilerParams(dimension_semantics=sem, vmem_limit_bytes=vmem_limit)


def _rms(x, g):
    return x * lax.rsqrt(jnp.mean(x * x, axis=-1, keepdims=True) + EPS) * g


def _split3(x):
    hi = x.astype(BF16)
    r = x - hi.astype(F32)
    mid = r.astype(BF16)
    lo = (r - mid.astype(F32)).astype(BF16)
    return hi, mid, lo


def _log_sigmoid(x):
    return jnp.minimum(x, 0.0) - jnp.log1p(jnp.exp(-jnp.abs(x)))


def _sigmoid(x):
    return 1.0 / (1.0 + jnp.exp(-x))


def _norm_mm_kernel(x_ref, *refs, splits):
    n_proj = len(splits)
    out_refs = iter(refs[2 * n_proj:])
    x = x_ref[...]
    xhat = x * lax.rsqrt(jnp.mean(x * x, axis=-1, keepdims=True) + EPS)
    ys = [(xhat * refs[2 * p][...]).astype(BF16) for p in range(n_proj)]
    rs = [jnp.dot(ys[p], refs[2 * p + 1][...], preferred_element_type=F32) for p in range(n_proj)]
    for p in range(n_proj):
        for off, n in splits[p]:
            o_ref = next(out_refs)
            o_ref[...] = rs[p][:, off:off + n].astype(o_ref.dtype)


def _layer_spec(arr, layer):
    idx = (layer,) + (0,) * (arr.ndim - 1)
    return pl.BlockSpec((None,) + arr.shape[1:], lambda *_: idx, pipeline_mode=pl.Buffered(1))


def norm_matmul(x, projections, tm):
    m, d = x.shape
    assert m % tm == 0
    params, param_specs, out_specs, out_shapes = [], [], [], []
    for g, lg, w, lw, splits, dtypes in projections:
        assert all(off + n <= w.shape[-1] for off, n in splits)
        params += [g, w]
        param_specs += [_layer_spec(g, lg), _layer_spec(w, lw)]
        out_specs += [pl.BlockSpec((tm, n), lambda i: (i, 0)) for _, n in splits]
        out_shapes += [jax.ShapeDtypeStruct((m, n), dt) for (_, n), dt in zip(splits, dtypes)]
    outs = pl.pallas_call(
        functools.partial(_norm_mm_kernel, splits=tuple(p[4] for p in projections)),
        grid=(m // tm,),
        in_specs=[pl.BlockSpec((tm, d), lambda i: (i, 0))] + param_specs,
        out_specs=out_specs,
        out_shape=out_shapes,
        compiler_params=_cparams("parallel"),
        name="norm_matmul",
    )(x, *params)
    outs, grouped = list(outs), []
    for p in projections:
        grouped.append(outs[:len(p[4])])
        outs = outs[len(p[4]):]
    return grouped


def _mlp_kernel(a_ref, wo_ref, x_ref, g_ref, wu_ref, wd_ref, gf_ref, o_ref, xn_ref, *,
                final_norm, row_chunk, ff_chunk):
    j = pl.program_id(1)
    tm = x_ref.shape[0]

    @pl.when(j == 0)
    def _():
        for r in range(0, tm, row_chunk):
            rows = slice(r, r + row_chunk)
            x = x_ref[rows, :] + jnp.dot(a_ref[rows, :].astype(BF16), wo_ref[...],
                                         preferred_element_type=F32)
            xn_ref[rows, :] = _rms(x, g_ref[...]).astype(BF16)
            o_ref[rows, :] = x

    for c in range(0, wu_ref.shape[1], ff_chunk):
        h = jnp.dot(xn_ref[...], wu_ref[:, c:c + ff_chunk], preferred_element_type=F32)
        h = jnp.maximum(h, 0.0)
        o_ref[...] += jnp.dot((h * h).astype(BF16), wd_ref[c:c + ff_chunk, :],
                              preferred_element_type=F32)

    if final_norm:
        @pl.when(j == pl.num_programs(1) - 1)
        def _():
            o_ref[...] = _rms(o_ref[...], gf_ref[...])


def mlp(a, w_o, lo, x, g, w_up, w_down, l, g_final, final_norm, tm, tf):
    m, d = x.shape
    ff = w_up.shape[-1]
    return pl.pallas_call(
        functools.partial(_mlp_kernel, final_norm=final_norm, row_chunk=min(tm, MLP_ROW_CHUNK),
                          ff_chunk=MLP_FF_CHUNK),
        grid=(m // tm, ff // tf),
        in_specs=[pl.BlockSpec((tm, d), lambda i, j: (i, 0)),
                  _layer_spec(w_o, lo),
                  pl.BlockSpec((tm, d), lambda i, j: (i, 0)),
                  _layer_spec(g, l),
                  pl.BlockSpec((None, d, tf), lambda i, j: (l, 0, j)),
                  pl.BlockSpec((None, tf, d), lambda i, j: (l, j, 0)),
                  pl.BlockSpec((1, d), lambda i, j: (0, 0))],
        out_specs=pl.BlockSpec((tm, d), lambda i, j: (i, 0)),
        out_shape=jax.ShapeDtypeStruct((m, d), F32),
        scratch_shapes=[pltpu.VMEM((tm, d), BF16)],
        compiler_params=_cparams("parallel", "arbitrary",
                                 vmem_limit=MLP_VMEM_LIMIT if tf > MLP_FF_CHUNK else VMEM_LIMIT),
        name="mlp",
    )(a, w_o, x, g, w_up, w_down, g_final.reshape(1, d))


def _mlstm_inproj_kernel(x_ref, g_ref, w_ref, q_ref, k_ref, v_ref, o_ref, gate_ref, *, k_transposed):
    hq, hv = N_HEADS * DQK, N_HEADS * DV
    tm = x_ref.shape[0]
    y = _rms(x_ref[...], g_ref[...]).astype(BF16)
    r = jnp.dot(y, w_ref[...], preferred_element_type=F32)
    q_ref[...] = r[:, :hq].astype(q_ref.dtype)
    k = r[:, hq:2 * hq] * DQK ** -0.5
    k_ref[...] = (k.T if k_transposed else k).astype(k_ref.dtype)
    v_ref[...] = r[:, 2 * hq:2 * hq + hv].astype(v_ref.dtype)
    o_ref[...] = r[:, 2 * hq + hv:2 * hq + 2 * hv]
    gate_ref[...] = jnp.concatenate(
        [r[:, 2 * hq + 2 * hv:], jnp.zeros((tm, LANES - 2 * N_HEADS), F32)], axis=1)


def mlstm_inproj(x, g, w, l, tm, k_transposed, act):
    m, d = x.shape
    hq, hv = N_HEADS * DQK, N_HEADS * DV
    assert w.shape[-1] == 2 * hq + 2 * hv + 2 * N_HEADS
    row = lambda width: pl.BlockSpec((tm, width), lambda i: (i, 0))
    k_spec = pl.BlockSpec((hq, tm), lambda i: (0, i)) if k_transposed else row(hq)
    k_shape = (hq, m) if k_transposed else (m, hq)
    return pl.pallas_call(
        functools.partial(_mlstm_inproj_kernel, k_transposed=k_transposed),
        grid=(m // tm,),
        in_specs=[row(d), _layer_spec(g, l), _layer_spec(w, l)],
        out_specs=[row(hq), k_spec, row(hv), row(hv), row(LANES)],
        out_shape=[jax.ShapeDtypeStruct((m, hq), act), jax.ShapeDtypeStruct(k_shape, act),
                   jax.ShapeDtypeStruct((m, hv), act), jax.ShapeDtypeStruct((m, hv), F32),
                   jax.ShapeDtypeStruct((m, LANES), F32)],
        compiler_params=_cparams("parallel"),
        name="mlstm_inproj",
    )(x, g, w)


def _chunk_scan(x, pos, op, fill, length):
    k = 1
    while k < length:
        x = op(x, jnp.where(pos >= k, pltpu.roll(x, k, 0), fill))
        k *= 2
    return x


def _mlstm_prompt_kernel(q_ref, kt_ref, v_ref, o_ref, gate_ref, bi_ref, bf_ref, gain_ref,
                         hg_ref, cx_out_ref, m_out_ref, cx_s, m_s, *, chunk):
    t_blk = q_ref.shape[1]
    n_chunks = t_blk // chunk
    step = pl.program_id(1)

    @pl.when(step == 0)
    def _():
        cx_s[...] = jnp.zeros_like(cx_s)
        m_s[...] = jnp.zeros_like(m_s)

    gates = gate_ref[0]
    li = gates + bi_ref[...]
    lf = _log_sigmoid(pltpu.roll(gates, LANES - N_HEADS, 1) + bf_ref[...])
    pos = lax.broadcasted_iota(jnp.int32, (t_blk, 1), 0) % chunk
    b = _chunk_scan(lf, pos, jnp.add, 0.0, chunk)
    c = li - b
    cm = _chunk_scan(c, pos, jnp.maximum, NEG_INF, chunk)

    m_prev = m_s[...]
    xs, w_inters, e_negms, wks = [], [], [], []
    for ck in range(n_chunks):
        sl = slice(ck * chunk, (ck + 1) * chunk)
        m_t = b[sl] + jnp.maximum(m_prev, cm[sl])
        m_new = m_t[chunk - 1:chunk]
        b_last = b[(ck + 1) * chunk - 1:(ck + 1) * chunk]
        xs.append((b[sl] - m_t) * LOG2_E)
        w_inters.append(jnp.exp(b[sl] + m_prev - m_t))
        e_negms.append(jnp.exp(-m_t))
        wks.append(jnp.exp(c[sl] + (b_last - m_new)))
        m_prev = m_new
    m_s[...] = m_prev
    c_t = (c * LOG2_E).T
    wk_t = jnp.concatenate(wks, axis=0).T

    ri = lax.broadcasted_iota(jnp.int32, (chunk, chunk), 0)
    ci = lax.broadcasted_iota(jnp.int32, (chunk, chunk), 1)
    causal = ci <= ri
    ones_blk = jnp.ones((chunk, DV), BF16)
    mean_sq = jnp.full((DV, DV), 1.0 / DV, BF16)
    heads = range(N_HEADS)
    dot = functools.partial(jnp.dot, preferred_element_type=F32)

    cxs = {h: cx_s[h] for h in heads}
    head_groups = [heads[g:g + PROMPT_HEAD_GROUP] for g in range(0, N_HEADS, PROMPT_HEAD_GROUP)]
    for ck in range(n_chunks):
        sl = slice(ck * chunk, (ck + 1) * chunk)
        for hs in head_groups:
            qs = {h: q_ref[0, sl, h * DQK:(h + 1) * DQK] for h in hs}
            kts = {h: kt_ref[h * DQK:(h + 1) * DQK, sl] for h in hs}
            vs = {h: v_ref[0, sl, h * DV:(h + 1) * DV] for h in hs}
            cx16s = {h: cxs[h].astype(BF16) for h in hs}
            kws = {h: (kts[h].astype(F32) * wk_t[h:h + 1, sl]).astype(BF16) for h in hs}
            vones = {h: jnp.concatenate([vs[h], ones_blk], axis=1) for h in hs}
            ss = {h: dot(qs[h], kts[h]) for h in hs}
            qcs = {h: dot(qs[h], cx16s[h]) for h in hs}
            dcs = {h: dot(kws[h], vones[h]) for h in hs}
            for h in hs:
                cxs[h] = w_inters[ck][chunk - 1:chunk, h:h + 1] * cxs[h] + dcs[h]
            a16s = {}
            for h in hs:
                dm = xs[ck][:, h:h + 1] + c_t[h:h + 1, sl]
                a16s[h] = (ss[h] * jnp.exp2(jnp.where(causal, dm, NEG_INF))).astype(BF16)
            avs = {h: dot(a16s[h], vones[h]) for h in hs}
            hhs = {}
            for h in hs:
                wi = jnp.broadcast_to(w_inters[ck][:, h:h + 1], (chunk, DV))
                en = jnp.broadcast_to(e_negms[ck][:, h:h + 1], (chunk, DV))
                den = avs[h][:, DV:] + wi * qcs[h][:, DV:]
                inv = 1.0 / jnp.maximum(jnp.abs(den), en)
                hhs[h] = (avs[h][:, :DV] + qcs[h][:, :DV] * wi) * inv
            sq16s = {h: (hhs[h] * hhs[h]).astype(BF16) for h in hs}
            mss = {h: dot(sq16s[h], mean_sq) for h in hs}
            for h in hs:
                hn = hhs[h] * lax.rsqrt(mss[h] + EPS)
                hn = hn * gain_ref[:, h * DV:(h + 1) * DV]
                og = _sigmoid(o_ref[0, sl, h * DV:(h + 1) * DV])
                hg_ref[0, sl, h * DV:(h + 1) * DV] = (og * hn).astype(hg_ref.dtype)
    for h in heads:
        cx_s[h] = cxs[h]

    @pl.when(step == pl.num_programs(1) - 1)
    def _():
        cx_out_ref[0] = cx_s[...]
        m_out_ref[0] = m_s[...]


def mlstm_prompt(q, kt, v, o, gate, b_i, b_f, gain, batch, seq):
    hq, hv = N_HEADS * DQK, N_HEADS * DV
    t = PROMPT_BLOCK
    nblk = seq // t
    r3 = lambda a: a.reshape(batch, seq, a.shape[-1])
    pad8 = lambda a: jnp.pad(a.reshape(1, N_HEADS), ((0, 0), (0, LANES - N_HEADS)))
    tok = lambda w: pl.BlockSpec((1, t, w), lambda bb, s: (bb, s, 0))
    cst = lambda w: pl.BlockSpec((1, w), lambda bb, s: (0, 0))
    hg, cx, m_new = pl.pallas_call(
        functools.partial(_mlstm_prompt_kernel, chunk=PROMPT_CHUNK),
        grid=(batch, nblk),
        in_specs=[tok(hq), pl.BlockSpec((hq, t), lambda bb, s: (0, bb * nblk + s)),
                  tok(hv), tok(hv), tok(LANES), cst(LANES), cst(LANES), cst(hv)],
        out_specs=[tok(hv),
                   pl.BlockSpec((1, N_HEADS, DQK, 2 * DV), lambda bb, s: (bb, 0, 0, 0)),
                   pl.BlockSpec((1, 1, LANES), lambda bb, s: (bb, 0, 0))],
        out_shape=[jax.ShapeDtypeStruct((batch, seq, hv), BF16),
                   jax.ShapeDtypeStruct((batch, N_HEADS, DQK, 2 * DV), F32),
                   jax.ShapeDtypeStruct((batch, 1, LANES), F32)],
        scratch_shapes=[pltpu.VMEM((N_HEADS, DQK, 2 * DV), F32), pltpu.VMEM((1, LANES), F32)],
        compiler_params=_cparams("parallel", "arbitrary"),
        name="mlstm_prompt",
    )(r3(q), kt, r3(v), r3(o), r3(gate), pad8(b_i), pad8(b_f), gain.reshape(1, hv))
    return hg.reshape(batch * seq, hv), cx[..., :DV], cx[..., DV], m_new[:, 0, :N_HEADS]


def _mlstm_sample_kernel(q_ref, k_ref, v_ref, o_ref, gate_ref, m0_ref, n0_ref, c0_ref, c_carry_ref,
                         bi_ref, bf_ref, gain_ref, seg64_ref, seg128_ref, e64_ref, e128_ref,
                         hg_ref, c_out_ref, n_out_ref, m_out_ref,
                         qc_s, *, seq):
    del c_carry_ref
    spans_layers = len(c_out_ref.shape) == 5
    c_new_ref = c_out_ref.at[0] if spans_layers else c_out_ref
    if spans_layers:
        for later in range(1, c_out_ref.shape[0]):
            c_out_ref[later] = jnp.zeros(c_out_ref.shape[1:], F32)
    rows = q_ref.shape[0]
    n_b = rows // seq
    tpos = lax.broadcasted_iota(jnp.int32, (rows, 1), 0) % seq
    dot = functools.partial(jnp.dot, preferred_element_type=F32)

    def shift(x, d):
        return x if d == 0 else pltpu.roll(x, d, 0)

    def unshift(x, d):
        return x if d == 0 else pltpu.roll(x, rows - d, 0)

    def split_cat(x):
        return jnp.concatenate(_split3(x), axis=1)


    q = q_ref[...].astype(BF16).astype(F32)
    k = k_ref[...].astype(BF16).astype(F32)
    v = v_ref[...].astype(BF16).astype(F32)

    units = [(p, h) for p in range(n_b // 2) for h in range(N_HEADS)]
    prow = lambda p: slice(p * 2 * seq, (p + 1) * 2 * seq)
    q_pairs = [q[prow(p), h * DQK:(h + 1) * DQK].astype(BF16) for p, h in units]
    c_cats = [jnp.concatenate([c0_ref[2 * p, h], c0_ref[2 * p + 1, h]], axis=1).astype(BF16)
              for p, h in units]
    qcs = [dot(q_pairs[i], c_cats[i]) for i in range(len(units))]

    gates = gate_ref[...]
    li = gates + bi_ref[...]
    lf = _log_sigmoid(pltpu.roll(gates, LANES - N_HEADS, 1) + bf_ref[...])
    b = lf
    for d in range(1, seq):
        b = b + jnp.where(tpos >= d, shift(lf, d), 0.0)
    inter = b + m0_ref[...]
    dvals = []
    m_t = inter
    for d in range(seq):
        dd = jnp.where(tpos >= d, b - shift(b, d) + shift(li, d), NEG_INF)
        dvals.append(dd)
        m_t = jnp.maximum(m_t, dd)
    w_inter = jnp.exp(inter - m_t)
    w_intra = [jnp.exp(dd - m_t) for dd in dvals]
    last = lambda x: functools.reduce(
        lambda acc, d: jnp.where(tpos == seq - 1 - d, unshift(x, d), acc), range(1, seq), x)
    wk = jnp.exp(last(b) - b + li - last(m_t))

    qk_in = [split_cat(q * shift(k, d)) for d in range(seq)]
    qn_in = split_cat(q * n0_ref[...].astype(BF16).astype(F32))
    wk_in, wi_in = split_cat(wk), split_cat(w_inter)
    seg64, e64, e128 = seg64_ref[...], e64_ref[...], e128_ref[...]
    qks = [dot(x, seg64) for x in qk_in]
    qn = dot(qn_in, seg64)
    wk_exp = dot(wk_in, e64)
    decay64 = dot(wi_in, e64)
    decay_exp = dot(wi_in, e128)

    a_s = [qks[d] * w_intra[d] for d in range(seq)]
    e128_1 = e128[:LANES]
    a_exps = [dot(a.astype(BF16), e128_1) for a in a_s]
    den = w_inter * qn
    num = jnp.zeros((rows, N_HEADS * DV), F32)
    for d in range(seq):
        den = den + a_s[d]
        num = num + a_exps[d] * shift(v, d)

    kw = k * wk_exp
    ksum = kw
    for d in range(1, seq):
        ksum = ksum + shift(kw, d)
    n_rows = decay64 * n0_ref[...] + ksum
    own = ((lax.broadcasted_iota(jnp.int32, (2 * seq, 2 * DV), 0) < seq)
           == (lax.broadcasted_iota(jnp.int32, (2 * seq, 2 * DV), 1) < DV))
    kw_pairs = [kw[prow(p), h * DQK:(h + 1) * DQK].astype(BF16) for p, h in units]
    v2s = []
    for p, h in units:
        v_pair = v[prow(p), h * DV:(h + 1) * DV]
        v2s.append(jnp.where(own, jnp.concatenate([v_pair, v_pair], axis=1), 0.0).astype(BF16))
    sel3 = (lax.broadcasted_iota(jnp.int32, (n_b, 3 * rows), 1) % rows
            == lax.broadcasted_iota(jnp.int32, (n_b, 3 * rows), 0) * seq + (seq - 1))
    sel3 = jnp.where(sel3, 1.0, 0.0).astype(BF16)
    n_in = jnp.concatenate(_split3(n_rows), axis=0)
    m_in = jnp.concatenate(_split3(m_t), axis=0)
    d_cs = [lax.dot_general(kw_pairs[i], v2s[i], (((0,), (0,)), ((), ())),
                            preferred_element_type=F32) for i in range(len(units))]
    n_out_ref[...] = dot(sel3, n_in)
    m_out_ref[...] = dot(sel3, m_in)

    first = lax.broadcasted_iota(jnp.int32, (2 * seq, DV), 0) < seq
    for i, (p, h) in enumerate(units):
        r0 = p * 2 * seq
        qc_s[prow(p), h * DV:(h + 1) * DV] = jnp.where(first, qcs[i][:, :DV], qcs[i][:, DV:])
        dec_a = decay_exp[r0 + seq - 1:r0 + seq, h * DV:(h + 1) * DV]
        dec_b = decay_exp[r0 + 2 * seq - 1:r0 + 2 * seq, h * DV:(h + 1) * DV]
        c_new_ref[2 * p, h] = dec_a * c0_ref[2 * p, h] + d_cs[i][:, :DV]
        c_new_ref[2 * p + 1, h] = dec_b * c0_ref[2 * p + 1, h] + d_cs[i][:, DV:]

    num = num + qc_s[...] * decay_exp
    inv_scale = 1.0 / jnp.maximum(jnp.abs(den), jnp.exp(-m_t))
    hh = num * dot(split_cat(inv_scale), e128)
    ms = dot(split_cat(hh * hh), seg128_ref[...]) * (1.0 / DV)
    hn = hh * dot(split_cat(lax.rsqrt(ms + EPS)), e128) * gain_ref[...]
    hg_ref[...] = (_sigmoid(o_ref[...]) * hn).astype(hg_ref.dtype)


def mlstm_sample(q, k, v, o, gate, b_i, b_f, gain, state_c, c_carry, l, n0, m0, batch, seq):
    hq, hv = N_HEADS * DQK, N_HEADS * DV
    c_blk = pl.BlockSpec((None, SAMPLE_BATCH_BLOCK, N_HEADS, DQK, DV), lambda i: (l, i, 0, 0, 0))
    c_all = pl.BlockSpec((state_c.shape[0], SAMPLE_BATCH_BLOCK, N_HEADS, DQK, DV),
                         lambda i: (0, i, 0, 0, 0))
    rows = SAMPLE_BATCH_BLOCK * seq
    pad8 = lambda a: jnp.pad(a.reshape(1, N_HEADS), ((0, 0), (0, LANES - N_HEADS)))
    m0_rows = jnp.pad(jnp.repeat(m0, seq, axis=0), ((0, 0), (0, LANES - N_HEADS)))
    n0_rows = jnp.repeat(n0.reshape(batch, hq), seq, axis=0)
    lane = jnp.arange(LANES)
    seg64 = (jnp.arange(hq)[:, None] // DQK == lane[None, :]).astype(BF16)
    seg128 = (jnp.arange(hv)[:, None] // DV == lane[None, :]).astype(BF16)
    tok = lambda w: pl.BlockSpec((rows, w), lambda i: (i, 0))
    cst = lambda a: pl.BlockSpec(a.shape, lambda i: (0,) * a.ndim)
    x3 = lambda a: jnp.tile(a, (3, 1))
    consts = [pad8(b_i), pad8(b_f), gain.reshape(1, hv),
              x3(seg64), x3(seg128), x3(seg64.T), x3(seg128.T)]
    hg, c_new, n_new, m_new = pl.pallas_call(
        functools.partial(_mlstm_sample_kernel, seq=seq),
        grid=(batch // SAMPLE_BATCH_BLOCK,),
        in_specs=[tok(hq), tok(hq), tok(hv), tok(hv), tok(LANES), tok(LANES), tok(hq),
                  c_blk, pl.BlockSpec(memory_space=pl.ANY)]
                 + [cst(a) for a in consts],
        out_specs=[tok(hv), c_all if l == 0 else c_blk,
                   pl.BlockSpec((SAMPLE_BATCH_BLOCK, hq), lambda i: (i, 0)),
                   pl.BlockSpec((SAMPLE_BATCH_BLOCK, LANES), lambda i: (i, 0))],
        out_shape=[jax.ShapeDtypeStruct((batch * seq, hv), BF16),
                   jax.ShapeDtypeStruct(state_c.shape, F32),
                   jax.ShapeDtypeStruct((batch, hq), F32),
                   jax.ShapeDtypeStruct((batch, LANES), F32)],
        scratch_shapes=[pltpu.VMEM((rows, hv), F32)],
        input_output_aliases={8: 1} if l > 0 else {},
        compiler_params=_cparams("parallel"),
        name="mlstm_sample",
    )(q, k, v, o, gate, m0_rows, n0_rows, state_c, c_carry, *consts)
    return hg, c_new, n_new.reshape(batch, N_HEADS, DQK), m_new[:, :N_HEADS]


def _bias_table_kernel(rb_ref, sink_ref, o_ref, *, sink_col):
    n_layers, _, nq, ns = o_ref.shape
    col = lax.broadcasted_iota(jnp.int32, (nq, ns), 1)
    dist = lax.broadcasted_iota(jnp.int32, (nq, ns), 0) + WINDOW - col
    n = jnp.maximum(dist, 0)
    large = MAX_EXACT + jnp.floor(
        jnp.log(jnp.maximum(n, 1).astype(F32) / MAX_EXACT)
        / math.log(MAX_DISTANCE / MAX_EXACT) * (N_BUCKETS - MAX_EXACT)).astype(jnp.int32)
    large = jnp.minimum(large, N_BUCKETS - 1)
    bucket = jnp.where(n < MAX_EXACT, n, large)
    valid = (dist >= 0) & (dist < WINDOW)
    for h in range(Q_HEADS):
        acc = jnp.zeros((nq, ns), F32)
        for bkt in range(N_BUCKETS):
            acc = jnp.where(bucket == bkt, rb_ref[bkt, h], acc)
        acc = jnp.where(valid, acc, NEG_INF)
        for layer in range(n_layers):
            o_ref[layer, h] = jnp.where(col == sink_col, sink_ref[layer, h], acc)


def bias_table(rel_bias, sinks, nq, ns, sink_col):
    return pl.pallas_call(
        functools.partial(_bias_table_kernel, sink_col=sink_col),
        in_specs=[pl.BlockSpec(memory_space=pltpu.SMEM), pl.BlockSpec(memory_space=pltpu.SMEM)],
        out_specs=pl.BlockSpec(memory_space=pltpu.VMEM),
        out_shape=jax.ShapeDtypeStruct((sinks.shape[0], Q_HEADS, nq, ns), F32),
        name="bias_table",
    )(rel_bias, sinks)


PROMPT_SINK_KEY = 0


def _attn_prompt_kernel(q_ref, kp_ref, kc_ref, vp_ref, vc_ref, bias_ref, o_ref):
    dk = KV_HEADS * HEAD_DIM
    kvh = range(KV_HEADS)
    sink_row = lax.broadcasted_iota(jnp.int32, (2 * WINDOW, dk), 0) == PROMPT_SINK_KEY
    lane_head = lax.broadcasted_iota(jnp.int32, (2 * WINDOW, dk), 1) // HEAD_DIM
    out_head = lax.broadcasted_iota(jnp.int32, (GROUP * WINDOW, dk), 1) // HEAD_DIM
    zero = jnp.zeros((), BF16)
    first_of_sequence = pl.program_id(1) == 0
    for j in range(q_ref.shape[1] // WINDOW):
        rows = slice(j * WINDOW, (j + 1) * WINDOW)
        before = slice((j - 1) * WINDOW, j * WINDOW)
        k_prev, v_prev = (kp_ref[0], vp_ref[0]) if j == 0 else (kc_ref[0, before], vc_ref[0, before])
        variant = jnp.where(first_of_sequence, 0, 1) if j == 0 else 1
        k_all = jnp.where(sink_row, zero, jnp.concatenate([k_prev, kc_ref[0, rows]], axis=0))
        v_all = jnp.where(sink_row, zero, jnp.concatenate([v_prev, vc_ref[0, rows]], axis=0))
        k_bd = jnp.concatenate([jnp.where(lane_head == kh, k_all, zero) for kh in kvh], axis=0)
        v_bd = jnp.concatenate([jnp.where(lane_head == kh, v_all, zero) for kh in kvh], axis=0)
        q_cat = jnp.concatenate([q_ref[0, rows, g * dk:(g + 1) * dk] for g in range(GROUP)],
                                axis=0)
        s = lax.dot_general(q_cat, k_bd, (((1,), (1,)), ((), ())), preferred_element_type=F32)
        es, invs = [], []
        for kh in kvh:
            sk = s[:, kh * 2 * WINDOW:(kh + 1) * 2 * WINDOW] + bias_ref[variant, kh]
            e = jnp.exp(sk - jnp.max(sk, axis=1, keepdims=True))
            invs.append(1.0 / jnp.sum(e, axis=1, keepdims=True))
            es.append(e.astype(BF16))
        pv = jnp.dot(jnp.concatenate(es, axis=1), v_bd, preferred_element_type=F32)
        inv = invs[KV_HEADS - 1]
        for kh in range(KV_HEADS - 1):
            inv = jnp.where(out_head == kh, invs[kh], inv)
        o = pv * inv
        for g in range(GROUP):
            o_ref[0, rows, g * dk:(g + 1) * dk] = o[g * WINDOW:(g + 1) * WINDOW].astype(o_ref.dtype)


def attn_prompt(q, k, v, table, batch, seq):
    d = Q_HEADS * HEAD_DIM
    dk = KV_HEADS * HEAD_DIM
    nq = ATTN_PROMPT_BLOCKS
    r3 = lambda a: a.reshape(batch, seq, a.shape[-1])
    prev = pl.BlockSpec((1, WINDOW, dk), lambda b, i: (b, jnp.maximum(i * nq - 1, 0), 0))
    cur = pl.BlockSpec((1, nq * WINDOW, dk), lambda b, i: (b, i, 0))
    table = table.reshape(KV_HEADS, GROUP * WINDOW, 2 * WINDOW)
    key = jnp.arange(2 * WINDOW)
    no_prev = jnp.where((key < WINDOW) & (key != PROMPT_SINK_KEY), NEG_INF, table)
    tables = jnp.stack([no_prev, table])
    out = pl.pallas_call(
        _attn_prompt_kernel,
        grid=(batch, seq // (nq * WINDOW)),
        in_specs=[pl.BlockSpec((1, nq * WINDOW, d), lambda b, i: (b, i, 0)),
                  prev, cur, prev, cur,
                  pl.BlockSpec(tables.shape, lambda b, i: (0, 0, 0, 0))],
        out_specs=pl.BlockSpec((1, nq * WINDOW, d), lambda b, i: (b, i, 0)),
        out_shape=jax.ShapeDtypeStruct((batch, seq, d), BF16),
        compiler_params=_cparams("parallel", "arbitrary"),
        name="attn_prompt",
    )(r3(q), r3(k), r3(k), r3(v), r3(v), tables)
    return out.reshape(batch * seq, d)


SAMPLE_NEW_KEYS = 16


def _attn_sample_kernel(q_ref, kct_ref, vct_ref, kn_ref, vn_ref, knt_ref, vnt_ref, bias_c_ref,
                        bias_n_ref, o_ref, *cache_out_refs, seq):
    n_b = q_ref.shape[0]
    dk = KV_HEADS * HEAD_DIM
    pairs = [(b, kh) for b in range(n_b) for kh in range(KV_HEADS)]
    kslice = lambda kh: slice(kh * HEAD_DIM, (kh + 1) * HEAD_DIM)
    head_lanes = lambda kh, g: slice((g * KV_HEADS + kh) * HEAD_DIM, (g * KV_HEADS + kh + 1) * HEAD_DIM)
    pad = jnp.zeros((SAMPLE_NEW_KEYS - seq, dk), F32)
    nt = (((1,), (1,)), ((), ()))
    dot = functools.partial(jnp.dot, preferred_element_type=F32)
    dot_nt = functools.partial(lax.dot_general, dimension_numbers=nt, preferred_element_type=F32)

    if cache_out_refs:
        lane = lax.broadcasted_iota(jnp.int32, (HEAD_DIM, WINDOW), 1)
        steps_per_block = knt_ref.shape[1] // (n_b * seq)
        first_token = (pl.program_id(0) % steps_per_block) * (n_b * seq)
        for out_ref, old_ref, newt_ref in zip(cache_out_refs, (kct_ref, vct_ref), (knt_ref, vnt_ref)):
            for b, kh in pairs:
                moved = pltpu.roll(old_ref[b, kh], WINDOW - seq, 1)
                shift = (2 * WINDOW - seq - first_token - b * seq) % WINDOW
                fresh = pltpu.roll(newt_ref[kslice(kh), :], shift, 1)
                out_ref[b, kh] = jnp.where(lane >= WINDOW - seq, fresh, moved)

    qss = [jnp.concatenate([q_ref[b, :, head_lanes(kh, g)] for g in range(GROUP)],
                           axis=0).astype(BF16) for b, kh in pairs]
    kcts = [kct_ref[b, kh].astype(BF16) for b, kh in pairs]
    vcts = [vct_ref[b, kh].astype(BF16) for b, kh in pairs]
    kns = [jnp.concatenate([kn_ref[b], pad], axis=0).astype(BF16) for b in range(n_b)]
    vns = [jnp.concatenate([vn_ref[b], pad], axis=0).astype(BF16) for b in range(n_b)]
    s_cs = [dot(qss[i], kcts[i]) for i in range(len(pairs))]
    s_ns = [dot_nt(qss[i], kns[b][:, kslice(kh)]) for i, (b, kh) in enumerate(pairs)]
    e_cs, e_ns, denoms = [], [], []
    for i, (b, kh) in enumerate(pairs):
        s_c = s_cs[i] + bias_c_ref[kh]
        s_n = s_ns[i] + bias_n_ref[kh]
        m = jnp.maximum(jnp.max(s_c, axis=1, keepdims=True), jnp.max(s_n, axis=1, keepdims=True))
        e_c, e_n = jnp.exp(s_c - m), jnp.exp(s_n - m)
        denoms.append(jnp.sum(e_c, axis=1, keepdims=True) + jnp.sum(e_n, axis=1, keepdims=True))
        e_cs.append(e_c.astype(BF16))
        e_ns.append(e_n.astype(BF16))
    pv_cs = [dot_nt(e_cs[i], vcts[i]) for i in range(len(pairs))]
    pv_ns = [dot(e_ns[i], vns[b][:, kslice(kh)]) for i, (b, kh) in enumerate(pairs)]
    for i, (b, kh) in enumerate(pairs):
        o = (pv_cs[i] + pv_ns[i]) / denoms[i]
        for g in range(GROUP):
            o_ref[b, :, head_lanes(kh, g)] = o[g * seq:(g + 1) * seq].astype(o_ref.dtype)


def attn_sample(q, k_new, v_new, cache_k, cache_v, table, batch, seq, write_cache):
    d = Q_HEADS * HEAD_DIM
    dk = KV_HEADS * HEAD_DIM
    nbk = ATTN_SAMPLE_BLOCK
    assert WINDOW % (nbk * seq) == 0
    new = pl.BlockSpec((nbk, seq, dk), lambda i: (i, 0, 0))
    old = pl.BlockSpec((nbk, KV_HEADS, HEAD_DIM, WINDOW), lambda i: (i, 0, 0, 0))
    newt = pl.BlockSpec((dk, WINDOW), lambda i: (0, i * nbk * seq // WINDOW))
    to_t = lambda c: jnp.transpose(c, (0, 2, 3, 1))
    table = table.reshape(KV_HEADS, GROUP * seq, 2 * WINDOW)
    n_cache = 2 if write_cache else 0
    out = pl.pallas_call(
        functools.partial(_attn_sample_kernel, seq=seq),
        grid=(batch // nbk,),
        in_specs=[pl.BlockSpec((nbk, seq, d), lambda i: (i, 0, 0)),
                  old, old, new, new, newt, newt,
                  pl.BlockSpec((KV_HEADS, GROUP * seq, WINDOW), lambda i: (0, 0, 0)),
                  pl.BlockSpec((KV_HEADS, GROUP * seq, SAMPLE_NEW_KEYS), lambda i: (0, 0, 0))],
        out_specs=[pl.BlockSpec((nbk, seq, d), lambda i: (i, 0, 0))] + [old] * n_cache,
        out_shape=[jax.ShapeDtypeStruct((batch, seq, d), F32)]
                  + [jax.ShapeDtypeStruct((batch, KV_HEADS, HEAD_DIM, WINDOW), F32)] * n_cache,
        compiler_params=_cparams("parallel"),
        name="attn_sample",
    )(q.reshape(batch, seq, d), to_t(cache_k), to_t(cache_v),
      k_new.reshape(batch, seq, dk), v_new.reshape(batch, seq, dk), k_new.T, v_new.T,
      table[:, :, :WINDOW], table[:, :, WINDOW:WINDOW + SAMPLE_NEW_KEYS])
    return (out[0].reshape(batch * seq, d),) + tuple(jnp.transpose(c, (0, 3, 1, 2)) for c in out[1:])


def _trunk(x, state, cache, w, tm, tm_mlp):
    batch, seq, d = x.shape
    x = x.reshape(batch * seq, d)
    hq, hv = N_HEADS * DQK, N_HEADS * DV
    depth = w["norm_mix"].shape[0]
    n_a = w["w_in"].shape[0]
    prompt = state is None
    act = BF16 if prompt else F32
    cs, ns, ms = [], [], []
    c_stack = None if prompt else state[0]
    for l in range(depth):
        if l < n_a:
            gates = (w["b_igate"][l], w["b_fgate"][l], w["mlstm_norm"][l])
            q, k, v, o, gate = mlstm_inproj(x, w["norm_mix"], w["w_in"], l, tm, prompt, act)
            if prompt:
                a, c_new, n_new, m_new = mlstm_prompt(q, k, v, o, gate, *gates, batch, seq)
                cs.append(c_new)
            else:
                a, c_stack, n_new, m_new = mlstm_sample(q, k, v, o, gate, *gates, state[0], c_stack,
                                                        l, state[1][l], state[2][l], batch, seq)
            ns.append(n_new); ms.append(m_new)
            w_o, lo = w["w_mlstm_out"], l
        else:
            j = l - n_a
            q_proj = (w["norm_mix"], l, w["w_q"], j, ((0, Q_HEADS * HEAD_DIM),), (act,))
            if j == 0:
                dk = KV_HEADS * HEAD_DIM
                kv_proj = (w["kv_norm"], 0, w["w_kv"], 0, ((0, dk), (dk, dk), (0, dk), (dk, dk)),
                           (F32, F32, BF16, BF16))
                (k_new, v_new, k16, v16), (q,) = norm_matmul(x, (kv_proj, q_proj), tm_mlp)
                tables = bias_table(w["rel_bias"], w["attn_sinks"], WINDOW if prompt else seq,
                                    2 * WINDOW, PROMPT_SINK_KEY if prompt else WINDOW + seq)
            else:
                ((q,),) = norm_matmul(x, (q_proj,), tm_mlp)
            if prompt:
                a = attn_prompt(q, k16, v16, tables[j], batch, seq)
            elif j == 0:
                a, win_k, win_v = attn_sample(q, k_new, v_new, cache[0], cache[1], tables[j],
                                              batch, seq, True)
            else:
                (a,) = attn_sample(q, k_new, v_new, cache[0], cache[1], tables[j],
                                   batch, seq, False)
            w_o, lo = w["w_attn_out"], j
        x = mlp(a, w_o, lo, x, w["norm_ffn"], w["w_up"], w["w_down"], l, w["final_norm"],
                l == depth - 1, tm_mlp, MLP_FF_BLOCK if prompt else MLP_FF_CHUNK)
    if prompt:
        dk = KV_HEADS * HEAD_DIM
        win_k = k_new.reshape(batch, seq, dk)[:, -WINDOW:]
        win_v = v_new.reshape(batch, seq, dk)[:, -WINDOW:]
        c_stack = jnp.stack(cs)
    shp = (batch, WINDOW, KV_HEADS, HEAD_DIM)
    return (x.reshape(batch, seq, d), c_stack, jnp.stack(ns), jnp.stack(ms),
            win_k.reshape(shp), win_v.reshape(shp))


def kernel(x_prompt, x_sample, state_C, state_n, state_m, cache_k, cache_v, norm_mix, norm_ffn,
           w_mlstm_in, b_igate, b_fgate, mlstm_norm, w_mlstm_out, kv_norm, w_kv, w_q, attn_sinks,
           w_attn_out, rel_bias, w_up, w_down, final_norm):
    n_b, d = w_q.shape[0], w_q.shape[1]
    heads = (KV_HEADS, GROUP, HEAD_DIM)
    w_q_perm = (w_q * HEAD_DIM ** -0.5).astype(BF16).reshape((n_b, d) + heads)
    w_q_perm = w_q_perm.transpose(0, 1, 3, 2, 4).reshape(n_b, d, d)
    w_ao_perm = w_attn_out.astype(BF16).reshape((n_b,) + heads + (d,))
    w_ao_perm = w_ao_perm.transpose(0, 2, 1, 3, 4).reshape(n_b, d, d)
    w = dict(norm_mix=norm_mix[:, None, :], norm_ffn=norm_ffn[:, None, :],
             w_in=w_mlstm_in.astype(BF16), b_igate=b_igate, b_fgate=b_fgate,
             mlstm_norm=mlstm_norm, w_mlstm_out=w_mlstm_out.astype(BF16),
             kv_norm=kv_norm[None, None, :], w_kv=w_kv.astype(BF16)[None],
             w_q=w_q_perm, attn_sinks=attn_sinks, w_attn_out=w_ao_perm, rel_bias=rel_bias,
             w_up=w_up.astype(BF16), w_down=w_down.astype(BF16), final_norm=final_norm)
    y_p, c_p, n_p, m_p, k_p, v_p = _trunk(x_prompt, None, None, w, 512, 1024)
    y_s, c_s, n_s, m_s, k_s, v_s = _trunk(x_sample, (state_C, state_n, state_m),
                                          (cache_k, cache_v), w, 512, 512)
    return (y_p, y_s, c_p, n_p, m_p, k_p, v_p, c_s, n_s, m_s, k_s, v_s)
```
